```python
import math
import jax
import jax.numpy as jnp
from jax import lax
import numpy as np

D_MODEL = 1024
BATCH = 16
SEQ = 2048
DEPTH = 1

GRID_W = 64
CTX_LEN = 256

DN_HEADS = 4
DN_DK = 128
DN_DV = 128
DN_QK = DN_HEADS * DN_DK
DN_WIDTH = DN_HEADS * DN_DV
DN_CONV_DIM = 2 * DN_QK + DN_WIDTH
SHORT_CONV = 7
CHUNK = 64

CF_CH = 512
CF_K = 31

D_MIX = DN_WIDTH + CF_CH
OFF_Z = DN_CONV_DIM
OFF_BA = OFF_Z + DN_WIDTH
OFF_CF = OFF_BA + 4 * DN_HEADS
D_IN = OFF_CF + 2 * CF_CH

N_EXPERTS = 64
TOP_K = 6
N_GROUPS = 8
TOPK_GROUPS = 4
D_EXPERT = 256
D_SHARED = 256
ROUTED_SCALE = 2.5
MOE_BLOCK = 128

EPS = 1e-6

kernel_name = 'hybrid_deltanet_conformer_moe_dit'


def rmsnorm(x, g):
    xf = x.astype(jnp.float32)
    y = xf * lax.rsqrt(jnp.mean(xf * xf, axis=-1, keepdims=True) + EPS)
    return (y * g.astype(jnp.float32)).astype(x.dtype)


def layernorm(x, g, b):
    xf = x.astype(jnp.float32)
    mu = jnp.mean(xf, axis=-1, keepdims=True)
    var = jnp.mean(jnp.square(xf - mu), axis=-1, keepdims=True)
    y = (xf - mu) * lax.rsqrt(var + EPS) * g.astype(jnp.float32) + b.astype(jnp.float32)
    return y.astype(x.dtype)


def l2norm(x):
    return x * lax.rsqrt(jnp.sum(x * x, axis=-1, keepdims=True) + EPS)


def modulate(h, shift, scale):
    return h * (1 + scale) + shift


def depthwise_conv_seq(x, w):
    k, ch = w.shape
    return lax.conv_general_dilated(x, w[:, None, :].astype(x.dtype), (1,), [(k // 2, k // 2)],
                                    dimension_numbers=('NWC', 'WIO', 'NWC'), feature_group_count=ch)


def depthwise_conv_grid_columns(x, w):
    bsz, n, ch = x.shape
    rows = n // GRID_W
    k = w.shape[0]
    y = lax.conv_general_dilated(x.reshape(bsz, rows, GRID_W, ch), w[:, None, None, :].astype(x.dtype),
                                 (1, 1), [(k // 2, k // 2), (0, 0)],
                                 dimension_numbers=('NHWC', 'HWIO', 'NHWC'), feature_group_count=ch)
    return y.reshape(bsz, n, ch)


def split_mixer_inputs(u):
    return u[..., :OFF_Z], u[..., OFF_Z:OFF_BA], u[..., OFF_BA:OFF_CF], u[..., OFF_CF:]


def delta_inputs(qkv, ba, conv_w, a_log, dt_bias):
    bsz, n, _ = qkv.shape
    qkv = jax.nn.silu(depthwise_conv_seq(qkv, conv_w)).astype(jnp.float32)
    q = l2norm(qkv[..., :DN_QK].reshape(bsz, n, DN_HEADS, DN_DK)) * (DN_DK ** -0.5)
    k = l2norm(qkv[..., DN_QK:2 * DN_QK].reshape(bsz, n, DN_HEADS, DN_DK))
    v = qkv[..., 2 * DN_QK:].reshape(bsz, n, DN_HEADS, DN_DV)
    ba = ba.astype(jnp.float32).reshape(bsz, n, 2, 2, DN_HEADS)
    beta = jax.nn.sigmoid(ba[:, :, :, 0])
    g = -jnp.exp(a_log.astype(jnp.float32)) * jax.nn.softplus(ba[:, :, :, 1] + dt_bias.astype(jnp.float32))
    return q, k, v, beta, g


def gated_delta_chunked(q, k, v, g, beta, s0):
    bsz, n, heads, dk = q.shape
    dv = v.shape[-1]
    nc = n // CHUNK

    def blocks(t):
        t = t.reshape(bsz, nc, CHUNK, heads, *t.shape[3:])
        return jnp.moveaxis(t, (1, 3), (0, 2))

    qc, kc, vc, bc = blocks(q), blocks(k), blocks(v), blocks(beta)
    gc = jnp.cumsum(blocks(g), axis=-1)
    tri_incl = jnp.tril(jnp.ones((CHUNK, CHUNK), dtype=bool))
    tri_strict = jnp.tril(jnp.ones((CHUNK, CHUNK), dtype=bool), -1)
    decay = jnp.exp(jnp.where(tri_incl, gc[..., :, None] - gc[..., None, :], -jnp.inf))
    kb = kc * bc[..., None]
    lmat = jnp.where(tri_strict, jnp.einsum('nbhik,nbhjk->nbhij', kb, kc) * decay, 0.0)
    eye = jnp.eye(CHUNK, dtype=lmat.dtype)
    tinv = lax.linalg.triangular_solve(eye + lmat, jnp.broadcast_to(eye, lmat.shape),
                                       left_side=True, lower=True, unit_diagonal=True)
    u = jnp.einsum('nbhij,nbhjv->nbhiv', tinv, vc * bc[..., None])
    w = jnp.einsum('nbhij,nbhjk->nbhik', tinv, kb * jnp.exp(gc)[..., None])
    intra = jnp.einsum('nbhik,nbhjk->nbhij', qc, kc) * decay
    qd = qc * jnp.exp(gc)[..., None]
    kd = kc * jnp.exp(gc[..., -1:] - gc)[..., None]
    glast = jnp.exp(gc[..., -1])

    def step(s, inp):
        u_i, w_i, a_i, qd_i, kd_i, gl_i = inp
        v_new = u_i - jnp.einsum('bhck,bhkv->bhcv', w_i, s)
        o = jnp.einsum('bhck,bhkv->bhcv', qd_i, s) + jnp.einsum('bhij,bhjv->bhiv', a_i, v_new)
        s = s * gl_i[..., None, None] + jnp.einsum('bhck,bhcv->bhkv', kd_i, v_new)
        return s, o

    s_fin, o = lax.scan(step, s0, (u, w, intra, qd, kd, glast))
    o = jnp.moveaxis(o, (0, 2), (1, 3)).reshape(bsz, n, heads, dv)
    return o, s_fin


def bidir_gated_delta(q, k, v, beta, g, s0_fwd, s0_bwd):
    o_f, s_f = gated_delta_chunked(q, k, v, g[:, :, 0], beta[:, :, 0], s0_fwd)
    fl = lambda t: jnp.flip(t, axis=1)
    o_b, s_b = gated_delta_chunked(fl(q), fl(k), fl(v), fl(g[:, :, 1]), fl(beta[:, :, 1]), s0_bwd)
    return o_f + fl(o_b), s_f, s_b


def gated_head_norm(o, z, norm_g):
    bsz, n = z.shape[0], z.shape[1]
    on = o * lax.rsqrt(jnp.mean(o * o, axis=-1, keepdims=True) + EPS) * norm_g.astype(jnp.float32)
    gate = jax.nn.silu(z.astype(jnp.float32)).reshape(bsz, n, DN_HEADS, DN_DV)
    return (on * gate).reshape(bsz, n, DN_WIDTH).astype(z.dtype)


def conformer_module(cf_in, conv_fn, dw_w, dw_b, ln_g, ln_b):
    y = cf_in[..., :CF_CH] * jax.nn.sigmoid(cf_in[..., CF_CH:])
    y = conv_fn(y, dw_w) + dw_b
    return jax.nn.silu(layernorm(y, ln_g, ln_b))


def swiglu(h, wg, wu, wd):
    return (jax.nn.silu(h @ wg) * (h @ wu)) @ wd


def moe_ffn(h, router_w, router_bias, w_gate, w_up, w_down, s_gate, s_up, s_down):
    bsz, n, d = h.shape
    t = bsz * n
    hf = h.reshape(t, d)
    scores = jax.nn.sigmoid((hf @ router_w).astype(jnp.float32))
    sel = scores + router_bias.astype(jnp.float32)
    grp_score = lax.top_k(sel.reshape(t, N_GROUPS, N_EXPERTS // N_GROUPS), 2)[0].sum(-1)
    _, grp_idx = lax.top_k(grp_score, TOPK_GROUPS)
    grp_keep = jnp.any(grp_idx[..., None] == jnp.arange(N_GROUPS), axis=-2)
    exp_keep = jnp.repeat(grp_keep, N_EXPERTS // N_GROUPS, axis=-1)
    _, top_idx = lax.top_k(jnp.where(exp_keep, sel, -jnp.inf), TOP_K)
    top_w = jnp.take_along_axis(scores, top_idx, axis=-1)
    top_w = top_w / jnp.sum(top_w, axis=-1, keepdims=True) * ROUTED_SCALE
    tk = t * TOP_K
    flat_e = top_idx.reshape(tk)
    order = jnp.argsort(flat_e, stable=True)
    sorted_e = flat_e[order]
    sorted_tok = (order // TOP_K).astype(jnp.int32)
    sorted_w = top_w.reshape(tk)[order]
    counts = jnp.bincount(flat_e, length=N_EXPERTS)
    padded = (counts + MOE_BLOCK - 1) // MOE_BLOCK * MOE_BLOCK
    start = jnp.cumsum(counts) - counts
    pad_end = jnp.cumsum(padded)
    pad_start = pad_end - padded
    dest = pad_start[sorted_e] + jnp.arange(tk) - start[sorted_e]
    n_blocks = -(-tk // MOE_BLOCK) + N_EXPERTS
    n_slots = n_blocks * MOE_BLOCK
    slot_tok = jnp.full((n_slots,), t, jnp.int32).at[dest].set(sorted_tok)
    slot_w = jnp.zeros((n_slots,), jnp.float32).at[dest].set(sorted_w)
    block_exp = jnp.minimum(jnp.searchsorted(pad_end, jnp.arange(n_blocks) * MOE_BLOCK, side='right'),
                            N_EXPERTS - 1)
    h_pad = jnp.concatenate([hf, jnp.zeros((1, d), hf.dtype)], axis=0)

    def expert_block(args):
        tok, e = args
        return swiglu(h_pad[tok], w_gate[e], w_up[e], w_down[e])

    out = lax.map(expert_block, (slot_tok.reshape(n_blocks, MOE_BLOCK), block_exp))
    weighted = (out.reshape(n_slots, d) * slot_w[:, None]).astype(h.dtype)
    routed = jnp.zeros((t + 1, d), h.dtype).at[slot_tok].add(weighted)[:t]
    return (routed + swiglu(hf, s_gate, s_up, s_down)).reshape(bsz, n, d)


def setup_inputs(seed: int = 0) -> dict:
    key = jax.random.key(seed)
    ks = jax.random.split(key, 32)
    f32 = jnp.float32
    L = DEPTH

    def nrm(k, shape, s):
        return jax.random.normal(k, shape, f32) * s

    dt = jnp.exp(jax.random.uniform(ks[11], (L, 2, DN_HEADS), f32, math.log(1e-3), math.log(1e-1)))
    return {
        'x': nrm(ks[0], (BATCH, SEQ, D_MODEL), 1.0),
        'c': nrm(ks[1], (BATCH, D_MODEL), 1.0),
        'ctx': nrm(ks[2], (BATCH, CTX_LEN, D_MODEL), 1.0),
        'c_ctx': nrm(ks[3], (D_MODEL,), 1.0),
        'w_mod': nrm(ks[4], (L, D_MODEL, 6 * D_MODEL), 0.5 * D_MODEL ** -0.5),
        'b_mod': nrm(ks[5], (L, 6 * D_MODEL), 0.02),
        'g_mix': 1.0 + nrm(ks[6], (L, D_MODEL), 0.02),
        'g_ffn': 1.0 + nrm(ks[7], (L, D_MODEL), 0.02),
        'w_in': nrm(ks[8], (L, D_MODEL, D_IN), D_MODEL ** -0.5),
        'w_out': nrm(ks[9], (L, D_MIX, D_MODEL), D_MIX ** -0.5),
        'dn_conv_w': nrm(ks[10], (L, SHORT_CONV, DN_CONV_DIM), SHORT_CONV ** -0.5),
        'dn_a_log': jnp.log(jax.random.uniform(ks[12], (L, 2, DN_HEADS), f32, 1.0, 16.0)),
        'dn_dt_bias': dt + jnp.log(-jnp.expm1(-dt)),
        'dn_norm_g': 1.0 + nrm(ks[13], (L, DN_DV), 0.02),
        'cf_dw_w': nrm(ks[14], (L, CF_K, CF_CH), CF_K ** -0.5),
        'cf_dw_b': nrm(ks[15], (L, CF_CH), 0.02),
        'cf_ln_g': 1.0 + nrm(ks[16], (L, CF_CH), 0.02),
        'cf_ln_b': nrm(ks[17], (L, CF_CH), 0.02),
        'router_w': nrm(ks[18], (L, D_MODEL, N_EXPERTS), D_MODEL ** -0.5),
        'router_bias': nrm(ks[19], (L, N_EXPERTS), 0.01),
        'exp_w_gate': nrm(ks[20], (L, N_EXPERTS, D_MODEL, D_EXPERT), D_MODEL ** -0.5),
        'exp_w_up': nrm(ks[21], (L, N_EXPERTS, D_MODEL, D_EXPERT), D_MODEL ** -0.5),
        'exp_w_down': nrm(ks[22], (L, N_EXPERTS, D_EXPERT, D_MODEL), D_EXPERT ** -0.5),
        'sh_w_gate': nrm(ks[23], (L, D_MODEL, D_SHARED), D_MODEL ** -0.5),
        'sh_w_up': nrm(ks[24], (L, D_MODEL, D_SHARED), D_MODEL ** -0.5),
        'sh_w_down': nrm(ks[25], (L, D_SHARED, D_MODEL), D_SHARED ** -0.5),
        'g_final': 1.0 + nrm(ks[26], (D_MODEL,), 0.02),
    }


def reference(x, c, ctx, c_ctx, w_mod, b_mod, g_mix, g_ffn, w_in, w_out, dn_conv_w, dn_a_log, dn_dt_bias,
              dn_norm_g, cf_dw_w, cf_dw_b, cf_ln_g, cf_ln_b, router_w, router_bias, exp_w_gate, exp_w_up,
              exp_w_down, sh_w_gate, sh_w_up, sh_w_down, g_final):
    bsz = x.shape[0]
    for l in range(DEPTH):
        sh1, sc1, gt1, sh2, sc2, gt2 = jnp.split((jax.nn.silu(c) @ w_mod[l] + b_mod[l])[:, None, :], 6, axis=-1)
        csh1, csc1, cgt1, csh2, csc2, cgt2 = jnp.split(jax.nn.silu(c_ctx) @ w_mod[l] + b_mod[l], 6, axis=-1)

        qkv_c, z_c, ba_c, cf_c = split_mixer_inputs(modulate(rmsnorm(ctx, g_mix[l]), csh1, csc1) @ w_in[l])
        s0 = jnp.zeros((bsz, DN_HEADS, DN_DK, DN_DV), jnp.float32)
        o_c, s_fwd, s_bwd = bidir_gated_delta(
            *delta_inputs(qkv_c, ba_c, dn_conv_w[l], dn_a_log[l], dn_dt_bias[l]), s0, s0)

        qkv_x, z_x, ba_x, cf_x = split_mixer_inputs(modulate(rmsnorm(x, g_mix[l]), sh1, sc1) @ w_in[l])
        o_x, _, _ = bidir_gated_delta(
            *delta_inputs(qkv_x, ba_x, dn_conv_w[l], dn_a_log[l], dn_dt_bias[l]), s_fwd, s_bwd)
        heads_x = jnp.concatenate([
            gated_head_norm(o_x, z_x, dn_norm_g[l]),
            conformer_module(cf_x, depthwise_conv_grid_columns, cf_dw_w[l], cf_dw_b[l], cf_ln_g[l], cf_ln_b[l]),
        ], axis=-1)
        x = x + gt1 * (heads_x @ w_out[l])
        x = x + gt2 * moe_ffn(modulate(rmsnorm(x, g_ffn[l]), sh2, sc2), router_w[l], router_bias[l],
                              exp_w_gate[l], exp_w_up[l], exp_w_down[l], sh_w_gate[l], sh_w_up[l], sh_w_down[l])

        if l + 1 < DEPTH:
            heads_c = jnp.concatenate([
                gated_head_norm(o_c, z_c, dn_norm_g[l]),
                conformer_module(cf_c, depthwise_conv_seq, cf_dw_w[l], cf_dw_b[l], cf_ln_g[l], cf_ln_b[l]),
            ], axis=-1)
            ctx = ctx + cgt1 * (heads_c @ w_out[l])
            ctx = ctx + cgt2 * moe_ffn(modulate(rmsnorm(ctx, g_ffn[l]), csh2, csc2), router_w[l], router_bias[l],
                                       exp_w_gate[l], exp_w_up[l], exp_w_down[l],
                                       sh_w_gate[l], sh_w_up[l], sh_w_down[l])
    return rmsnorm(x, g_final)
```

```python
import functools

import jax
import jax.numpy as jnp
from jax import lax
from jax.experimental import pallas as pl
from jax.experimental.pallas import tpu as pltpu

F32 = jnp.float32
BF16 = jnp.bfloat16
HIGHEST = lax.Precision.HIGHEST

EPS = 1e-6
LANES = 128
GRID_W = 64
DN_HEADS = 4
DN_D = 128
DN_QK = DN_HEADS * DN_D
DN_WIDTH = DN_HEADS * DN_D
DN_CONV_DIM = 2 * DN_QK + DN_WIDTH
SHORT_CONV = 7
CHUNK = 64
CF_CH = 512
CF_K = 31
N_EXPERTS = 64
TOP_K = 6
N_GROUPS = 8
GROUP_SIZE = N_EXPERTS // N_GROUPS
TOPK_GROUPS = 4
ROUTED_SCALE = 2.5
HALO = 8
VMEM_LIMIT = 56 * 1024 * 1024


def _cparams(*sem):
    return pltpu.CompilerParams(dimension_semantics=sem, vmem_limit_bytes=VMEM_LIMIT)


def _silu(v):
    return v * jax.nn.sigmoid(v)


def _bdot(a, b):
    return jnp.dot(a.astype(BF16), b.astype(BF16), preferred_element_type=F32)


def _mod_kernel(c_ref, w_ref, b_ref, o_ref):
    o_ref[...] = jnp.dot(_silu(c_ref[...]), w_ref[...], preferred_element_type=F32,
                         precision=HIGHEST) + b_ref[...]


def _modulation(cc, w_mod, b_mod):
    rows, d = cc.shape
    n = w_mod.shape[1]
    tn = 1024
    return pl.pallas_call(
        _mod_kernel,
        grid=(n // tn,),
        in_specs=[pl.BlockSpec((rows, d), lambda j: (0, 0)),
                  pl.BlockSpec((d, tn), lambda j: (0, j)),
                  pl.BlockSpec((1, tn), lambda j: (0, j))],
        out_specs=pl.BlockSpec((rows, tn), lambda j: (0, j)),
        out_shape=jax.ShapeDtypeStruct((rows, n), F32),
        compiler_params=_cparams("parallel"),
        name="modulation",
    )(cc, w_mod, b_mod.reshape(1, n))


def _chunk_cumsum(g, reverse):
    n = g.shape[0]
    pos = lax.broadcasted_iota(jnp.int32, g.shape, 0) % CHUNK
    s = 1
    while s < CHUNK:
        if reverse:
            shifted = pltpu.roll(g, n - s, 0)
            ok = pos < CHUNK - s
        else:
            shifted = pltpu.roll(g, s, 0)
            ok = pos >= s
        g = g + jnp.where(ok, shifted, 0.0)
        s *= 2
    return g


def _inproj_kernel(latent, x_ref, sh_ref, sc_ref, g_ref, wqkv_ref, wba_ref, alog_ref, dtb_ref, *rest):
    if latent:
        wz_ref, wcf_ref, qkv_o, bgc_o, bgct_o, sz_o, y_o = rest
    else:
        qkv_o, bgc_o, bgct_o = rest
    x = x_ref[...]
    xn = x * lax.rsqrt(jnp.mean(x * x, axis=-1, keepdims=True) + EPS) * g_ref[...]
    hb = (xn * (1.0 + sc_ref[0]) + sh_ref[0]).astype(BF16)
    qkv_o[...] = jnp.dot(hb, wqkv_ref[...], preferred_element_type=F32)

    ba = jnp.dot(hb, wba_ref[...], preferred_element_type=F32)
    col = lax.broadcasted_iota(jnp.int32, ba.shape, 1)
    is_beta = (col % 8) < DN_HEADS
    g = -jnp.exp(alog_ref[...]) * jax.nn.softplus(ba + dtb_ref[...])
    g = jnp.where(is_beta, 0.0, g)
    gc = jnp.where(col < 8, _chunk_cumsum(g, False), _chunk_cumsum(g, True))
    bgc = jnp.where(is_beta, jax.nn.sigmoid(ba), gc)
    bgc_o[...] = bgc
    for c in range(bgc.shape[0] // CHUNK):
        bgct_o[c] = jnp.transpose(bgc[c * CHUNK:(c + 1) * CHUNK, :])[:16, :]

    if latent:
        sz_o[...] = _silu(jnp.dot(hb, wz_ref[...], preferred_element_type=F32))
        cf = jnp.dot(hb, wcf_ref[...], preferred_element_type=F32)
        y_o[...] = cf[:, :CF_CH] * jax.nn.sigmoid(cf[:, CF_CH:])


def _in_projection(x2, sh, sc, g_mix, wqkv, wba, alog_row, dtb_row, wz, wcf, rows_per_mod, tm):
    t, d = x2.shape
    latent = wz is not None
    tiles_per_mod = rows_per_mod // tm
    const = lambda i: (0, 0)
    mod_map = lambda i: (i // tiles_per_mod, 0, 0)
    row_map = lambda i: (i, 0)
    in_specs = [pl.BlockSpec((tm, d), row_map),
                pl.BlockSpec((1, 1, d), mod_map), pl.BlockSpec((1, 1, d), mod_map),
                pl.BlockSpec((1, d), const),
                pl.BlockSpec(wqkv.shape, const), pl.BlockSpec(wba.shape, const),
                pl.BlockSpec((1, LANES), const), pl.BlockSpec((1, LANES), const)]
    args = [x2, sh, sc, g_mix, wqkv, wba, alog_row, dtb_row]
    out_specs = [pl.BlockSpec((tm, DN_CONV_DIM), row_map), pl.BlockSpec((tm, LANES), row_map),
                 pl.BlockSpec((tm // CHUNK, 16, CHUNK), lambda i: (i, 0, 0))]
    out_shape = [jax.ShapeDtypeStruct((t, DN_CONV_DIM), F32), jax.ShapeDtypeStruct((t, LANES), F32),
                 jax.ShapeDtypeStruct((t // CHUNK, 16, CHUNK), F32)]
    if latent:
        in_specs += [pl.BlockSpec(wz.shape, const), pl.BlockSpec(wcf.shape, const)]
        args += [wz, wcf]
        out_specs += [pl.BlockSpec((tm, DN_WIDTH), row_map), pl.BlockSpec((tm, CF_CH), row_map)]
        out_shape += [jax.ShapeDtypeStruct((t, DN_WIDTH), F32), jax.ShapeDtypeStruct((t, CF_CH), F32)]
    return pl.pallas_call(
        functools.partial(_inproj_kernel, latent),
        grid=(t // tm,),
        in_specs=in_specs, out_specs=out_specs, out_shape=out_shape,
        compiler_params=_cparams("parallel"),
        name="in_projection_latent" if latent else "in_projection_context",
    )(*args)


CONV_ROWS = 64


def _shortconv_kernel(tiles_per_seq, prev_ref, cur_ref, next_ref, w_ref, o_ref, ext_ref):
    i = pl.program_id(0)
    tm = cur_ref.shape[0]
    pos = i % tiles_per_seq
    ext_ref[0:HALO, :] = jnp.where(pos == 0, 0.0, prev_ref[...])
    ext_ref[HALO:HALO + tm, :] = cur_ref[...]
    ext_ref[HALO + tm:, :] = jnp.where(pos == tiles_per_seq - 1, 0.0, next_ref[...])
    reach = SHORT_CONV // 2

    def body(r, carry):
        r0 = pl.multiple_of(r * CONV_ROWS, CONV_ROWS)
        for cb in range(DN_CONV_DIM // LANES):
            cols = slice(cb * LANES, (cb + 1) * LANES)
            blk = ext_ref[pl.ds(r0, CONV_ROWS + 2 * HALO), cols]
            acc = jnp.zeros((CONV_ROWS, LANES), F32)
            for k in range(SHORT_CONV):
                s = HALO - reach + k
                acc = acc + w_ref[k:k + 1, cols] * blk[s:s + CONV_ROWS]
            a = _silu(acc)
            if cb < 2 * DN_HEADS:
                a = a * lax.rsqrt(jnp.sum(a * a, axis=-1, keepdims=True) + EPS)
            if cb < DN_HEADS:
                a = a * (DN_D ** -0.5)
            o_ref[pl.ds(r0, CONV_ROWS), cols] = a
        return carry

    lax.fori_loop(0, tm // CONV_ROWS, body, 0)


def _short_conv(qkv, conv_w, seq_len, tm):
    t, c = qkv.shape
    tiles_per_seq = seq_len // tm
    hb = tm // HALO
    n_halo_blocks = t // HALO
    return pl.pallas_call(
        functools.partial(_shortconv_kernel, tiles_per_seq),
        grid=(t // tm,),
        in_specs=[pl.BlockSpec((HALO, c), lambda i: (jnp.maximum(i * hb - 1, 0), 0)),
                  pl.BlockSpec((tm, c), lambda i: (i, 0)),
                  pl.BlockSpec((HALO, c), lambda i: (jnp.minimum((i + 1) * hb, n_halo_blocks - 1), 0)),
                  pl.BlockSpec(conv_w.shape, lambda i: (0, 0))],
        out_specs=pl.BlockSpec((tm, c), lambda i: (i, 0)),
        out_shape=jax.ShapeDtypeStruct((t, c), F32),
        scratch_shapes=[pltpu.VMEM((tm + 2 * HALO, c), F32)],
        compiler_params=_cparams("parallel"),
        name="short_conv",
    )(qkv, qkv, qkv, conv_w)


def _delta_chunk(qkv_ref, bgc_ref, bgct_ref, ci, h, d, s_ref, masks, eye, want_out):
    r0 = pl.multiple_of(ci * CHUNK, CHUNK)
    rows = pl.ds(r0, CHUNK)
    q = qkv_ref[rows, h * DN_D:(h + 1) * DN_D]
    k = qkv_ref[rows, DN_QK + h * DN_D:DN_QK + (h + 1) * DN_D]
    v = qkv_ref[rows, 2 * DN_QK + h * DN_D:2 * DN_QK + (h + 1) * DN_D]
    bg = bgc_ref[rows, :]
    bgt = bgct_ref[ci]
    jb = d * 8 + h
    jg = d * 8 + DN_HEADS + h
    beta = bg[:, jb:jb + 1]
    gcol = bg[:, jg:jg + 1]
    grow = bgt[jg:jg + 1, :]
    incl, strict = masks[d]
    decay = jnp.exp(jnp.where(incl, gcol - grow, -jnp.inf))
    eg = jnp.exp(gcol)
    gl = gcol[CHUNK - 1:CHUNK] if d == 0 else gcol[0:1]

    kq = jnp.concatenate([k, q], axis=0).astype(BF16)
    gram = lax.dot_general(kq, k.astype(BF16), (((1,), (1,)), ((), ())), preferred_element_type=F32)
    lmat = jnp.where(strict, gram[:CHUNK] * decay, 0.0) * beta
    tinv = eye - lmat
    p = lmat
    n_sq = 1
    while n_sq < CHUNK // 2:
        p = _bdot(p, p)
        tinv = tinv + _bdot(tinv, p)
        n_sq *= 2
    rhs = jnp.concatenate([v * beta, k * (beta * eg)], axis=1)
    uw = _bdot(tinv, rhs)
    u, w = uw[:, :DN_D], uw[:, DN_D:]

    s_idx = d * DN_HEADS + h
    s = s_ref[s_idx]
    sb = s.astype(BF16)
    if want_out:
        wq = _bdot(jnp.concatenate([w, q * eg], axis=0), sb)
        ws, qs = wq[:CHUNK], wq[CHUNK:]
    else:
        ws = _bdot(w, sb)
    v_new = u - ws
    vb = v_new.astype(BF16)
    kd = (k * jnp.exp(gl - gcol)).astype(BF16)
    s_ref[s_idx] = s * jnp.exp(gl) + lax.dot_general(
        kd, vb, (((0,), (0,)), ((), ())), preferred_element_type=F32)
    if want_out:
        return qs + jnp.dot((gram[CHUNK:] * decay).astype(BF16), vb, preferred_element_type=F32)
    return None


def _delta_kernel(qc_ref, bc_ref, btc_ref, qx_ref, bx_ref, btx_ref, of_ref, ob_ref, s_ref):
    s_ref[...] = jnp.zeros(s_ref.shape, F32)
    row = lax.broadcasted_iota(jnp.int32, (CHUNK, CHUNK), 0)
    col = lax.broadcasted_iota(jnp.int32, (CHUNK, CHUNK), 1)
    masks = ((row >= col, row > col), (row <= col, row < col))
    eye = (row == col).astype(F32)
    nc_ctx = qc_ref.shape[0] // CHUNK
    nc = qx_ref.shape[0] // CHUNK

    def ctx_body(i, carry):
        for h in range(DN_HEADS):
            _delta_chunk(qc_ref, bc_ref, btc_ref, i, h, 0, s_ref, masks, eye, False)
            _delta_chunk(qc_ref, bc_ref, btc_ref, nc_ctx - 1 - i, h, 1, s_ref, masks, eye, False)
        return carry

    def lat_body(i, carry):
        ib = nc - 1 - i
        for h in range(DN_HEADS):
            cols = slice(h * DN_D, (h + 1) * DN_D)
            of_ref[pl.ds(pl.multiple_of(i * CHUNK, CHUNK), CHUNK), cols] = _delta_chunk(
                qx_ref, bx_ref, btx_ref, i, h, 0, s_ref, masks, eye, True)
            ob_ref[pl.ds(pl.multiple_of(ib * CHUNK, CHUNK), CHUNK), cols] = _delta_chunk(
                qx_ref, bx_ref, btx_ref, ib, h, 1, s_ref, masks, eye, True)
        return carry

    lax.fori_loop(0, nc_ctx, ctx_body, 0)
    lax.fori_loop(0, nc, lat_body, 0)


def _delta_scan(qkv_c, bgc_c, bgct_c, qkv_x, bgc_x, bgct_x, bsz):
    lc = qkv_c.shape[0] // bsz
    lx = qkv_x.shape[0] // bsz
    row_map = lambda b: (b, 0)
    out = jax.ShapeDtypeStruct((bsz * lx, DN_WIDTH), F32)
    return pl.pallas_call(
        _delta_kernel,
        grid=(bsz,),
        in_specs=[pl.BlockSpec((lc, DN_CONV_DIM), row_map), pl.BlockSpec((lc, LANES), row_map),
                  pl.BlockSpec((lc // CHUNK, 16, CHUNK), lambda b: (b, 0, 0)),
                  pl.BlockSpec((lx, DN_CONV_DIM), row_map), pl.BlockSpec((lx, LANES), row_map),
                  pl.BlockSpec((lx // CHUNK, 16, CHUNK), lambda b: (b, 0, 0))],
        out_specs=[pl.BlockSpec((lx, DN_WIDTH), row_map), pl.BlockSpec((lx, DN_WIDTH), row_map)],
        out_shape=[out, out],
        scratch_shapes=[pltpu.VMEM((2 * DN_HEADS, DN_D, DN_D), F32)],
        compiler_params=_cparams("parallel"),
        name="delta_scan",
    )(qkv_c, bgc_c, bgct_c, qkv_x, bgc_x, bgct_x)


def _pad_rows(a, rows):
    return jnp.pad(a, ((0, rows - a.shape[0]), (0, 0)))


def _mixer_front(x, c, ctx, c_ctx, w_mod, b_mod, g_mix, w_in, conv_w, a_log, dt_bias):
    bsz, seq, d = x.shape
    lc = ctx.shape[1]
    off_z = DN_CONV_DIM
    off_ba = off_z + DN_WIDTH
    off_cf = off_ba + 4 * DN_HEADS

    cc = _pad_rows(jnp.concatenate([c, c_ctx[None, :]], axis=0), -(-(bsz + 1) // 8) * 8)
    mod = _modulation(cc, w_mod, b_mod)
    mods = [mod[:, j * d:(j + 1) * d] for j in range(6)]
    lat = [m[:bsz].reshape(bsz, 1, d) for m in mods]
    con = [m[bsz:bsz + 1].reshape(1, 1, d) for m in mods]

    wqkv = w_in[:, :off_z].astype(BF16)
    wz = w_in[:, off_z:off_ba].astype(BF16)
    wba = jnp.pad(w_in[:, off_ba:off_cf], ((0, 0), (0, LANES - 4 * DN_HEADS))).astype(BF16)
    wcf = w_in[:, off_cf:].astype(BF16)
    zeros4 = jnp.zeros((2, DN_HEADS), F32)
    alog_row = jnp.pad(jnp.concatenate([zeros4, a_log], axis=1).reshape(1, -1), ((0, 0), (0, LANES - 16)))
    dtb_row = jnp.pad(jnp.concatenate([zeros4, dt_bias], axis=1).reshape(1, -1), ((0, 0), (0, LANES - 16)))
    g_row = g_mix.reshape(1, d)

    qkv_c, bgc_c, bgct_c = _in_projection(ctx.reshape(bsz * lc, d), con[0], con[1], g_row, wqkv, wba,
                                          alog_row, dtb_row, None, None, bsz * lc, min(lc, 512))
    qkv_x, bgc_x, bgct_x, sz, y = _in_projection(x.reshape(bsz * seq, d), lat[0], lat[1], g_row, wqkv, wba,
                                                 alog_row, dtb_row, wz, wcf, seq, min(seq, 512))
    qkvn_c = _short_conv(qkv_c, conv_w, lc, min(lc, 512))
    qkvn_x = _short_conv(qkv_x, conv_w, seq, min(seq, 512))
    o_f, o_b = _delta_scan(qkvn_c, bgc_c, bgct_c, qkvn_x, bgc_x, bgct_x, bsz)
    return dict(mod=mod, lat=lat, qkv_x=qkv_x, qkv_c=qkv_c, sz=sz, y=y, qkvn_x=qkvn_x, bgc_x=bgc_x,
                bgct_x=bgct_x, o_f=o_f, o_b=o_b)


CF_ROWS = 32


def _cfconv_kernel(y_ref, w_ref, b_ref, lng_ref, lnb_ref, o_ref, pad_ref):
    n = y_ref.shape[0]
    reach = (CF_K // 2) * GRID_W
    pad_ref[0:reach, :] = jnp.zeros((reach, CF_CH), F32)
    pad_ref[reach:reach + n, :] = y_ref[...]
    pad_ref[reach + n:, :] = jnp.zeros((reach, CF_CH), F32)

    def body(r, carry):
        r0 = pl.multiple_of(r * CF_ROWS, CF_ROWS)
        acc = jnp.zeros((CF_ROWS, CF_CH), F32)
        for k in range(CF_K):
            acc = acc + w_ref[k:k + 1, :] * pad_ref[pl.ds(r0 + k * GRID_W, CF_ROWS), :]
        acc = acc + b_ref[...]
        mu = jnp.mean(acc, axis=-1, keepdims=True)
        xc = acc - mu
        var = jnp.mean(xc * xc, axis=-1, keepdims=True)
        o_ref[pl.ds(r0, CF_ROWS), :] = _silu(xc * lax.rsqrt(var + EPS) * lng_ref[...] + lnb_ref[...])
        return carry

    lax.fori_loop(0, n // CF_ROWS, body, 0)


def _cf_conv(y, dw_w, dw_b, ln_g, ln_b, bsz):
    t, ch = y.shape
    n = t // bsz
    reach = (CF_K // 2) * GRID_W
    const = lambda b: (0, 0)
    return pl.pallas_call(
        _cfconv_kernel,
        grid=(bsz,),
        in_specs=[pl.BlockSpec((n, ch), lambda b: (b, 0)), pl.BlockSpec(dw_w.shape, const),
                  pl.BlockSpec((1, ch), const), pl.BlockSpec((1, ch), const), pl.BlockSpec((1, ch), const)],
        out_specs=pl.BlockSpec((n, ch), lambda b: (b, 0)),
        out_shape=jax.ShapeDtypeStruct((t, ch), F32),
        scratch_shapes=[pltpu.VMEM((n + 2 * reach, ch), F32)],
        compiler_params=_cparams("parallel"),
        name="conformer_conv",
    )(y, dw_w, dw_b.reshape(1, ch), ln_g.reshape(1, ch), ln_b.reshape(1, ch))


def _mixout_kernel(of_ref, ob_ref, sz_ref, cf_ref, x_ref, gt1_ref, sh2_ref, sc2_ref, gt2_ref, ng_ref, wo_ref,
                   gffn_ref, rwt_ref, wsgu_ref, wsd_ref, base_o, hn_o, lg_o):
    o = of_ref[...] + ob_ref[...]
    parts = []
    for h in range(DN_HEADS):
        oh = o[:, h * DN_D:(h + 1) * DN_D]
        parts.append(oh * lax.rsqrt(jnp.mean(oh * oh, axis=-1, keepdims=True) + EPS) * ng_ref[...])
    dn = jnp.concatenate(parts, axis=1) * sz_ref[...]
    heads = jnp.concatenate([dn, cf_ref[...]], axis=1).astype(BF16)
    x1 = x_ref[...] + gt1_ref[0] * jnp.dot(heads, wo_ref[...], preferred_element_type=F32)
    hn = (x1 * lax.rsqrt(jnp.mean(x1 * x1, axis=-1, keepdims=True) + EPS) * gffn_ref[...]
          * (1.0 + sc2_ref[0]) + sh2_ref[0])
    hn_o[...] = hn
    lg_o[...] = lax.dot_general(rwt_ref[...], hn, (((1,), (1,)), ((), ())), precision=HIGHEST,
                                preferred_element_type=F32)
    gu = jnp.dot(hn.astype(BF16), wsgu_ref[...], preferred_element_type=F32)
    ds = gu.shape[1] // 2
    act = (_silu(gu[:, :ds]) * gu[:, ds:]).astype(BF16)
    base_o[...] = x1 + gt2_ref[0] * jnp.dot(act, wsd_ref[...], preferred_element_type=F32)


def _mixer_out(o_f, o_b, sz, cfo, x2, gt1, sh2, sc2, gt2, norm_g, w_out, g_ffn, rwt, wsgu, wsd, seq, tm):
    t, d = x2.shape
    tiles = seq // tm
    const = lambda i: (0, 0)
    row_map = lambda i: (i, 0)
    mod_map = lambda i: (i // tiles, 0, 0)
    half = pl.BlockSpec((tm, DN_WIDTH), row_map)
    mod_spec = pl.BlockSpec((1, 1, d), mod_map)
    return pl.pallas_call(
        _mixout_kernel,
        grid=(t // tm,),
        in_specs=[half, half, half, half, pl.BlockSpec((tm, d), row_map),
                  mod_spec, mod_spec, mod_spec, mod_spec,
                  pl.BlockSpec((1, DN_D), const), pl.BlockSpec(w_out.shape, const), pl.BlockSpec((1, d), const),
                  pl.BlockSpec(rwt.shape, const), pl.BlockSpec(wsgu.shape, const), pl.BlockSpec(wsd.shape, const)],
        out_specs=[pl.BlockSpec((tm, d), row_map), pl.BlockSpec((tm, d), row_map),
                   pl.BlockSpec((N_EXPERTS, tm), lambda i: (0, i))],
        out_shape=[jax.ShapeDtypeStruct((t, d), F32), jax.ShapeDtypeStruct((t, d), F32),
                   jax.ShapeDtypeStruct((N_EXPERTS, t), F32)],
        compiler_params=_cparams("parallel"),
        name="mixer_out",
    )(o_f, o_b, sz, cfo, x2, gt1, sh2, sc2, gt2, norm_g, w_out, g_ffn, rwt, wsgu, wsd)


def _first_argmax(vals, idx, sentinel):
    m = jnp.max(vals, axis=0, keepdims=True)
    return m, jnp.min(jnp.where(vals == m, idx, sentinel), axis=0, keepdims=True)


def _router_kernel(lg_ref, bias_ref, e_o, r_o, wt_o, cnt_o, carry_ref):
    @pl.when(pl.program_id(0) == 0)
    def _():
        carry_ref[...] = jnp.zeros(carry_ref.shape, F32)

    tt = lg_ref.shape[1]
    neg = -jnp.inf
    scores = jax.nn.sigmoid(lg_ref[...])
    sel = scores + bias_ref[...]
    sub = lax.broadcasted_iota(jnp.int32, (GROUP_SIZE, tt), 0)

    rows = []
    for g in range(N_GROUPS):
        sg = sel[g * GROUP_SIZE:(g + 1) * GROUP_SIZE]
        m1, first = _first_argmax(sg, sub, GROUP_SIZE)
        m2 = jnp.max(jnp.where(sub == first, neg, sg), axis=0, keepdims=True)
        rows.append(m1 + m2)
    cur = jnp.concatenate(rows, axis=0)
    gidx = lax.broadcasted_iota(jnp.int32, (N_GROUPS, tt), 0)
    keep = gidx < 0
    for _ in range(TOPK_GROUPS):
        _, a = _first_argmax(cur, gidx, N_GROUPS)
        pick = gidx == a
        keep = jnp.logical_or(keep, pick)
        cur = jnp.where(pick, neg, cur)
    keep_e = jnp.concatenate([jnp.broadcast_to(keep[g:g + 1], (GROUP_SIZE, tt)) for g in range(N_GROUPS)],
                             axis=0)
    masked = jnp.where(keep_e, sel, neg)

    eidx = lax.broadcasted_iota(jnp.int32, (N_EXPERTS, tt), 0)
    e_rows, w_rows, picks = [], [], []
    for _ in range(TOP_K):
        _, a = _first_argmax(masked, eidx, N_EXPERTS)
        pick = eidx == a
        e_rows.append(a)
        w_rows.append(jnp.sum(jnp.where(pick, scores, 0.0), axis=0, keepdims=True))
        picks.append(pick)
        masked = jnp.where(pick, neg, masked)
    onehot = sum(p.astype(F32) for p in picks)
    scale = ROUTED_SCALE / sum(w_rows)

    tri = (lax.broadcasted_iota(jnp.int32, (tt, tt), 0) < lax.broadcasted_iota(jnp.int32, (tt, tt), 1))
    cum = jnp.dot(onehot.astype(BF16), tri.astype(BF16), preferred_element_type=F32) + carry_ref[:, 0:1]
    r_rows = [jnp.sum(jnp.where(p, cum, 0.0), axis=0, keepdims=True).astype(jnp.int32) for p in picks]
    carry_ref[...] = carry_ref[...] + jnp.sum(onehot, axis=1, keepdims=True)
    cnt_o[...] = carry_ref[...]

    fill = 8 - TOP_K
    e_o[...] = jnp.concatenate(e_rows + [jnp.zeros((fill, tt), jnp.int32)], axis=0)
    r_o[...] = jnp.concatenate(r_rows + [jnp.zeros((fill, tt), jnp.int32)], axis=0)
    w_pad = jnp.concatenate([w * scale for w in w_rows] + [jnp.zeros((LANES - TOP_K, tt), F32)], axis=0)
    wt_o[...] = jnp.transpose(w_pad)


def _router(lg_t, router_bias, tt):
    e, t = lg_t.shape
    idx_spec = pl.BlockSpec((8, tt), lambda i: (0, i))
    return pl.pallas_call(
        _router_kernel,
        grid=(t // tt,),
        in_specs=[pl.BlockSpec((e, tt), lambda i: (0, i)), pl.BlockSpec((e, 1), lambda i: (0, 0))],
        out_specs=[idx_spec, idx_spec, pl.BlockSpec((tt, LANES), lambda i: (i, 0)),
                   pl.BlockSpec((e, LANES), lambda i: (0, 0))],
        out_shape=[jax.ShapeDtypeStruct((8, t), jnp.int32), jax.ShapeDtypeStruct((8, t), jnp.int32),
                   jax.ShapeDtypeStruct((t, LANES), F32), jax.ShapeDtypeStruct((e, LANES), F32)],
        scratch_shapes=[pltpu.VMEM((e, LANES), F32)],
        compiler_params=_cparams("arbitrary"),
        name="router",
    )(lg_t, router_bias.reshape(e, 1))


def _dispatch_kernel(pstart_ref, e_ref, r_ref, hn_hbm, xs_in, xs_hbm, sem):
    del xs_in
    td = e_ref.shape[1]
    base = pl.program_id(0) * td

    def body(t, carry):
        for j in range(TOP_K):
            dest = pstart_ref[e_ref[j, t]] + r_ref[j, t]
            pltpu.make_async_copy(hn_hbm.at[pl.ds(base + t, 1)], xs_hbm.at[pl.ds(dest, 1)], sem).start()
        return carry

    lax.fori_loop(0, td, body, 0)
    pltpu.make_async_copy(xs_hbm.at[pl.ds(0, td * TOP_K)], xs_hbm.at[pl.ds(0, td * TOP_K)], sem).wait()


def _dispatch(pstart, e_idx, rank, hn, n_slots, td):
    t, d = hn.shape
    smem_spec = pl.BlockSpec((8, td), lambda i, ps: (0, i), memory_space=pltpu.SMEM)
    any_spec = pl.BlockSpec(memory_space=pl.ANY)
    return pl.pallas_call(
        _dispatch_kernel,
        grid_spec=pltpu.PrefetchScalarGridSpec(
            num_scalar_prefetch=1, grid=(t // td,),
            in_specs=[smem_spec, smem_spec, any_spec, any_spec],
            out_specs=any_spec,
            scratch_shapes=[pltpu.SemaphoreType.DMA(())]),
        out_shape=jax.ShapeDtypeStruct((n_slots, d), F32),
        input_output_aliases={4: 0},
        compiler_params=_cparams("arbitrary"),
        name="dispatch",
    )(pstart, e_idx, rank, hn, jnp.zeros((n_slots, d), F32))


def _expert_kernel(bexp_ref, bsrc_ref, nused_ref, xs_ref, wgu_ref, wd_ref, y_ref):
    del bexp_ref, bsrc_ref
    i = pl.program_id(0)

    @pl.when(i < nused_ref[0])
    def _():
        gu = jnp.dot(xs_ref[...].astype(BF16), wgu_ref[0], preferred_element_type=F32)
        de = gu.shape[1] // 2
        act = (_silu(gu[:, :de]) * gu[:, de:]).astype(BF16)
        y_ref[...] = jnp.dot(act, wd_ref[0], preferred_element_type=F32)

    @pl.when(i >= nused_ref[0])
    def _():
        y_ref[...] = jnp.zeros(y_ref.shape, F32)


def _experts(block_exp, block_src, n_used, xs, wgu, wd, bm):
    n_slots, d = xs.shape
    return pl.pallas_call(
        _expert_kernel,
        grid_spec=pltpu.PrefetchScalarGridSpec(
            num_scalar_prefetch=3, grid=(n_slots // bm,),
            in_specs=[pl.BlockSpec((bm, d), lambda i, be, bs, nu: (bs[i], 0)),
                      pl.BlockSpec((1,) + wgu.shape[1:], lambda i, be, bs, nu: (be[i], 0, 0)),
                      pl.BlockSpec((1,) + wd.shape[1:], lambda i, be, bs, nu: (be[i], 0, 0))],
            out_specs=pl.BlockSpec((bm, d), lambda i, be, bs, nu: (i, 0))),
        out_shape=jax.ShapeDtypeStruct((n_slots, d), F32),
        compiler_params=_cparams("arbitrary"),
        name="experts",
    )(block_exp, block_src, n_used, xs, wgu, wd)


def _combine_kernel(pstart_ref, e_ref, r_ref, wt_ref, base_ref, gt2_ref, gfin_ref, y_hbm, o_ref, buf, sem):
    tg = base_ref.shape[0]

    def body(t, carry):
        for j in range(TOP_K):
            src = pstart_ref[e_ref[j, t]] + r_ref[j, t]
            pltpu.make_async_copy(y_hbm.at[pl.ds(src, 1)], buf.at[pl.ds(j * tg + t, 1)], sem).start()
        return carry

    lax.fori_loop(0, tg, body, 0)
    pltpu.make_async_copy(y_hbm.at[pl.ds(0, tg * TOP_K)], buf, sem).wait()
    wt = wt_ref[...]
    routed = wt[:, 0:1] * buf[0:tg, :]
    for j in range(1, TOP_K):
        routed = routed + wt[:, j:j + 1] * buf[j * tg:(j + 1) * tg, :]
    xf = base_ref[...] + gt2_ref[0] * routed
    o_ref[...] = xf * lax.rsqrt(jnp.mean(xf * xf, axis=-1, keepdims=True) + EPS) * gfin_ref[...]


def _combine(pstart, e_idx, rank, wt, base, gt2, g_final, y, seq, tg):
    t, d = base.shape
    tiles = seq // tg
    smem_spec = pl.BlockSpec((8, tg), lambda i, ps: (0, i), memory_space=pltpu.SMEM)
    return pl.pallas_call(
        _combine_kernel,
        grid_spec=pltpu.PrefetchScalarGridSpec(
            num_scalar_prefetch=1, grid=(t // tg,),
            in_specs=[smem_spec, smem_spec,
                      pl.BlockSpec((tg, LANES), lambda i, ps: (i, 0)),
                      pl.BlockSpec((tg, d), lambda i, ps: (i, 0)),
                      pl.BlockSpec((1, 1, d), lambda i, ps: (i // tiles, 0, 0)),
                      pl.BlockSpec((1, d), lambda i, ps: (0, 0)),
                      pl.BlockSpec(memory_space=pl.ANY)],
            out_specs=pl.BlockSpec((tg, d), lambda i, ps: (i, 0)),
            scratch_shapes=[pltpu.VMEM((TOP_K * tg, d), F32), pltpu.SemaphoreType.DMA(())]),
        out_shape=jax.ShapeDtypeStruct((t, d), F32),
        compiler_params=_cparams("arbitrary"),
        name="combine",
    )(pstart, e_idx, rank, wt, base, gt2, g_final.reshape(1, d), y)


EXPERT_BLOCK = 256
ROW_TILE = 512
ROUTER_TILE = 512
GATHER_TILE = 256


def _moe_plan(counts, n_tokens, bm):
    padded = (counts + bm - 1) // bm * bm
    pad_end = jnp.cumsum(padded)
    pad_start = (pad_end - padded).astype(jnp.int32)
    n_blocks = -(-(n_tokens * TOP_K) // bm) + N_EXPERTS
    n_used = (pad_end[-1] // bm).astype(jnp.int32)
    block_src = jnp.minimum(jnp.arange(n_blocks, dtype=jnp.int32), jnp.maximum(n_used - 1, 0))
    block_exp = jnp.minimum(jnp.searchsorted(pad_end, block_src * bm, side='right'), N_EXPERTS - 1)
    return pad_start, block_exp.astype(jnp.int32), block_src, n_used.reshape(1), n_blocks * bm


def kernel(x, c, ctx, c_ctx, w_mod, b_mod, g_mix, g_ffn, w_in, w_out, dn_conv_w, dn_a_log, dn_dt_bias,
           dn_norm_g, cf_dw_w, cf_dw_b, cf_ln_g, cf_ln_b, router_w, router_bias, exp_w_gate, exp_w_up,
           exp_w_down, sh_w_gate, sh_w_up, sh_w_down, g_final):
    assert w_mod.shape[0] == 1, "single-layer block: the context stream is never re-read"
    bsz, seq, d = x.shape
    t = bsz * seq
    x2 = x.reshape(t, d)
    st = _mixer_front(x, c, ctx, c_ctx, w_mod[0], b_mod[0], g_mix[0], w_in[0], dn_conv_w[0], dn_a_log[0],
                      dn_dt_bias[0])
    _, _, gt1, sh2, sc2, gt2 = st['lat']
    cfo = _cf_conv(st['y'], cf_dw_w[0], cf_dw_b[0], cf_ln_g[0], cf_ln_b[0], bsz)
    wsgu = jnp.concatenate([sh_w_gate[0], sh_w_up[0]], axis=1).astype(BF16)
    base, hn, lg_t = _mixer_out(st['o_f'], st['o_b'], st['sz'], cfo, x2, gt1, sh2, sc2, gt2,
                                dn_norm_g[0].reshape(1, DN_D), w_out[0].astype(BF16), g_ffn[0].reshape(1, d),
                                router_w[0].T, wsgu, sh_w_down[0].astype(BF16), seq, min(seq, ROW_TILE))
    e_idx, rank, wt, cnt = _router(lg_t, router_bias[0], min(t, ROUTER_TILE))
    pstart, block_exp, block_src, n_used, n_slots = _moe_plan(cnt[:, 0].astype(jnp.int32), t, EXPERT_BLOCK)
    xs = _dispatch(pstart, e_idx, rank, hn, n_slots, min(t, GATHER_TILE))
    wgu = jnp.concatenate([exp_w_gate[0], exp_w_up[0]], axis=2).astype(BF16)
    y = _experts(block_exp, block_src, n_used, xs, wgu, exp_w_down[0].astype(BF16), EXPERT_BLOCK)
    out = _combine(pstart, e_idx, rank, wt, base, gt2, g_final, y, seq, min(seq, GATHER_TILE))
    return out.reshape(bsz, seq, d)
```

```python
import functools

import jax
import jax.numpy as jnp
from jax import lax
from jax.experimental import pallas as pl
from jax.experimental.pallas import tpu as pltpu

F32 = jnp.float32
BF16 = jnp.bfloat16
HIGHEST = lax.Precision.HIGHEST

EPS = 1e-6
LANES = 128
GRID_W = 64
DN_HEADS = 4
DN_D = 128
DN_QK = DN_HEADS * DN_D
DN_WIDTH = DN_HEADS * DN_D
DN_CONV_DIM = 2 * DN_QK + DN_WIDTH
SHORT_CONV = 7
CHUNK = 64
CF_CH = 512
CF_K = 31
N_EXPERTS = 64
TOP_K = 6
N_GROUPS = 8
GROUP_SIZE = N_EXPERTS // N_GROUPS
TOPK_GROUPS = 4
ROUTED_SCALE = 2.5
HALO = 8
VMEM_LIMIT = 56 * 1024 * 1024


def _cparams(*sem):
    return pltpu.CompilerParams(dimension_semantics=sem, vmem_limit_bytes=VMEM_LIMIT)


def _silu(v):
    return v * jax.nn.sigmoid(v)


def _bdot(a, b):
    return jnp.dot(a.astype(BF16), b.astype(BF16), preferred_element_type=F32)


def _mod_kernel(c_ref, w_ref, b_ref, o_ref):
    o_ref[...] = jnp.dot(_silu(c_ref[...]), w_ref[...], preferred_element_type=F32,
                         precision=HIGHEST) + b_ref[...]


def _modulation(cc, w_mod, b_mod):
    rows, d = cc.shape
    n = w_mod.shape[1]
    tn = 1024
    return pl.pallas_call(
        _mod_kernel,
        grid=(n // tn,),
        in_specs=[pl.BlockSpec((rows, d), lambda j: (0, 0)),
                  pl.BlockSpec((d, tn), lambda j: (0, j)),
                  pl.BlockSpec((1, tn), lambda j: (0, j))],
        out_specs=pl.BlockSpec((rows, tn), lambda j: (0, j)),
        out_shape=jax.ShapeDtypeStruct((rows, n), F32),
        compiler_params=_cparams("parallel"),
        name="modulation",
    )(cc, w_mod, b_mod.reshape(1, n))


def _chunk_cumsum(g, reverse):
    n = g.shape[0]
    pos = lax.broadcasted_iota(jnp.int32, g.shape, 0) % CHUNK
    s = 1
    while s < CHUNK:
        if reverse:
            shifted = pltpu.roll(g, n - s, 0)
            ok = pos < CHUNK - s
        else:
            shifted = pltpu.roll(g, s, 0)
            ok = pos >= s
        g = g + jnp.where(ok, shifted, 0.0)
        s *= 2
    return g


def _inproj_kernel(latent, x_ref, sh_ref, sc_ref, g_ref, wqkv_ref, wba_ref, alog_ref, dtb_ref, *rest):
    if latent:
        wz_ref, wcf_ref, qkv_o, bgc_o, bgct_o, sz_o, y_o = rest
    else:
        qkv_o, bgc_o, bgct_o = rest
    x = x_ref[...]
    xn = x * lax.rsqrt(jnp.mean(x * x, axis=-1, keepdims=True) + EPS) * g_ref[...]
    hb = (xn * (1.0 + sc_ref[0]) + sh_ref[0]).astype(BF16)
    qkv_o[...] = jnp.dot(hb, wqkv_ref[...], preferred_element_type=F32)

    ba = jnp.dot(hb, wba_ref[...], preferred_element_type=F32)
    col = lax.broadcasted_iota(jnp.int32, ba.shape, 1)
    is_beta = (col % 8) < DN_HEADS
    g = -jnp.exp(alog_ref[...]) * jax.nn.softplus(ba + dtb_ref[...])
    g = jnp.where(is_beta, 0.0, g)
    gc = jnp.where(col < 8, _chunk_cumsum(g, False), _chunk_cumsum(g, True))
    bgc = jnp.where(is_beta, jax.nn.sigmoid(ba), gc)
    bgc_o[...] = bgc
    for c in range(bgc.shape[0] // CHUNK):
        bgct_o[c] = jnp.transpose(bgc[c * CHUNK:(c + 1) * CHUNK, :])[:16, :]

    if latent:
        sz_o[...] = _silu(jnp.dot(hb, wz_ref[...], preferred_element_type=F32))
        cf = jnp.dot(hb, wcf_ref[...], preferred_element_type=F32)
        y_o[...] = cf[:, :CF_CH] * jax.nn.sigmoid(cf[:, CF_CH:])


def _in_projection(x2, sh, sc, g_mix, wqkv, wba, alog_row, dtb_row, wz, wcf, rows_per_mod, tm):
    t, d = x2.shape
    latent = wz is not None
    tiles_per_mod = rows_per_mod // tm
    const = lambda i: (0, 0)
    mod_map = lambda i: (i // tiles_per_mod, 0, 0)
    row_map = lambda i: (i, 0)
    in_specs = [pl.BlockSpec((tm, d), row_map),
                pl.BlockSpec((1, 1, d), mod_map), pl.BlockSpec((1, 1, d), mod_map),
                pl.BlockSpec((1, d), const),
                pl.BlockSpec(wqkv.shape, const), pl.BlockSpec(wba.shape, const),
                pl.BlockSpec((1, LANES), const), pl.BlockSpec((1, LANES), const)]
    args = [x2, sh, sc, g_mix, wqkv, wba, alog_row, dtb_row]
    out_specs = [pl.BlockSpec((tm, DN_CONV_DIM), row_map), pl.BlockSpec((tm, LANES), row_map),
                 pl.BlockSpec((tm // CHUNK, 16, CHUNK), lambda i: (i, 0, 0))]
    out_shape = [jax.ShapeDtypeStruct((t, DN_CONV_DIM), F32), jax.ShapeDtypeStruct((t, LANES), F32),
                 jax.ShapeDtypeStruct((t // CHUNK, 16, CHUNK), F32)]
    if latent:
        in_specs += [pl.BlockSpec(wz.shape, const), pl.BlockSpec(wcf.shape, const)]
        args += [wz, wcf]
        out_specs += [pl.BlockSpec((tm, DN_WIDTH), row_map), pl.BlockSpec((tm, CF_CH), row_map)]
        out_shape += [jax.ShapeDtypeStruct((t, DN_WIDTH), F32), jax.ShapeDtypeStruct((t, CF_CH), F32)]
    return pl.pallas_call(
        functools.partial(_inproj_kernel, latent),
        grid=(t // tm,),
        in_specs=in_specs, out_specs=out_specs, out_shape=out_shape,
        compiler_params=_cparams("parallel"),
        name="in_projection_latent" if latent else "in_projection_context",
    )(*args)


CONV_ROWS = 64


def _shortconv_kernel(tiles_per_seq, prev_ref, cur_ref, next_ref, w_ref, o_ref, ext_ref):
    i = pl.program_id(0)
    tm = cur_ref.shape[0]
    pos = i % tiles_per_seq
    ext_ref[0:HALO, :] = jnp.where(pos == 0, 0.0, prev_ref[...])
    ext_ref[HALO:HALO + tm, :] = cur_ref[...]
    ext_ref[HALO + tm:, :] = jnp.where(pos == tiles_per_seq - 1, 0.0, next_ref[...])
    reach = SHORT_CONV // 2

    def body(r, carry):
        r0 = pl.multiple_of(r * CONV_ROWS, CONV_ROWS)
        for cb in range(DN_CONV_DIM // LANES):
            cols = slice(cb * LANES, (cb + 1) * LANES)
            blk = ext_ref[pl.ds(r0, CONV_ROWS + 2 * HALO), cols]
            acc = jnp.zeros((CONV_ROWS, LANES), F32)
            for k in range(SHORT_CONV):
                s = HALO - reach + k
                acc = acc + w_ref[k:k + 1, cols] * blk[s:s + CONV_ROWS]
            a = _silu(acc)
            if cb < 2 * DN_HEADS:
                a = a * lax.rsqrt(jnp.sum(a * a, axis=-1, keepdims=True) + EPS)
            if cb < DN_HEADS:
                a = a * (DN_D ** -0.5)
            o_ref[pl.ds(r0, CONV_ROWS), cols] = a
        return carry

    lax.fori_loop(0, tm // CONV_ROWS, body, 0)


def _short_conv(qkv, conv_w, seq_len, tm):
    t, c = qkv.shape
    tiles_per_seq = seq_len // tm
    hb = tm // HALO
    n_halo_blocks = t // HALO
    return pl.pallas_call(
        functools.partial(_shortconv_kernel, tiles_per_seq),
        grid=(t // tm,),
        in_specs=[pl.BlockSpec((HALO, c), lambda i: (jnp.maximum(i * hb - 1, 0), 0)),
                  pl.BlockSpec((tm, c), lambda i: (i, 0)),
                  pl.BlockSpec((HALO, c), lambda i: (jnp.minimum((i + 1) * hb, n_halo_blocks - 1), 0)),
                  pl.BlockSpec(conv_w.shape, lambda i: (0, 0))],
        out_specs=pl.BlockSpec((tm, c), lambda i: (i, 0)),
        out_shape=jax.ShapeDtypeStruct((t, c), F32),
        scratch_shapes=[pltpu.VMEM((tm + 2 * HALO, c), F32)],
        compiler_params=_cparams("parallel"),
        name="short_conv",
    )(qkv, qkv, qkv, conv_w)


def _delta_step(refs, ci_f, ci_b, s_ref, masks, eye, want_out):
    qkv_ref, bgc_ref, bgct_ref = refs
    chains = range(2 * DN_HEADS)
    q, k, v, beta, gcol, decay, eg, gl = [], [], [], [], [], [], [], []
    for d, ci in enumerate((ci_f, ci_b)):
        rows = pl.ds(pl.multiple_of(ci * CHUNK, CHUNK), CHUNK)
        bg = bgc_ref[rows, :]
        bgt = bgct_ref[ci]
        for h in range(DN_HEADS):
            q.append(qkv_ref[rows, h * DN_D:(h + 1) * DN_D])
            k.append(qkv_ref[rows, DN_QK + h * DN_D:DN_QK + (h + 1) * DN_D])
            v.append(qkv_ref[rows, 2 * DN_QK + h * DN_D:2 * DN_QK + (h + 1) * DN_D])
            jb = d * 8 + h
            jg = d * 8 + DN_HEADS + h
            beta.append(bg[:, jb:jb + 1])
            gc = bg[:, jg:jg + 1]
            gcol.append(gc)
            decay.append(jnp.exp(jnp.where(masks[d][0], gc - bgt[jg:jg + 1, :], -jnp.inf)))
            eg.append(jnp.exp(gc))
            gl.append(gc[CHUNK - 1:CHUNK] if d == 0 else gc[0:1])

    contract_last = (((1,), (1,)), ((), ()))
    contract_first = (((0,), (0,)), ((), ()))
    gram = [lax.dot_general(jnp.concatenate([k[n], q[n]], axis=0).astype(BF16), k[n].astype(BF16),
                            contract_last, preferred_element_type=F32) for n in chains]
    lmat = [jnp.where(masks[n // DN_HEADS][1], gram[n][:CHUNK] * decay[n], 0.0) * beta[n] for n in chains]
    tinv = [eye - lmat[n] for n in chains]
    p = [_bdot(lmat[n], lmat[n]) for n in chains]
    n_sq = 2
    while n_sq < CHUNK // 2:
        x = [_bdot(jnp.concatenate([tinv[n], p[n]], axis=0), p[n]) for n in chains]
        tinv = [tinv[n] + x[n][:CHUNK] for n in chains]
        p = [x[n][CHUNK:] for n in chains]
        n_sq *= 2
    tinv = [tinv[n] + _bdot(tinv[n], p[n]) for n in chains]
    uw = [_bdot(tinv[n], jnp.concatenate([v[n] * beta[n], k[n] * (beta[n] * eg[n])], axis=1)) for n in chains]

    s = [s_ref[n] for n in chains]
    if want_out:
        wq = [_bdot(jnp.concatenate([uw[n][:, DN_D:], q[n] * eg[n]], axis=0), s[n]) for n in chains]
        ws = [wq[n][:CHUNK] for n in chains]
    else:
        ws = [_bdot(uw[n][:, DN_D:], s[n]) for n in chains]
    vb = [(uw[n][:, :DN_D] - ws[n]).astype(BF16) for n in chains]
    for n in chains:
        kd = (k[n] * jnp.exp(gl[n] - gcol[n])).astype(BF16)
        s_ref[n] = s[n] * jnp.exp(gl[n]) + lax.dot_general(kd, vb[n], contract_first,
                                                         preferred_element_type=F32)
    if not want_out:
        return None
    return [wq[n][CHUNK:] + jnp.dot((gram[n][CHUNK:] * decay[n]).astype(BF16), vb[n],
                                    preferred_element_type=F32) for n in chains]


def _delta_kernel(qc_ref, bc_ref, btc_ref, qx_ref, bx_ref, btx_ref, of_ref, ob_ref, s_ref):
    s_ref[...] = jnp.zeros(s_ref.shape, F32)
    row = lax.broadcasted_iota(jnp.int32, (CHUNK, CHUNK), 0)
    col = lax.broadcasted_iota(jnp.int32, (CHUNK, CHUNK), 1)
    masks = ((row >= col, row > col), (row <= col, row < col))
    eye = (row == col).astype(F32)
    nc_ctx = qc_ref.shape[0] // CHUNK
    nc = qx_ref.shape[0] // CHUNK

    def ctx_body(i, carry):
        _delta_step((qc_ref, bc_ref, btc_ref), i, nc_ctx - 1 - i, s_ref, masks, eye, False)
        return carry

    def lat_body(i, carry):
        ib = nc - 1 - i
        o = _delta_step((qx_ref, bx_ref, btx_ref), i, ib, s_ref, masks, eye, True)
        for h in range(DN_HEADS):
            cols = slice(h * DN_D, (h + 1) * DN_D)
            of_ref[pl.ds(pl.multiple_of(i * CHUNK, CHUNK), CHUNK), cols] = o[h]
            ob_ref[pl.ds(pl.multiple_of(ib * CHUNK, CHUNK), CHUNK), cols] = o[DN_HEADS + h]
        return carry

    lax.fori_loop(0, nc_ctx, ctx_body, 0)
    lax.fori_loop(0, nc, lat_body, 0)


def _delta_scan(qkv_c, bgc_c, bgct_c, qkv_x, bgc_x, bgct_x, bsz):
    lc = qkv_c.shape[0] // bsz
    lx = qkv_x.shape[0] // bsz
    row_map = lambda b: (b, 0)
    out = jax.ShapeDtypeStruct((bsz * lx, DN_WIDTH), F32)
    return pl.pallas_call(
        _delta_kernel,
        grid=(bsz,),
        in_specs=[pl.BlockSpec((lc, DN_CONV_DIM), row_map), pl.BlockSpec((lc, LANES), row_map),
                  pl.BlockSpec((lc // CHUNK, 16, CHUNK), lambda b: (b, 0, 0)),
                  pl.BlockSpec((lx, DN_CONV_DIM), row_map), pl.BlockSpec((lx, LANES), row_map),
                  pl.BlockSpec((lx // CHUNK, 16, CHUNK), lambda b: (b, 0, 0))],
        out_specs=[pl.BlockSpec((lx, DN_WIDTH), row_map), pl.BlockSpec((lx, DN_WIDTH), row_map)],
        out_shape=[out, out],
        scratch_shapes=[pltpu.VMEM((2 * DN_HEADS, DN_D, DN_D), F32)],
        compiler_params=_cparams("parallel"),
        name="delta_scan",
    )(qkv_c, bgc_c, bgct_c, qkv_x, bgc_x, bgct_x)


def _pad_rows(a, rows):
    return jnp.pad(a, ((0, rows - a.shape[0]), (0, 0)))


def _mixer_front(x, c, ctx, c_ctx, w_mod, b_mod, g_mix, w_in, conv_w, a_log, dt_bias):
    bsz, seq, d = x.shape
    lc = ctx.shape[1]
    off_z = DN_CONV_DIM
    off_ba = off_z + DN_WIDTH
    off_cf = off_ba + 4 * DN_HEADS

    cc = _pad_rows(jnp.concatenate([c, c_ctx[None, :]], axis=0), -(-(bsz + 1) // 8) * 8)
    mod = _modulation(cc, w_mod, b_mod)
    mods = [mod[:, j * d:(j + 1) * d] for j in range(6)]
    lat = [m[:bsz].reshape(bsz, 1, d) for m in mods]
    con = [m[bsz:bsz + 1].reshape(1, 1, d) for m in mods]

    wqkv = w_in[:, :off_z].astype(BF16)
    wz = w_in[:, off_z:off_ba].astype(BF16)
    wba = jnp.pad(w_in[:, off_ba:off_cf], ((0, 0), (0, LANES - 4 * DN_HEADS))).astype(BF16)
    wcf = w_in[:, off_cf:].astype(BF16)
    zeros4 = jnp.zeros((2, DN_HEADS), F32)
    alog_row = jnp.pad(jnp.concatenate([zeros4, a_log], axis=1).reshape(1, -1), ((0, 0), (0, LANES - 16)))
    dtb_row = jnp.pad(jnp.concatenate([zeros4, dt_bias], axis=1).reshape(1, -1), ((0, 0), (0, LANES - 16)))
    g_row = g_mix.reshape(1, d)

    qkv_c, bgc_c, bgct_c = _in_projection(ctx.reshape(bsz * lc, d), con[0], con[1], g_row, wqkv, wba,
                                          alog_row, dtb_row, None, None, bsz * lc, min(lc, 512))
    qkv_x, bgc_x, bgct_x, sz, y = _in_projection(x.reshape(bsz * seq, d), lat[0], lat[1], g_row, wqkv, wba,
                                                 alog_row, dtb_row, wz, wcf, seq, min(seq, 512))
    qkvn_c = _short_conv(qkv_c, conv_w, lc, min(lc, 512))
    qkvn_x = _short_conv(qkv_x, conv_w, seq, min(seq, 512))
    o_f, o_b = _delta_scan(qkvn_c, bgc_c, bgct_c, qkvn_x, bgc_x, bgct_x, bsz)
    return dict(mod=mod, lat=lat, qkv_x=qkv_x, qkv_c=qkv_c, sz=sz, y=y, qkvn_x=qkvn_x, bgc_x=bgc_x,
                bgct_x=bgct_x, o_f=o_f, o_b=o_b)


CF_ROWS = 32


def _cfconv_kernel(y_ref, w_ref, b_ref, lng_ref, lnb_ref, o_ref, pad_ref):
    n = y_ref.shape[0]
    reach = (CF_K // 2) * GRID_W
    pad_ref[0:reach, :] = jnp.zeros((reach, CF_CH), F32)
    pad_ref[reach:reach + n, :] = y_ref[...]
    pad_ref[reach + n:, :] = jnp.zeros((reach, CF_CH), F32)

    def body(r, carry):
        r0 = pl.multiple_of(r * CF_ROWS, CF_ROWS)
        acc = jnp.zeros((CF_ROWS, CF_CH), F32)
        for k in range(CF_K):
            acc = acc + w_ref[k:k + 1, :] * pad_ref[pl.ds(r0 + k * GRID_W, CF_ROWS), :]
        acc = acc + b_ref[...]
        mu = jnp.mean(acc, axis=-1, keepdims=True)
        xc = acc - mu
        var = jnp.mean(xc * xc, axis=-1, keepdims=True)
        o_ref[pl.ds(r0, CF_ROWS), :] = _silu(xc * lax.rsqrt(var + EPS) * lng_ref[...] + lnb_ref[...])
        return carry

    lax.fori_loop(0, n // CF_ROWS, body, 0)


def _cf_conv(y, dw_w, dw_b, ln_g, ln_b, bsz):
    t, ch = y.shape
    n = t // bsz
    reach = (CF_K // 2) * GRID_W
    const = lambda b: (0, 0)
    return pl.pallas_call(
        _cfconv_kernel,
        grid=(bsz,),
        in_specs=[pl.BlockSpec((n, ch), lambda b: (b, 0)), pl.BlockSpec(dw_w.shape, const),
                  pl.BlockSpec((1, ch), const), pl.BlockSpec((1, ch), const), pl.BlockSpec((1, ch), const)],
        out_specs=pl.BlockSpec((n, ch), lambda b: (b, 0)),
        out_shape=jax.ShapeDtypeStruct((t, ch), F32),
        scratch_shapes=[pltpu.VMEM((n + 2 * reach, ch), F32)],
        compiler_params=_cparams("parallel"),
        name="conformer_conv",
    )(y, dw_w, dw_b.reshape(1, ch), ln_g.reshape(1, ch), ln_b.reshape(1, ch))


def _mixout_kernel(of_ref, ob_ref, sz_ref, cf_ref, x_ref, gt1_ref, sh2_ref, sc2_ref, gt2_ref, ng_ref, wo_ref,
                   gffn_ref, rwt_ref, wsgu_ref, wsd_ref, base_o, hn_o, lg_o):
    o = of_ref[...] + ob_ref[...]
    parts = []
    for h in range(DN_HEADS):
        oh = o[:, h * DN_D:(h + 1) * DN_D]
        parts.append(oh * lax.rsqrt(jnp.mean(oh * oh, axis=-1, keepdims=True) + EPS) * ng_ref[...])
    dn = jnp.concatenate(parts, axis=1) * sz_ref[...]
    heads = jnp.concatenate([dn, cf_ref[...]], axis=1).astype(BF16)
    x1 = x_ref[...] + gt1_ref[0] * jnp.dot(heads, wo_ref[...], preferred_element_type=F32)
    hn = (x1 * lax.rsqrt(jnp.mean(x1 * x1, axis=-1, keepdims=True) + EPS) * gffn_ref[...]
          * (1.0 + sc2_ref[0]) + sh2_ref[0])
    hn_o[...] = hn
    lg_o[...] = lax.dot_general(rwt_ref[...], hn, (((1,), (1,)), ((), ())), precision=HIGHEST,
                                preferred_element_type=F32)
    gu = jnp.dot(hn.astype(BF16), wsgu_ref[...], preferred_element_type=F32)
    ds = gu.shape[1] // 2
    act = (_silu(gu[:, :ds]) * gu[:, ds:]).astype(BF16)
    base_o[...] = x1 + gt2_ref[0] * jnp.dot(act, wsd_ref[...], preferred_element_type=F32)


def _mixer_out(o_f, o_b, sz, cfo, x2, gt1, sh2, sc2, gt2, norm_g, w_out, g_ffn, rwt, wsgu, wsd, seq, tm):
    t, d = x2.shape
    tiles = seq // tm
    const = lambda i: (0, 0)
    row_map = lambda i: (i, 0)
    mod_map = lambda i: (i // tiles, 0, 0)
    half = pl.BlockSpec((tm, DN_WIDTH), row_map)
    mod_spec = pl.BlockSpec((1, 1, d), mod_map)
    return pl.pallas_call(
        _mixout_kernel,
        grid=(t // tm,),
        in_specs=[half, half, half, half, pl.BlockSpec((tm, d), row_map),
                  mod_spec, mod_spec, mod_spec, mod_spec,
                  pl.BlockSpec((1, DN_D), const), pl.BlockSpec(w_out.shape, const), pl.BlockSpec((1, d), const),
                  pl.BlockSpec(rwt.shape, const), pl.BlockSpec(wsgu.shape, const), pl.BlockSpec(wsd.shape, const)],
        out_specs=[pl.BlockSpec((tm, d), row_map), pl.BlockSpec((tm, d), row_map),
                   pl.BlockSpec((N_EXPERTS, tm), lambda i: (0, i))],
        out_shape=[jax.ShapeDtypeStruct((t, d), F32), jax.ShapeDtypeStruct((t, d), F32),
                   jax.ShapeDtypeStruct((N_EXPERTS, t), F32)],
        compiler_params=_cparams("parallel"),
        name="mixer_out",
    )(o_f, o_b, sz, cfo, x2, gt1, sh2, sc2, gt2, norm_g, w_out, g_ffn, rwt, wsgu, wsd)


def _first_argmax(vals, idx, sentinel):
    m = jnp.max(vals, axis=0, keepdims=True)
    return m, jnp.min(jnp.where(vals == m, idx, sentinel), axis=0, keepdims=True)


def _router_kernel(lg_ref, bias_ref, e_o, r_o, wt_o, cnt_o, carry_ref):
    @pl.when(pl.program_id(0) == 0)
    def _():
        carry_ref[...] = jnp.zeros(carry_ref.shape, F32)

    tt = lg_ref.shape[1]
    neg = -jnp.inf
    scores = jax.nn.sigmoid(lg_ref[...])
    sel = scores + bias_ref[...]
    sub = lax.broadcasted_iota(jnp.int32, (GROUP_SIZE, tt), 0)

    rows = []
    for g in range(N_GROUPS):
        sg = sel[g * GROUP_SIZE:(g + 1) * GROUP_SIZE]
        m1, first = _first_argmax(sg, sub, GROUP_SIZE)
        m2 = jnp.max(jnp.where(sub == first, neg, sg), axis=0, keepdims=True)
        rows.append(m1 + m2)
    cur = jnp.concatenate(rows, axis=0)
    gidx = lax.broadcasted_iota(jnp.int32, (N_GROUPS, tt), 0)
    keep = gidx < 0
    for _ in range(TOPK_GROUPS):
        _, a = _first_argmax(cur, gidx, N_GROUPS)
        pick = gidx == a
        keep = jnp.logical_or(keep, pick)
        cur = jnp.where(pick, neg, cur)
    keep_e = jnp.concatenate([jnp.broadcast_to(keep[g:g + 1], (GROUP_SIZE, tt)) for g in range(N_GROUPS)],
                             axis=0)
    masked = jnp.where(keep_e, sel, neg)

    eidx = lax.broadcasted_iota(jnp.int32, (N_EXPERTS, tt), 0)
    e_rows, w_rows, picks = [], [], []
    for _ in range(TOP_K):
        _, a = _first_argmax(masked, eidx, N_EXPERTS)
        pick = eidx == a
        e_rows.append(a)
        w_rows.append(jnp.sum(jnp.where(pick, scores, 0.0), axis=0, keepdims=True))
        picks.append(pick)
        masked = jnp.where(pick, neg, masked)
    onehot = sum(p.astype(F32) for p in picks)
    scale = ROUTED_SCALE / sum(w_rows)

    tri = (lax.broadcasted_iota(jnp.int32, (tt, tt), 0) < lax.broadcasted_iota(jnp.int32, (tt, tt), 1))
    cum = jnp.dot(onehot.astype(BF16), tri.astype(BF16), preferred_element_type=F32) + carry_ref[:, 0:1]
    r_rows = [jnp.sum(jnp.where(p, cum, 0.0), axis=0, keepdims=True).astype(jnp.int32) for p in picks]
    carry_ref[...] = carry_ref[...] + jnp.sum(onehot, axis=1, keepdims=True)
    cnt_o[...] = carry_ref[...]

    fill = 8 - TOP_K
    e_o[...] = jnp.concatenate(e_rows + [jnp.zeros((fill, tt), jnp.int32)], axis=0)
    r_o[...] = jnp.concatenate(r_rows + [jnp.zeros((fill, tt), jnp.int32)], axis=0)
    w_pad = jnp.concatenate([w * scale for w in w_rows] + [jnp.zeros((LANES - TOP_K, tt), F32)], axis=0)
    wt_o[...] = jnp.transpose(w_pad)


def _router(lg_t, router_bias, tt):
    e, t = lg_t.shape
    idx_spec = pl.BlockSpec((8, tt), lambda i: (0, i))
    return pl.pallas_call(
        _router_kernel,
        grid=(t // tt,),
        in_specs=[pl.BlockSpec((e, tt), lambda i: (0, i)), pl.BlockSpec((e, 1), lambda i: (0, 0))],
        out_specs=[idx_spec, idx_spec, pl.BlockSpec((tt, LANES), lambda i: (i, 0)),
                   pl.BlockSpec((e, LANES), lambda i: (0, 0))],
        out_shape=[jax.ShapeDtypeStruct((8, t), jnp.int32), jax.ShapeDtypeStruct((8, t), jnp.int32),
                   jax.ShapeDtypeStruct((t, LANES), F32), jax.ShapeDtypeStruct((e, LANES), F32)],
        scratch_shapes=[pltpu.VMEM((e, LANES), F32)],
        compiler_params=_cparams("arbitrary"),
        name="router",
    )(lg_t, router_bias.reshape(e, 1))


def _dispatch_kernel(pstart_ref, count_ref, e_ref, r_ref, hn_ref, xs_hbm, zero_ref, sem):
    td = e_ref.shape[1]
    bm = zero_ref.shape[0]

    def zero_fill(e, wait):
        end = pstart_ref[e] + count_ref[e]
        aligned = pl.multiple_of((end + 7) // 8 * 8, 8)
        for i in range(7):
            @pl.when(end + i < aligned)
            def _():
                cp = pltpu.make_async_copy(zero_ref.at[pl.ds(0, 1)], xs_hbm.at[pl.ds(end + i, 1)], sem)
                cp.wait() if wait else cp.start()
        cp = pltpu.make_async_copy(zero_ref, xs_hbm.at[pl.ds(aligned, bm)], sem)
        cp.wait() if wait else cp.start()

    @pl.when(pl.program_id(0) == 0)
    def _():
        zero_ref[...] = jnp.zeros(zero_ref.shape, F32)

        def start(e, carry):
            zero_fill(e, False)
            return carry

        def wait(e, carry):
            zero_fill(e, True)
            return carry

        lax.fori_loop(0, N_EXPERTS, start, 0)
        lax.fori_loop(0, N_EXPERTS, wait, 0)

        last = N_EXPERTS - 1
        tail = (pstart_ref[last] + count_ref[last] + bm - 1) // bm

        def tail_copy(b):
            return pltpu.make_async_copy(zero_ref, xs_hbm.at[pl.ds(pl.multiple_of(b * bm, bm), bm)], sem)

        def tail_start(b, carry):
            tail_copy(b).start()
            return carry

        def tail_wait(b, carry):
            tail_copy(b).wait()
            return carry

        lax.fori_loop(tail, xs_hbm.shape[0] // bm, tail_start, 0)
        lax.fori_loop(tail, xs_hbm.shape[0] // bm, tail_wait, 0)

    def body(t, carry):
        for j in range(TOP_K):
            dest = pstart_ref[e_ref[j, t]] + r_ref[j, t]
            pltpu.make_async_copy(hn_ref.at[pl.ds(t, 1)], xs_hbm.at[pl.ds(dest, 1)], sem).start()
        return carry

    lax.fori_loop(0, td, body, 0)
    pltpu.make_async_copy(xs_hbm.at[pl.ds(0, td * TOP_K)], xs_hbm.at[pl.ds(0, td * TOP_K)], sem).wait()


def _dispatch(pstart, counts, e_idx, rank, hn, n_slots, td, bm):
    t, d = hn.shape
    smem_spec = pl.BlockSpec((8, td), lambda i, ps, cn: (0, i), memory_space=pltpu.SMEM)
    return pl.pallas_call(
        _dispatch_kernel,
        grid_spec=pltpu.PrefetchScalarGridSpec(
            num_scalar_prefetch=2, grid=(t // td,),
            in_specs=[smem_spec, smem_spec, pl.BlockSpec((td, d), lambda i, ps, cn: (i, 0))],
            out_specs=pl.BlockSpec(memory_space=pl.ANY),
            scratch_shapes=[pltpu.VMEM((bm, d), F32), pltpu.SemaphoreType.DMA(())]),
        out_shape=jax.ShapeDtypeStruct((n_slots, d), F32),
        compiler_params=_cparams("arbitrary"),
        name="dispatch",
    )(pstart, counts, e_idx, rank, hn)


def _expert_kernel(bexp_ref, bsrc_ref, nused_ref, xs_ref, wgu_ref, wd_ref, y_ref):
    del bexp_ref, bsrc_ref
    i = pl.program_id(0)

    @pl.when(i < nused_ref[0])
    def _():
        gu = jnp.dot(xs_ref[...].astype(BF16), wgu_ref[0], preferred_element_type=F32)
        de = gu.shape[1] // 2
        act = (_silu(gu[:, :de]) * gu[:, de:]).astype(BF16)
        y_ref[...] = jnp.dot(act, wd_ref[0], preferred_element_type=F32)

    @pl.when(i >= nused_ref[0])
    def _():
        y_ref[...] = jnp.zeros(y_ref.shape, F32)


def _experts(block_exp, block_src, n_used, xs, wgu, wd, bm):
    n_slots, d = xs.shape
    return pl.pallas_call(
        _expert_kernel,
        grid_spec=pltpu.PrefetchScalarGridSpec(
            num_scalar_prefetch=3, grid=(n_slots // bm,),
            in_specs=[pl.BlockSpec((bm, d), lambda i, be, bs, nu: (bs[i], 0)),
                      pl.BlockSpec((1,) + wgu.shape[1:], lambda i, be, bs, nu: (be[i], 0, 0)),
                      pl.BlockSpec((1,) + wd.shape[1:], lambda i, be, bs, nu: (be[i], 0, 0))],
            out_specs=pl.BlockSpec((bm, d), lambda i, be, bs, nu: (i, 0))),
        out_shape=jax.ShapeDtypeStruct((n_slots, d), F32),
        compiler_params=_cparams("arbitrary"),
        name="experts",
    )(block_exp, block_src, n_used, xs, wgu, wd)


def _combine_kernel(pstart_ref, e_ref, r_ref, wt_ref, base_ref, gt2_ref, gfin_ref, y_hbm, o_ref, buf, sem):
    tg = base_ref.shape[0]

    def body(t, carry):
        for j in range(TOP_K):
            src = pstart_ref[e_ref[j, t]] + r_ref[j, t]
            pltpu.make_async_copy(y_hbm.at[pl.ds(src, 1)], buf.at[pl.ds(j * tg + t, 1)], sem).start()
        return carry

    lax.fori_loop(0, tg, body, 0)
    pltpu.make_async_copy(y_hbm.at[pl.ds(0, tg * TOP_K)], buf, sem).wait()
    wt = wt_ref[...]
    routed = wt[:, 0:1] * buf[0:tg, :]
    for j in range(1, TOP_K):
        routed = routed + wt[:, j:j + 1] * buf[j * tg:(j + 1) * tg, :]
    xf = base_ref[...] + gt2_ref[0] * routed
    o_ref[...] = xf * lax.rsqrt(jnp.mean(xf * xf, axis=-1, keepdims=True) + EPS) * gfin_ref[...]


def _combine(pstart, e_idx, rank, wt, base, gt2, g_final, y, seq, tg):
    t, d = base.shape
    tiles = seq // tg
    smem_spec = pl.BlockSpec((8, tg), lambda i, ps: (0, i), memory_space=pltpu.SMEM)
    return pl.pallas_call(
        _combine_kernel,
        grid_spec=pltpu.PrefetchScalarGridSpec(
            num_scalar_prefetch=1, grid=(t // tg,),
            in_specs=[smem_spec, smem_spec,
                      pl.BlockSpec((tg, LANES), lambda i, ps: (i, 0)),
                      pl.BlockSpec((tg, d), lambda i, ps: (i, 0)),
                      pl.BlockSpec((1, 1, d), lambda i, ps: (i // tiles, 0, 0)),
                      pl.BlockSpec((1, d), lambda i, ps: (0, 0)),
                      pl.BlockSpec(memory_space=pl.ANY)],
            out_specs=pl.BlockSpec((tg, d), lambda i, ps: (i, 0)),
            scratch_shapes=[pltpu.VMEM((TOP_K * tg, d), F32), pltpu.SemaphoreType.DMA(())]),
        out_shape=jax.ShapeDtypeStruct((t, d), F32),
        compiler_params=_cparams("arbitrary"),
        name="combine",
    )(pstart, e_idx, rank, wt, base, gt2, g_final.reshape(1, d), y)


EXPERT_BLOCK = 256
ROW_TILE = 512
ROUTER_TILE = 512
GATHER_TILE = 256


def _moe_plan(counts, n_tokens, bm):
    padded = (counts + bm - 1) // bm * bm
    pad_end = jnp.cumsum(padded)
    pad_start = (pad_end - padded).astype(jnp.int32)
    n_blocks = -(-(n_tokens * TOP_K) // bm) + N_EXPERTS + 1
    n_used = (pad_end[-1] // bm).astype(jnp.int32)
    block_src = jnp.minimum(jnp.arange(n_blocks, dtype=jnp.int32), jnp.maximum(n_used - 1, 0))
    block_exp = jnp.sum((pad_end[None, :] <= (block_src * bm)[:, None]).astype(jnp.int32), axis=1)
    block_exp = jnp.minimum(block_exp, N_EXPERTS - 1)
    return pad_start, block_exp, block_src, n_used.reshape(1), n_blocks * bm


def kernel(x, c, ctx, c_ctx, w_mod, b_mod, g_mix, g_ffn, w_in, w_out, dn_conv_w, dn_a_log, dn_dt_bias,
           dn_norm_g, cf_dw_w, cf_dw_b, cf_ln_g, cf_ln_b, router_w, router_bias, exp_w_gate, exp_w_up,
           exp_w_down, sh_w_gate, sh_w_up, sh_w_down, g_final):
    assert w_mod.shape[0] == 1, "single-layer block: the context stream is never re-read"
    bsz, seq, d = x.shape
    t = bsz * seq
    x2 = x.reshape(t, d)
    st = _mixer_front(x, c, ctx, c_ctx, w_mod[0], b_mod[0], g_mix[0], w_in[0], dn_conv_w[0], dn_a_log[0],
                      dn_dt_bias[0])
    _, _, gt1, sh2, sc2, gt2 = st['lat']
    cfo = _cf_conv(st['y'], cf_dw_w[0], cf_dw_b[0], cf_ln_g[0], cf_ln_b[0], bsz)
    wsgu = jnp.concatenate([sh_w_gate[0], sh_w_up[0]], axis=1).astype(BF16)
    base, hn, lg_t = _mixer_out(st['o_f'], st['o_b'], st['sz'], cfo, x2, gt1, sh2, sc2, gt2,
                                dn_norm_g[0].reshape(1, DN_D), w_out[0].astype(BF16), g_ffn[0].reshape(1, d),
                                router_w[0].T, wsgu, sh_w_down[0].astype(BF16), seq, min(seq, ROW_TILE))
    e_idx, rank, wt, cnt = _router(lg_t, router_bias[0], min(t, ROUTER_TILE))
    counts = cnt[:, 0].astype(jnp.int32)
    pstart, block_exp, block_src, n_used, n_slots = _moe_plan(counts, t, EXPERT_BLOCK)
    xs = _dispatch(pstart, counts, e_idx, rank, hn, n_slots, min(t, GATHER_TILE), EXPERT_BLOCK)
    wgu = jnp.concatenate([exp_w_gate[0], exp_w_up[0]], axis=2).astype(BF16)
    y = _experts(block_exp, block_src, n_used, xs, wgu, exp_w_down[0].astype(BF16), EXPERT_BLOCK)
    out = _combine(pstart, e_idx, rank, wt, base, gt2, g_final, y, seq, min(seq, GATHER_TILE))
    return out.reshape(bsz, seq, d)
```

```python
import functools

import jax
import jax.numpy as jnp
from jax import lax
from jax.experimental import pallas as pl
from jax.experimental.pallas import tpu as pltpu

F32 = jnp.float32
U32 = jnp.uint32
BF16 = jnp.bfloat16
HIGHEST = lax.Precision.HIGHEST

EPS = 1e-6
LANES = 128
GRID_W = 64
DN_HEADS = 4
DN_D = 128
DN_QK = DN_HEADS * DN_D
DN_WIDTH = DN_HEADS * DN_D
DN_CONV_DIM = 2 * DN_QK + DN_WIDTH
SHORT_CONV = 7
CHUNK = 64
CF_CH = 512
CF_K = 31
N_EXPERTS = 64
TOP_K = 6
N_GROUPS = 8
GROUP_SIZE = N_EXPERTS // N_GROUPS
TOPK_GROUPS = 4
ROUTED_SCALE = 2.5
HALO = 8
VMEM_LIMIT = 56 * 1024 * 1024


def _cparams(*sem):
    return pltpu.CompilerParams(dimension_semantics=sem, vmem_limit_bytes=VMEM_LIMIT)


def _silu(v):
    return v * jax.nn.sigmoid(v)


def _bdot(a, b):
    return jnp.dot(a.astype(BF16), b.astype(BF16), preferred_element_type=F32)


def _pack_bf16_pairs(v):
    n = v.shape[1] // 2
    lo = lax.bitcast_convert_type(v[:, :n].astype(BF16).astype(F32), U32)
    hi = lax.bitcast_convert_type(v[:, n:].astype(BF16).astype(F32), U32)
    return (lo >> 16) | (hi & jnp.uint32(0xFFFF0000))


def _unpack_bf16_pairs(w):
    lo = lax.bitcast_convert_type(w << 16, F32)
    hi = lax.bitcast_convert_type(w & jnp.uint32(0xFFFF0000), F32)
    return lo, hi


def _mod_kernel(c_ref, w_ref, b_ref, o_ref):
    o_ref[...] = jnp.dot(_silu(c_ref[...]), w_ref[...], preferred_element_type=F32,
                         precision=HIGHEST) + b_ref[...]


def _modulation(cc, w_mod, b_mod):
    rows, d = cc.shape
    n = w_mod.shape[1]
    tn = 1024
    return pl.pallas_call(
        _mod_kernel,
        grid=(n // tn,),
        in_specs=[pl.BlockSpec((rows, d), lambda j: (0, 0)),
                  pl.BlockSpec((d, tn), lambda j: (0, j)),
                  pl.BlockSpec((1, tn), lambda j: (0, j))],
        out_specs=pl.BlockSpec((rows, tn), lambda j: (0, j)),
        out_shape=jax.ShapeDtypeStruct((rows, n), F32),
        compiler_params=_cparams("parallel"),
        name="modulation",
    )(cc, w_mod, b_mod.reshape(1, n))


def _chunk_cumsum(g, reverse):
    n = g.shape[0]
    pos = lax.broadcasted_iota(jnp.int32, g.shape, 0) % CHUNK
    s = 1
    while s < CHUNK:
        if reverse:
            shifted = pltpu.roll(g, n - s, 0)
            ok = pos < CHUNK - s
        else:
            shifted = pltpu.roll(g, s, 0)
            ok = pos >= s
        g = g + jnp.where(ok, shifted, 0.0)
        s *= 2
    return g


def _inproj_kernel(latent, x_ref, sh_ref, sc_ref, g_ref, wqkv_ref, wba_ref, alog_ref, dtb_ref, *rest):
    if latent:
        wz_ref, wcf_ref, qkv_o, bgc_o, bgct_o, sz_o, y_o = rest
    else:
        qkv_o, bgc_o, bgct_o = rest
    x = x_ref[...]
    xn = x * lax.rsqrt(jnp.mean(x * x, axis=-1, keepdims=True) + EPS) * g_ref[...]
    hb = (xn * (1.0 + sc_ref[0]) + sh_ref[0]).astype(BF16)
    qkv_o[...] = jnp.dot(hb, wqkv_ref[...], preferred_element_type=F32)

    ba = jnp.dot(hb, wba_ref[...], preferred_element_type=F32)
    col = lax.broadcasted_iota(jnp.int32, ba.shape, 1)
    is_beta = (col % 8) < DN_HEADS
    g = -jnp.exp(alog_ref[...]) * jax.nn.softplus(ba + dtb_ref[...])
    g = jnp.where(is_beta, 0.0, g)
    gc = jnp.where(col < 8, _chunk_cumsum(g, False), _chunk_cumsum(g, True))
    bgc = jnp.where(is_beta, jax.nn.sigmoid(ba), gc)
    bgc_o[...] = bgc
    for c in range(bgc.shape[0] // CHUNK):
        bgct_o[c] = jnp.transpose(bgc[c * CHUNK:(c + 1) * CHUNK, :])[:16, :]

    if latent:
        sz_o[...] = _silu(jnp.dot(hb, wz_ref[...], preferred_element_type=F32))
        cf = jnp.dot(hb, wcf_ref[...], preferred_element_type=F32)
        y_o[...] = cf[:, :CF_CH] * jax.nn.sigmoid(cf[:, CF_CH:])


def _in_projection(x2, sh, sc, g_mix, wqkv, wba, alog_row, dtb_row, wz, wcf, rows_per_mod, tm):
    t, d = x2.shape
    latent = wz is not None
    tiles_per_mod = rows_per_mod // tm
    const = lambda i: (0, 0)
    mod_map = lambda i: (i // tiles_per_mod, 0, 0)
    row_map = lambda i: (i, 0)
    in_specs = [pl.BlockSpec((tm, d), row_map),
                pl.BlockSpec((1, 1, d), mod_map), pl.BlockSpec((1, 1, d), mod_map),
                pl.BlockSpec((1, d), const),
                pl.BlockSpec(wqkv.shape, const), pl.BlockSpec(wba.shape, const),
                pl.BlockSpec((1, LANES), const), pl.BlockSpec((1, LANES), const)]
    args = [x2, sh, sc, g_mix, wqkv, wba, alog_row, dtb_row]
    out_specs = [pl.BlockSpec((tm, DN_CONV_DIM), row_map), pl.BlockSpec((tm, LANES), row_map),
                 pl.BlockSpec((tm // CHUNK, 16, CHUNK), lambda i: (i, 0, 0))]
    out_shape = [jax.ShapeDtypeStruct((t, DN_CONV_DIM), F32), jax.ShapeDtypeStruct((t, LANES), F32),
                 jax.ShapeDtypeStruct((t // CHUNK, 16, CHUNK), F32)]
    if latent:
        in_specs += [pl.BlockSpec(wz.shape, const), pl.BlockSpec(wcf.shape, const)]
        args += [wz, wcf]
        out_specs += [pl.BlockSpec((tm, DN_WIDTH), row_map), pl.BlockSpec((tm, CF_CH), row_map)]
        out_shape += [jax.ShapeDtypeStruct((t, DN_WIDTH), F32), jax.ShapeDtypeStruct((t, CF_CH), F32)]
    return pl.pallas_call(
        functools.partial(_inproj_kernel, latent),
        grid=(t // tm,),
        in_specs=in_specs, out_specs=out_specs, out_shape=out_shape,
        compiler_params=_cparams("parallel"),
        name="in_projection_latent" if latent else "in_projection_context",
    )(*args)


CONV_ROWS = 64


def _shortconv_kernel(tiles_per_seq, prev_ref, cur_ref, next_ref, w_ref, o_ref, ext_ref):
    i = pl.program_id(0)
    tm = cur_ref.shape[0]
    pos = i % tiles_per_seq
    ext_ref[0:HALO, :] = jnp.where(pos == 0, 0.0, prev_ref[...])
    ext_ref[HALO:HALO + tm, :] = cur_ref[...]
    ext_ref[HALO + tm:, :] = jnp.where(pos == tiles_per_seq - 1, 0.0, next_ref[...])
    reach = SHORT_CONV // 2

    def body(r, carry):
        r0 = pl.multiple_of(r * CONV_ROWS, CONV_ROWS)
        for cb in range(DN_CONV_DIM // LANES):
            cols = slice(cb * LANES, (cb + 1) * LANES)
            blk = ext_ref[pl.ds(r0, CONV_ROWS + 2 * HALO), cols]
            acc = jnp.zeros((CONV_ROWS, LANES), F32)
            for k in range(SHORT_CONV):
                s = HALO - reach + k
                acc = acc + w_ref[k:k + 1, cols] * blk[s:s + CONV_ROWS]
            a = _silu(acc)
            if cb < 2 * DN_HEADS:
                a = a * lax.rsqrt(jnp.sum(a * a, axis=-1, keepdims=True) + EPS)
            if cb < DN_HEADS:
                a = a * (DN_D ** -0.5)
            o_ref[pl.ds(r0, CONV_ROWS), cols] = a
        return carry

    lax.fori_loop(0, tm // CONV_ROWS, body, 0)


def _short_conv(qkv, conv_w, seq_len, tm):
    t, c = qkv.shape
    tiles_per_seq = seq_len // tm
    hb = tm // HALO
    n_halo_blocks = t // HALO
    return pl.pallas_call(
        functools.partial(_shortconv_kernel, tiles_per_seq),
        grid=(t // tm,),
        in_specs=[pl.BlockSpec((HALO, c), lambda i: (jnp.maximum(i * hb - 1, 0), 0)),
                  pl.BlockSpec((tm, c), lambda i: (i, 0)),
                  pl.BlockSpec((HALO, c), lambda i: (jnp.minimum((i + 1) * hb, n_halo_blocks - 1), 0)),
                  pl.BlockSpec(conv_w.shape, lambda i: (0, 0))],
        out_specs=pl.BlockSpec((tm, c), lambda i: (i, 0)),
        out_shape=jax.ShapeDtypeStruct((t, c), F32),
        scratch_shapes=[pltpu.VMEM((tm + 2 * HALO, c), F32)],
        compiler_params=_cparams("parallel"),
        name="short_conv",
    )(qkv, qkv, qkv, conv_w)


def _delta_step(refs, ci_f, ci_b, s_ref, masks, eye, want_out):
    qkv_ref, bgc_ref, bgct_ref = refs
    chains = range(2 * DN_HEADS)
    q, k, v, beta, gcol, decay, eg, gl = [], [], [], [], [], [], [], []
    for d, ci in enumerate((ci_f, ci_b)):
        rows = pl.ds(pl.multiple_of(ci * CHUNK, CHUNK), CHUNK)
        bg = bgc_ref[rows, :]
        bgt = bgct_ref[ci]
        for h in range(DN_HEADS):
            q.append(qkv_ref[rows, h * DN_D:(h + 1) * DN_D])
            k.append(qkv_ref[rows, DN_QK + h * DN_D:DN_QK + (h + 1) * DN_D])
            v.append(qkv_ref[rows, 2 * DN_QK + h * DN_D:2 * DN_QK + (h + 1) * DN_D])
            jb = d * 8 + h
            jg = d * 8 + DN_HEADS + h
            beta.append(bg[:, jb:jb + 1])
            gc = bg[:, jg:jg + 1]
            gcol.append(gc)
            decay.append(jnp.exp(jnp.where(masks[d][0], gc - bgt[jg:jg + 1, :], -jnp.inf)))
            eg.append(jnp.exp(gc))
            gl.append(gc[CHUNK - 1:CHUNK] if d == 0 else gc[0:1])

    contract_last = (((1,), (1,)), ((), ()))
    contract_first = (((0,), (0,)), ((), ()))
    gram = [lax.dot_general(jnp.concatenate([k[n], q[n]], axis=0).astype(BF16), k[n].astype(BF16),
                            contract_last, preferred_element_type=F32) for n in chains]
    lmat = [jnp.where(masks[n // DN_HEADS][1], gram[n][:CHUNK] * decay[n], 0.0) * beta[n] for n in chains]
    tinv = [eye - lmat[n] for n in chains]
    p = [_bdot(lmat[n], lmat[n]) for n in chains]
    n_sq = 2
    while n_sq < CHUNK // 2:
        x = [_bdot(jnp.concatenate([tinv[n], p[n]], axis=0), p[n]) for n in chains]
        tinv = [tinv[n] + x[n][:CHUNK] for n in chains]
        p = [x[n][CHUNK:] for n in chains]
        n_sq *= 2
    tinv = [tinv[n] + _bdot(tinv[n], p[n]) for n in chains]
    uw = [_bdot(tinv[n], jnp.concatenate([v[n] * beta[n], k[n] * (beta[n] * eg[n])], axis=1)) for n in chains]

    s = [s_ref[n] for n in chains]
    if want_out:
        wq = [_bdot(jnp.concatenate([uw[n][:, DN_D:], q[n] * eg[n]], axis=0), s[n]) for n in chains]
        ws = [wq[n][:CHUNK] for n in chains]
    else:
        ws = [_bdot(uw[n][:, DN_D:], s[n]) for n in chains]
    vb = [(uw[n][:, :DN_D] - ws[n]).astype(BF16) for n in chains]
    for n in chains:
        kd = (k[n] * jnp.exp(gl[n] - gcol[n])).astype(BF16)
        s_ref[n] = s[n] * jnp.exp(gl[n]) + lax.dot_general(kd, vb[n], contract_first,
                                                         preferred_element_type=F32)
    if not want_out:
        return None
    return [wq[n][CHUNK:] + jnp.dot((gram[n][CHUNK:] * decay[n]).astype(BF16), vb[n],
                                    preferred_element_type=F32) for n in chains]


def _delta_kernel(qc_ref, bc_ref, btc_ref, qx_ref, bx_ref, btx_ref, of_ref, ob_ref, s_ref):
    s_ref[...] = jnp.zeros(s_ref.shape, F32)
    row = lax.broadcasted_iota(jnp.int32, (CHUNK, CHUNK), 0)
    col = lax.broadcasted_iota(jnp.int32, (CHUNK, CHUNK), 1)
    masks = ((row >= col, row > col), (row <= col, row < col))
    eye = (row == col).astype(F32)
    nc_ctx = qc_ref.shape[0] // CHUNK
    nc = qx_ref.shape[0] // CHUNK

    def ctx_body(i, carry):
        _delta_step((qc_ref, bc_ref, btc_ref), i, nc_ctx - 1 - i, s_ref, masks, eye, False)
        return carry

    def lat_body(i, carry):
        ib = nc - 1 - i
        o = _delta_step((qx_ref, bx_ref, btx_ref), i, ib, s_ref, masks, eye, True)
        for h in range(DN_HEADS):
            cols = slice(h * DN_D, (h + 1) * DN_D)
            of_ref[pl.ds(pl.multiple_of(i * CHUNK, CHUNK), CHUNK), cols] = o[h]
            ob_ref[pl.ds(pl.multiple_of(ib * CHUNK, CHUNK), CHUNK), cols] = o[DN_HEADS + h]
        return carry

    lax.fori_loop(0, nc_ctx, ctx_body, 0)
    lax.fori_loop(0, nc, lat_body, 0)


def _delta_scan(qkv_c, bgc_c, bgct_c, qkv_x, bgc_x, bgct_x, bsz):
    lc = qkv_c.shape[0] // bsz
    lx = qkv_x.shape[0] // bsz
    row_map = lambda b: (b, 0)
    out = jax.ShapeDtypeStruct((bsz * lx, DN_WIDTH), F32)
    return pl.pallas_call(
        _delta_kernel,
        grid=(bsz,),
        in_specs=[pl.BlockSpec((lc, DN_CONV_DIM), row_map), pl.BlockSpec((lc, LANES), row_map),
                  pl.BlockSpec((lc // CHUNK, 16, CHUNK), lambda b: (b, 0, 0)),
                  pl.BlockSpec((lx, DN_CONV_DIM), row_map), pl.BlockSpec((lx, LANES), row_map),
                  pl.BlockSpec((lx // CHUNK, 16, CHUNK), lambda b: (b, 0, 0))],
        out_specs=[pl.BlockSpec((lx, DN_WIDTH), row_map), pl.BlockSpec((lx, DN_WIDTH), row_map)],
        out_shape=[out, out],
        scratch_shapes=[pltpu.VMEM((2 * DN_HEADS, DN_D, DN_D), F32)],
        compiler_params=_cparams("parallel"),
        name="delta_scan",
    )(qkv_c, bgc_c, bgct_c, qkv_x, bgc_x, bgct_x)


def _pad_rows(a, rows):
    return jnp.pad(a, ((0, rows - a.shape[0]), (0, 0)))


def _mixer_front(x, c, ctx, c_ctx, w_mod, b_mod, g_mix, w_in, conv_w, a_log, dt_bias):
    bsz, seq, d = x.shape
    lc = ctx.shape[1]
    off_z = DN_CONV_DIM
    off_ba = off_z + DN_WIDTH
    off_cf = off_ba + 4 * DN_HEADS

    cc = _pad_rows(jnp.concatenate([c, c_ctx[None, :]], axis=0), -(-(bsz + 1) // 8) * 8)
    mod = _modulation(cc, w_mod, b_mod)
    mods = [mod[:, j * d:(j + 1) * d] for j in range(6)]
    lat = [m[:bsz].reshape(bsz, 1, d) for m in mods]
    con = [m[bsz:bsz + 1].reshape(1, 1, d) for m in mods]

    wqkv = w_in[:, :off_z].astype(BF16)
    wz = w_in[:, off_z:off_ba].astype(BF16)
    wba = jnp.pad(w_in[:, off_ba:off_cf], ((0, 0), (0, LANES - 4 * DN_HEADS))).astype(BF16)
    wcf = w_in[:, off_cf:].astype(BF16)
    zeros4 = jnp.zeros((2, DN_HEADS), F32)
    alog_row = jnp.pad(jnp.concatenate([zeros4, a_log], axis=1).reshape(1, -1), ((0, 0), (0, LANES - 16)))
    dtb_row = jnp.pad(jnp.concatenate([zeros4, dt_bias], axis=1).reshape(1, -1), ((0, 0), (0, LANES - 16)))
    g_row = g_mix.reshape(1, d)

    qkv_c, bgc_c, bgct_c = _in_projection(ctx.reshape(bsz * lc, d), con[0], con[1], g_row, wqkv, wba,
                                          alog_row, dtb_row, None, None, bsz * lc, min(lc, 512))
    qkv_x, bgc_x, bgct_x, sz, y = _in_projection(x.reshape(bsz * seq, d), lat[0], lat[1], g_row, wqkv, wba,
                                                 alog_row, dtb_row, wz, wcf, seq, min(seq, 512))
    qkvn_c = _short_conv(qkv_c, conv_w, lc, min(lc, 512))
    qkvn_x = _short_conv(qkv_x, conv_w, seq, min(seq, 512))
    o_f, o_b = _delta_scan(qkvn_c, bgc_c, bgct_c, qkvn_x, bgc_x, bgct_x, bsz)
    return dict(mod=mod, lat=lat, qkv_x=qkv_x, qkv_c=qkv_c, sz=sz, y=y, qkvn_x=qkvn_x, bgc_x=bgc_x,
                bgct_x=bgct_x, o_f=o_f, o_b=o_b)


CF_ROWS = 32


def _cfconv_kernel(y_ref, w_ref, b_ref, lng_ref, lnb_ref, o_ref, pad_ref):
    n = y_ref.shape[0]
    reach = (CF_K // 2) * GRID_W
    pad_ref[0:reach, :] = jnp.zeros((reach, CF_CH), F32)
    pad_ref[reach:reach + n, :] = y_ref[...]
    pad_ref[reach + n:, :] = jnp.zeros((reach, CF_CH), F32)

    def body(r, carry):
        r0 = pl.multiple_of(r * CF_ROWS, CF_ROWS)
        acc = jnp.zeros((CF_ROWS, CF_CH), F32)
        for k in range(CF_K):
            acc = acc + w_ref[k:k + 1, :] * pad_ref[pl.ds(r0 + k * GRID_W, CF_ROWS), :]
        acc = acc + b_ref[...]
        mu = jnp.mean(acc, axis=-1, keepdims=True)
        xc = acc - mu
        var = jnp.mean(xc * xc, axis=-1, keepdims=True)
        o_ref[pl.ds(r0, CF_ROWS), :] = _silu(xc * lax.rsqrt(var + EPS) * lng_ref[...] + lnb_ref[...])
        return carry

    lax.fori_loop(0, n // CF_ROWS, body, 0)


def _cf_conv(y, dw_w, dw_b, ln_g, ln_b, bsz):
    t, ch = y.shape
    n = t // bsz
    reach = (CF_K // 2) * GRID_W
    const = lambda b: (0, 0)
    return pl.pallas_call(
        _cfconv_kernel,
        grid=(bsz,),
        in_specs=[pl.BlockSpec((n, ch), lambda b: (b, 0)), pl.BlockSpec(dw_w.shape, const),
                  pl.BlockSpec((1, ch), const), pl.BlockSpec((1, ch), const), pl.BlockSpec((1, ch), const)],
        out_specs=pl.BlockSpec((n, ch), lambda b: (b, 0)),
        out_shape=jax.ShapeDtypeStruct((t, ch), F32),
        scratch_shapes=[pltpu.VMEM((n + 2 * reach, ch), F32)],
        compiler_params=_cparams("parallel"),
        name="conformer_conv",
    )(y, dw_w, dw_b.reshape(1, ch), ln_g.reshape(1, ch), ln_b.reshape(1, ch))


def _mixout_kernel(of_ref, ob_ref, sz_ref, cf_ref, x_ref, gt1_ref, sh2_ref, sc2_ref, gt2_ref, ng_ref, wo_ref,
                   gffn_ref, rwt_ref, wsgu_ref, wsd_ref, base_o, hn_o, lg_o):
    o = of_ref[...] + ob_ref[...]
    parts = []
    for h in range(DN_HEADS):
        oh = o[:, h * DN_D:(h + 1) * DN_D]
        parts.append(oh * lax.rsqrt(jnp.mean(oh * oh, axis=-1, keepdims=True) + EPS) * ng_ref[...])
    dn = jnp.concatenate(parts, axis=1) * sz_ref[...]
    heads = jnp.concatenate([dn, cf_ref[...]], axis=1).astype(BF16)
    x1 = x_ref[...] + gt1_ref[0] * jnp.dot(heads, wo_ref[...], preferred_element_type=F32)
    hn = (x1 * lax.rsqrt(jnp.mean(x1 * x1, axis=-1, keepdims=True) + EPS) * gffn_ref[...]
          * (1.0 + sc2_ref[0]) + sh2_ref[0])
    hn_o[...] = _pack_bf16_pairs(hn)
    lg_o[...] = lax.dot_general(rwt_ref[...], hn, (((1,), (1,)), ((), ())), precision=HIGHEST,
                                preferred_element_type=F32)
    gu = jnp.dot(hn.astype(BF16), wsgu_ref[...], preferred_element_type=F32)
    ds = gu.shape[1] // 2
    act = (_silu(gu[:, :ds]) * gu[:, ds:]).astype(BF16)
    base_o[...] = x1 + gt2_ref[0] * jnp.dot(act, wsd_ref[...], preferred_element_type=F32)


def _mixer_out(o_f, o_b, sz, cfo, x2, gt1, sh2, sc2, gt2, norm_g, w_out, g_ffn, rwt, wsgu, wsd, seq, tm):
    t, d = x2.shape
    tiles = seq // tm
    const = lambda i: (0, 0)
    row_map = lambda i: (i, 0)
    mod_map = lambda i: (i // tiles, 0, 0)
    half = pl.BlockSpec((tm, DN_WIDTH), row_map)
    mod_spec = pl.BlockSpec((1, 1, d), mod_map)
    return pl.pallas_call(
        _mixout_kernel,
        grid=(t // tm,),
        in_specs=[half, half, half, half, pl.BlockSpec((tm, d), row_map),
                  mod_spec, mod_spec, mod_spec, mod_spec,
                  pl.BlockSpec((1, DN_D), const), pl.BlockSpec(w_out.shape, const), pl.BlockSpec((1, d), const),
                  pl.BlockSpec(rwt.shape, const), pl.BlockSpec(wsgu.shape, const), pl.BlockSpec(wsd.shape, const)],
        out_specs=[pl.BlockSpec((tm, d), row_map), pl.BlockSpec((tm, d // 2), row_map),
                   pl.BlockSpec((N_EXPERTS, tm), lambda i: (0, i))],
        out_shape=[jax.ShapeDtypeStruct((t, d), F32), jax.ShapeDtypeStruct((t, d // 2), U32),
                   jax.ShapeDtypeStruct((N_EXPERTS, t), F32)],
        compiler_params=_cparams("parallel"),
        name="mixer_out",
    )(o_f, o_b, sz, cfo, x2, gt1, sh2, sc2, gt2, norm_g, w_out, g_ffn, rwt, wsgu, wsd)


def _first_argmax(vals, idx, sentinel):
    m = jnp.max(vals, axis=0, keepdims=True)
    return m, jnp.min(jnp.where(vals == m, idx, sentinel), axis=0, keepdims=True)


def _router_kernel(lg_ref, bias_ref, e_o, r_o, wt_o, cnt_o, carry_ref):
    @pl.when(pl.program_id(0) == 0)
    def _():
        carry_ref[...] = jnp.zeros(carry_ref.shape, F32)

    tt = lg_ref.shape[1]
    neg = -jnp.inf
    scores = jax.nn.sigmoid(lg_ref[...])
    sel = scores + bias_ref[...]
    sub = lax.broadcasted_iota(jnp.int32, (GROUP_SIZE, tt), 0)

    rows = []
    for g in range(N_GROUPS):
        sg = sel[g * GROUP_SIZE:(g + 1) * GROUP_SIZE]
        m1, first = _first_argmax(sg, sub, GROUP_SIZE)
        m2 = jnp.max(jnp.where(sub == first, neg, sg), axis=0, keepdims=True)
        rows.append(m1 + m2)
    cur = jnp.concatenate(rows, axis=0)
    gidx = lax.broadcasted_iota(jnp.int32, (N_GROUPS, tt), 0)
    keep = gidx < 0
    for _ in range(TOPK_GROUPS):
        _, a = _first_argmax(cur, gidx, N_GROUPS)
        pick = gidx == a
        keep = jnp.logical_or(keep, pick)
        cur = jnp.where(pick, neg, cur)
    keep_e = jnp.concatenate([jnp.broadcast_to(keep[g:g + 1], (GROUP_SIZE, tt)) for g in range(N_GROUPS)],
                             axis=0)
    masked = jnp.where(keep_e, sel, neg)

    eidx = lax.broadcasted_iota(jnp.int32, (N_EXPERTS, tt), 0)
    e_rows, w_rows, picks = [], [], []
    for _ in range(TOP_K):
        _, a = _first_argmax(masked, eidx, N_EXPERTS)
        pick = eidx == a
        e_rows.append(a)
        w_rows.append(jnp.sum(jnp.where(pick, scores, 0.0), axis=0, keepdims=True))
        picks.append(pick)
        masked = jnp.where(pick, neg, masked)
    onehot = sum(p.astype(F32) for p in picks)
    scale = ROUTED_SCALE / sum(w_rows)

    tri = (lax.broadcasted_iota(jnp.int32, (tt, tt), 0) < lax.broadcasted_iota(jnp.int32, (tt, tt), 1))
    cum = jnp.dot(onehot.astype(BF16), tri.astype(BF16), preferred_element_type=F32) + carry_ref[:, 0:1]
    r_rows = [jnp.sum(jnp.where(p, cum, 0.0), axis=0, keepdims=True).astype(jnp.int32) for p in picks]
    carry_ref[...] = carry_ref[...] + jnp.sum(onehot, axis=1, keepdims=True)
    cnt_o[...] = carry_ref[...]

    fill = 8 - TOP_K
    e_o[...] = jnp.concatenate(e_rows + [jnp.zeros((fill, tt), jnp.int32)], axis=0)
    r_o[...] = jnp.concatenate(r_rows + [jnp.zeros((fill, tt), jnp.int32)], axis=0)
    w_pad = jnp.concatenate([w * scale for w in w_rows] + [jnp.zeros((LANES - TOP_K, tt), F32)], axis=0)
    wt_o[...] = jnp.transpose(w_pad)


def _router(lg_t, router_bias, tt):
    e, t = lg_t.shape
    idx_spec = pl.BlockSpec((8, tt), lambda i: (0, i))
    return pl.pallas_call(
        _router_kernel,
        grid=(t // tt,),
        in_specs=[pl.BlockSpec((e, tt), lambda i: (0, i)), pl.BlockSpec((e, 1), lambda i: (0, 0))],
        out_specs=[idx_spec, idx_spec, pl.BlockSpec((tt, LANES), lambda i: (i, 0)),
                   pl.BlockSpec((e, LANES), lambda i: (0, 0))],
        out_shape=[jax.ShapeDtypeStruct((8, t), jnp.int32), jax.ShapeDtypeStruct((8, t), jnp.int32),
                   jax.ShapeDtypeStruct((t, LANES), F32), jax.ShapeDtypeStruct((e, LANES), F32)],
        scratch_shapes=[pltpu.VMEM((e, LANES), F32)],
        compiler_params=_cparams("arbitrary"),
        name="router",
    )(lg_t, router_bias.reshape(e, 1))


SLOT_STRIDE = 8
ROW_GROUP = 8


def _slot_kernel(pstart_ref, e_ref, r_ref, o_ref):
    e = e_ref[...]
    acc = r_ref[...]
    for x in range(N_EXPERTS):
        acc = acc + jnp.where(e == x, pstart_ref[x], 0)
    tt = acc.shape[1]
    pad = jnp.concatenate([acc, jnp.zeros((LANES - acc.shape[0], tt), jnp.int32)], axis=0)
    o_ref[...] = jnp.transpose(pad)[:, :SLOT_STRIDE]


def _slot_index(pstart, e_idx, rank, tt):
    rows, t = e_idx.shape
    spec = pl.BlockSpec((rows, tt), lambda i, ps: (0, i))
    slot = pl.pallas_call(
        _slot_kernel,
        grid_spec=pltpu.PrefetchScalarGridSpec(
            num_scalar_prefetch=1, grid=(t // tt,), in_specs=[spec, spec],
            out_specs=pl.BlockSpec((tt, SLOT_STRIDE), lambda i, ps: (i, 0))),
        out_shape=jax.ShapeDtypeStruct((t, SLOT_STRIDE), jnp.int32),
        compiler_params=_cparams("parallel"),
        name="slot_index",
    )(pstart, e_idx, rank)
    return slot.reshape(t * SLOT_STRIDE)


def _row_copies(n_tokens, make_copy):
    def body(g, carry):
        base = g * (ROW_GROUP * SLOT_STRIDE)
        for s in range(ROW_GROUP):
            for j in range(TOP_K):
                make_copy(g, s, j, base + (s * SLOT_STRIDE + j)).start(priority=(s * TOP_K + j) % 2)
        return carry

    lax.fori_loop(0, n_tokens // ROW_GROUP, body, 0)


def _dispatch_kernel(pstart_ref, count_ref, slot_ref, hn_ref, xs_hbm, zero_ref, sem):
    td = hn_ref.shape[0] * ROW_GROUP
    bm = zero_ref.shape[0]

    def zero_fill(e, wait):
        end = pstart_ref[e] + count_ref[e]
        aligned = pl.multiple_of((end + 7) // 8 * 8, 8)
        for i in range(7):
            @pl.when(end + i < aligned)
            def _():
                cp = pltpu.make_async_copy(zero_ref.at[pl.ds(0, 1)], xs_hbm.at[pl.ds(end + i, 1)], sem)
                cp.wait() if wait else cp.start()
        cp = pltpu.make_async_copy(zero_ref, xs_hbm.at[pl.ds(aligned, bm)], sem)
        cp.wait() if wait else cp.start()

    @pl.when(pl.program_id(0) == 0)
    def _():
        zero_ref[...] = jnp.zeros(zero_ref.shape, zero_ref.dtype)

        def start(e, carry):
            zero_fill(e, False)
            return carry

        def wait(e, carry):
            zero_fill(e, True)
            return carry

        lax.fori_loop(0, N_EXPERTS, start, 0)
        lax.fori_loop(0, N_EXPERTS, wait, 0)

        last = N_EXPERTS - 1
        tail = (pstart_ref[last] + count_ref[last] + bm - 1) // bm

        def tail_copy(b):
            return pltpu.make_async_copy(zero_ref, xs_hbm.at[pl.ds(pl.multiple_of(b * bm, bm), bm)], sem)

        def tail_start(b, carry):
            tail_copy(b).start()
            return carry

        def tail_wait(b, carry):
            tail_copy(b).wait()
            return carry

        lax.fori_loop(tail, xs_hbm.shape[0] // bm, tail_start, 0)
        lax.fori_loop(tail, xs_hbm.shape[0] // bm, tail_wait, 0)

    _row_copies(td, lambda g, s, j, i: pltpu.make_async_copy(
        hn_ref.at[g, pl.ds(s, 1)], xs_hbm.at[pl.ds(slot_ref[i], 1)], sem))
    pltpu.make_async_copy(xs_hbm.at[pl.ds(0, td * TOP_K)], xs_hbm.at[pl.ds(0, td * TOP_K)], sem).wait()


def _dispatch(pstart, counts, slot, hn, n_slots, td, bm):
    t, d = hn.shape
    return pl.pallas_call(
        _dispatch_kernel,
        grid_spec=pltpu.PrefetchScalarGridSpec(
            num_scalar_prefetch=2, grid=(t // td,),
            in_specs=[pl.BlockSpec((td * SLOT_STRIDE,), lambda i, ps, cn: (i,), memory_space=pltpu.SMEM),
                      pl.BlockSpec((td // ROW_GROUP, ROW_GROUP, d), lambda i, ps, cn: (i, 0, 0))],
            out_specs=pl.BlockSpec(memory_space=pl.ANY),
            scratch_shapes=[pltpu.VMEM((bm, d), hn.dtype), pltpu.SemaphoreType.DMA(())]),
        out_shape=jax.ShapeDtypeStruct((n_slots, d), hn.dtype),
        compiler_params=_cparams("arbitrary"),
        name="dispatch",
    )(pstart, counts, slot, hn.reshape(t // ROW_GROUP, ROW_GROUP, d))


def _expert_kernel(bexp_ref, bsrc_ref, nused_ref, xs_ref, wgu_ref, wd_ref, y_ref):
    del bexp_ref, bsrc_ref
    i = pl.program_id(0)

    @pl.when(i < nused_ref[0])
    def _():
        xb = jnp.concatenate(_unpack_bf16_pairs(xs_ref[...]), axis=1).astype(BF16)
        gu = jnp.dot(xb, wgu_ref[0], preferred_element_type=F32)
        de = gu.shape[1] // 2
        act = (_silu(gu[:, :de]) * gu[:, de:]).astype(BF16)
        y_ref[...] = _pack_bf16_pairs(jnp.dot(act, wd_ref[0], preferred_element_type=F32))

    @pl.when(i >= nused_ref[0])
    def _():
        y_ref[...] = jnp.zeros(y_ref.shape, y_ref.dtype)


def _experts(block_exp, block_src, n_used, xs, wgu, wd, bm):
    n_slots, d = xs.shape
    return pl.pallas_call(
        _expert_kernel,
        grid_spec=pltpu.PrefetchScalarGridSpec(
            num_scalar_prefetch=3, grid=(n_slots // bm,),
            in_specs=[pl.BlockSpec((bm, d), lambda i, be, bs, nu: (bs[i], 0)),
                      pl.BlockSpec((1,) + wgu.shape[1:], lambda i, be, bs, nu: (be[i], 0, 0)),
                      pl.BlockSpec((1,) + wd.shape[1:], lambda i, be, bs, nu: (be[i], 0, 0))],
            out_specs=pl.BlockSpec((bm, d), lambda i, be, bs, nu: (i, 0))),
        out_shape=jax.ShapeDtypeStruct((n_slots, d), xs.dtype),
        compiler_params=_cparams("arbitrary"),
        name="experts",
    )(block_exp, block_src, n_used, xs, wgu, wd)


def _combine_kernel(slot_ref, wt_ref, base_ref, gt2_ref, gfin_ref, y_hbm, o_ref, buf, sem):
    tg = base_ref.shape[0]
    _row_copies(tg, lambda g, s, j, i: pltpu.make_async_copy(
        y_hbm.at[pl.ds(slot_ref[i], 1)], buf.at[j, g, pl.ds(s, 1)], sem))
    pltpu.make_async_copy(buf, buf, sem).wait()
    wt = wt_ref[...]
    lo = hi = None
    for j in range(TOP_K):
        ylo, yhi = _unpack_bf16_pairs(buf[j].reshape(tg, buf.shape[-1]))
        w = wt[:, j:j + 1]
        lo = w * ylo if lo is None else lo + w * ylo
        hi = w * yhi if hi is None else hi + w * yhi
    xf = base_ref[...] + gt2_ref[0] * jnp.concatenate([lo, hi], axis=1)
    o_ref[...] = xf * lax.rsqrt(jnp.mean(xf * xf, axis=-1, keepdims=True) + EPS) * gfin_ref[...]


def _combine(slot, wt, base, gt2, g_final, y, seq, tg):
    t, d = base.shape
    tiles = seq // tg
    return pl.pallas_call(
        _combine_kernel,
        grid=(t // tg,),
        in_specs=[pl.BlockSpec((tg * SLOT_STRIDE,), lambda i: (i,), memory_space=pltpu.SMEM),
                  pl.BlockSpec((tg, LANES), lambda i: (i, 0)),
                  pl.BlockSpec((tg, d), lambda i: (i, 0)),
                  pl.BlockSpec((1, 1, d), lambda i: (i // tiles, 0, 0)),
                  pl.BlockSpec((1, d), lambda i: (0, 0)),
                  pl.BlockSpec(memory_space=pl.ANY)],
        out_specs=pl.BlockSpec((tg, d), lambda i: (i, 0)),
        out_shape=jax.ShapeDtypeStruct((t, d), F32),
        scratch_shapes=[pltpu.VMEM((TOP_K, tg // ROW_GROUP, ROW_GROUP, y.shape[1]), y.dtype),
                        pltpu.SemaphoreType.DMA(())],
        compiler_params=_cparams("arbitrary"),
        name="combine",
    )(slot, wt, base, gt2, g_final.reshape(1, d), y)


EXPERT_BLOCK = 256
ROW_TILE = 512
ROUTER_TILE = 512
GATHER_TILE = 256
SLOT_TILE = 4096


def _moe_plan(counts, n_tokens, bm):
    padded = (counts + bm - 1) // bm * bm
    pad_end = jnp.cumsum(padded)
    pad_start = (pad_end - padded).astype(jnp.int32)
    n_blocks = -(-(n_tokens * TOP_K) // bm) + N_EXPERTS + 1
    n_used = (pad_end[-1] // bm).astype(jnp.int32)
    block_src = jnp.minimum(jnp.arange(n_blocks, dtype=jnp.int32), jnp.maximum(n_used - 1, 0))
    block_exp = jnp.sum((pad_end[None, :] <= (block_src * bm)[:, None]).astype(jnp.int32), axis=1)
    block_exp = jnp.minimum(block_exp, N_EXPERTS - 1)
    return pad_start, block_exp, block_src, n_used.reshape(1), n_blocks * bm


def kernel(x, c, ctx, c_ctx, w_mod, b_mod, g_mix, g_ffn, w_in, w_out, dn_conv_w, dn_a_log, dn_dt_bias,
           dn_norm_g, cf_dw_w, cf_dw_b, cf_ln_g, cf_ln_b, router_w, router_bias, exp_w_gate, exp_w_up,
           exp_w_down, sh_w_gate, sh_w_up, sh_w_down, g_final):
    assert w_mod.shape[0] == 1, "single-layer block: the context stream is never re-read"
    bsz, seq, d = x.shape
    t = bsz * seq
    x2 = x.reshape(t, d)
    st = _mixer_front(x, c, ctx, c_ctx, w_mod[0], b_mod[0], g_mix[0], w_in[0], dn_conv_w[0], dn_a_log[0],
                      dn_dt_bias[0])
    _, _, gt1, sh2, sc2, gt2 = st['lat']
    cfo = _cf_conv(st['y'], cf_dw_w[0], cf_dw_b[0], cf_ln_g[0], cf_ln_b[0], bsz)
    wsgu = jnp.concatenate([sh_w_gate[0], sh_w_up[0]], axis=1).astype(BF16)
    base, hn, lg_t = _mixer_out(st['o_f'], st['o_b'], st['sz'], cfo, x2, gt1, sh2, sc2, gt2,
                                dn_norm_g[0].reshape(1, DN_D), w_out[0].astype(BF16), g_ffn[0].reshape(1, d),
                                router_w[0].T, wsgu, sh_w_down[0].astype(BF16), seq, min(seq, ROW_TILE))
    e_idx, rank, wt, cnt = _router(lg_t, router_bias[0], min(t, ROUTER_TILE))
    counts = cnt[:, 0].astype(jnp.int32)
    pstart, block_exp, block_src, n_used, n_slots = _moe_plan(counts, t, EXPERT_BLOCK)
    slot = _slot_index(pstart, e_idx, rank, min(t, SLOT_TILE))
    xs = _dispatch(pstart, counts, slot, hn, n_slots, min(t, GATHER_TILE), EXPERT_BLOCK)
    wgu = jnp.concatenate([exp_w_gate[0], exp_w_up[0]], axis=2).astype(BF16)
    y = _experts(block_exp, block_src, n_used, xs, wgu, exp_w_down[0].astype(BF16), EXPERT_BLOCK)
    out = _combine(slot, wt, base, gt2, g_final, y, seq, min(seq, GATHER_TILE))
    return out.reshape(bsz, seq, d)
```

```python
import functools

import jax
import jax.numpy as jnp
from jax import lax
from jax.experimental import pallas as pl
from jax.experimental.pallas import tpu as pltpu

F32 = jnp.float32
U32 = jnp.uint32
BF16 = jnp.bfloat16
HIGHEST = lax.Precision.HIGHEST

EPS = 1e-6
LANES = 128
GRID_W = 64
DN_HEADS = 4
DN_D = 128
DN_QK = DN_HEADS * DN_D
DN_WIDTH = DN_HEADS * DN_D
DN_CONV_DIM = 2 * DN_QK + DN_WIDTH
SHORT_CONV = 7
CHUNK = 64
CF_CH = 512
CF_K = 31
N_EXPERTS = 64
TOP_K = 6
N_GROUPS = 8
GROUP_SIZE = N_EXPERTS // N_GROUPS
TOPK_GROUPS = 4
ROUTED_SCALE = 2.5
HALO = 8
VMEM_LIMIT = 56 * 1024 * 1024


def _cparams(*sem):
    return pltpu.CompilerParams(dimension_semantics=sem, vmem_limit_bytes=VMEM_LIMIT)


def _silu(v):
    return v * jax.nn.sigmoid(v)


def _bdot(a, b):
    return jnp.dot(a.astype(BF16), b.astype(BF16), preferred_element_type=F32)


def _pack_bf16_pairs(v):
    n = v.shape[1] // 2
    lo = lax.bitcast_convert_type(v[:, :n].astype(BF16).astype(F32), U32)
    hi = lax.bitcast_convert_type(v[:, n:].astype(BF16).astype(F32), U32)
    return (lo >> 16) | (hi & jnp.uint32(0xFFFF0000))


def _unpack_bf16_pairs(w):
    lo = lax.bitcast_convert_type(w << 16, F32)
    hi = lax.bitcast_convert_type(w & jnp.uint32(0xFFFF0000), F32)
    return lo, hi


def _mod_kernel(c_ref, w_ref, b_ref, o_ref):
    o_ref[...] = jnp.dot(_silu(c_ref[...]), w_ref[...], preferred_element_type=F32,
                         precision=HIGHEST) + b_ref[...]


def _modulation(cc, w_mod, b_mod):
    rows, d = cc.shape
    n = w_mod.shape[1]
    tn = 1024
    return pl.pallas_call(
        _mod_kernel,
        grid=(n // tn,),
        in_specs=[pl.BlockSpec((rows, d), lambda j: (0, 0)),
                  pl.BlockSpec((d, tn), lambda j: (0, j)),
                  pl.BlockSpec((1, tn), lambda j: (0, j))],
        out_specs=pl.BlockSpec((rows, tn), lambda j: (0, j)),
        out_shape=jax.ShapeDtypeStruct((rows, n), F32),
        compiler_params=_cparams("parallel"),
        name="modulation",
    )(cc, w_mod, b_mod.reshape(1, n))


def _chunk_cumsum(g, reverse):
    n = g.shape[0]
    pos = lax.broadcasted_iota(jnp.int32, g.shape, 0) % CHUNK
    s = 1
    while s < CHUNK:
        if reverse:
            shifted = pltpu.roll(g, n - s, 0)
            ok = pos < CHUNK - s
        else:
            shifted = pltpu.roll(g, s, 0)
            ok = pos >= s
        g = g + jnp.where(ok, shifted, 0.0)
        s *= 2
    return g


def _inproj_kernel(latent, x_ref, sh_ref, sc_ref, g_ref, wqkv_ref, wba_ref, alog_ref, dtb_ref, *rest):
    if latent:
        wz_ref, wcf_ref, qkv_o, bgc_o, bgct_o, sz_o, y_o = rest
    else:
        qkv_o, bgc_o, bgct_o = rest
    x = x_ref[...]
    xn = x * lax.rsqrt(jnp.mean(x * x, axis=-1, keepdims=True) + EPS) * g_ref[...]
    hb = (xn * (1.0 + sc_ref[0]) + sh_ref[0]).astype(BF16)
    qkv_o[...] = jnp.dot(hb, wqkv_ref[...], preferred_element_type=F32)

    ba = jnp.dot(hb, wba_ref[...], preferred_element_type=F32)
    col = lax.broadcasted_iota(jnp.int32, ba.shape, 1)
    is_beta = (col % 8) < DN_HEADS
    g = -jnp.exp(alog_ref[...]) * jax.nn.softplus(ba + dtb_ref[...])
    g = jnp.where(is_beta, 0.0, g)
    gc = jnp.where(col < 8, _chunk_cumsum(g, False), _chunk_cumsum(g, True))
    bgc = jnp.where(is_beta, jax.nn.sigmoid(ba), gc)
    bgc_o[...] = bgc
    for c in range(bgc.shape[0] // CHUNK):
        bgct_o[c] = jnp.transpose(bgc[c * CHUNK:(c + 1) * CHUNK, :])[:16, :]

    if latent:
        sz_o[...] = _silu(jnp.dot(hb, wz_ref[...], preferred_element_type=F32))
        cf = jnp.dot(hb, wcf_ref[...], preferred_element_type=F32)
        y_o[...] = cf[:, :CF_CH] * jax.nn.sigmoid(cf[:, CF_CH:])


def _in_projection(x2, sh, sc, g_mix, wqkv, wba, alog_row, dtb_row, wz, wcf, rows_per_mod, tm):
    t, d = x2.shape
    latent = wz is not None
    tiles_per_mod = rows_per_mod // tm
    const = lambda i: (0, 0)
    mod_map = lambda i: (i // tiles_per_mod, 0, 0)
    row_map = lambda i: (i, 0)
    in_specs = [pl.BlockSpec((tm, d), row_map),
                pl.BlockSpec((1, 1, d), mod_map), pl.BlockSpec((1, 1, d), mod_map),
                pl.BlockSpec((1, d), const),
                pl.BlockSpec(wqkv.shape, const), pl.BlockSpec(wba.shape, const),
                pl.BlockSpec((1, LANES), const), pl.BlockSpec((1, LANES), const)]
    args = [x2, sh, sc, g_mix, wqkv, wba, alog_row, dtb_row]
    out_specs = [pl.BlockSpec((tm, DN_CONV_DIM), row_map), pl.BlockSpec((tm, LANES), row_map),
                 pl.BlockSpec((tm // CHUNK, 16, CHUNK), lambda i: (i, 0, 0))]
    out_shape = [jax.ShapeDtypeStruct((t, DN_CONV_DIM), F32), jax.ShapeDtypeStruct((t, LANES), F32),
                 jax.ShapeDtypeStruct((t // CHUNK, 16, CHUNK), F32)]
    if latent:
        in_specs += [pl.BlockSpec(wz.shape, const), pl.BlockSpec(wcf.shape, const)]
        args += [wz, wcf]
        out_specs += [pl.BlockSpec((tm, DN_WIDTH), row_map), pl.BlockSpec((tm, CF_CH), row_map)]
        out_shape += [jax.ShapeDtypeStruct((t, DN_WIDTH), F32), jax.ShapeDtypeStruct((t, CF_CH), F32)]
    return pl.pallas_call(
        functools.partial(_inproj_kernel, latent),
        grid=(t // tm,),
        in_specs=in_specs, out_specs=out_specs, out_shape=out_shape,
        compiler_params=_cparams("parallel"),
        name="in_projection_latent" if latent else "in_projection_context",
    )(*args)


CONV_ROWS = 64


def _shortconv_kernel(tiles_per_seq, prev_ref, cur_ref, next_ref, w_ref, o_ref, ext_ref):
    i = pl.program_id(0)
    tm = cur_ref.shape[0]
    pos = i % tiles_per_seq
    ext_ref[0:HALO, :] = jnp.where(pos == 0, 0.0, prev_ref[...])
    ext_ref[HALO:HALO + tm, :] = cur_ref[...]
    ext_ref[HALO + tm:, :] = jnp.where(pos == tiles_per_seq - 1, 0.0, next_ref[...])
    reach = SHORT_CONV // 2

    for r0 in range(0, tm, CONV_ROWS):
        for cb in range(DN_CONV_DIM // LANES):
            cols = slice(cb * LANES, (cb + 1) * LANES)
            acc = jnp.zeros((CONV_ROWS, LANES), F32)
            for k in range(SHORT_CONV):
                s = r0 + HALO - reach + k
                acc = acc + w_ref[k:k + 1, cols] * ext_ref[s:s + CONV_ROWS, cols]
            a = _silu(acc)
            if cb < 2 * DN_HEADS:
                a = a * lax.rsqrt(jnp.sum(a * a, axis=-1, keepdims=True) + EPS)
            if cb < DN_HEADS:
                a = a * (DN_D ** -0.5)
            o_ref[r0:r0 + CONV_ROWS, cols] = a


def _short_conv(qkv, conv_w, seq_len, tm):
    t, c = qkv.shape
    tiles_per_seq = seq_len // tm
    hb = tm // HALO
    n_halo_blocks = t // HALO
    return pl.pallas_call(
        functools.partial(_shortconv_kernel, tiles_per_seq),
        grid=(t // tm,),
        in_specs=[pl.BlockSpec((HALO, c), lambda i: (jnp.maximum(i * hb - 1, 0), 0)),
                  pl.BlockSpec((tm, c), lambda i: (i, 0)),
                  pl.BlockSpec((HALO, c), lambda i: (jnp.minimum((i + 1) * hb, n_halo_blocks - 1), 0)),
                  pl.BlockSpec(conv_w.shape, lambda i: (0, 0))],
        out_specs=pl.BlockSpec((tm, c), lambda i: (i, 0)),
        out_shape=jax.ShapeDtypeStruct((t, c), F32),
        scratch_shapes=[pltpu.VMEM((tm + 2 * HALO, c), F32)],
        compiler_params=_cparams("parallel"),
        name="short_conv",
    )(qkv, qkv, qkv, conv_w)


def _delta_step(refs, ci_f, ci_b, s_ref, masks, eye, want_out):
    qkv_ref, bgc_ref, bgct_ref = refs
    chains = range(2 * DN_HEADS)
    q, k, v, beta, gcol, decay, eg, gl = [], [], [], [], [], [], [], []
    for d, ci in enumerate((ci_f, ci_b)):
        rows = pl.ds(pl.multiple_of(ci * CHUNK, CHUNK), CHUNK)
        bg = bgc_ref[rows, :]
        bgt = bgct_ref[ci]
        for h in range(DN_HEADS):
            q.append(qkv_ref[rows, h * DN_D:(h + 1) * DN_D])
            k.append(qkv_ref[rows, DN_QK + h * DN_D:DN_QK + (h + 1) * DN_D])
            v.append(qkv_ref[rows, 2 * DN_QK + h * DN_D:2 * DN_QK + (h + 1) * DN_D])
            jb = d * 8 + h
            jg = d * 8 + DN_HEADS + h
            beta.append(bg[:, jb:jb + 1])
            gc = bg[:, jg:jg + 1]
            gcol.append(gc)
            decay.append(jnp.exp(jnp.where(masks[d][0], gc - bgt[jg:jg + 1, :], -jnp.inf)))
            eg.append(jnp.exp(gc))
            gl.append(gc[CHUNK - 1:CHUNK] if d == 0 else gc[0:1])

    contract_last = (((1,), (1,)), ((), ()))
    contract_first = (((0,), (0,)), ((), ()))
    gram = [lax.dot_general(jnp.concatenate([k[n], q[n]], axis=0).astype(BF16), k[n].astype(BF16),
                            contract_last, preferred_element_type=F32) for n in chains]
    lmat = [jnp.where(masks[n // DN_HEADS][1], gram[n][:CHUNK] * decay[n], 0.0) * beta[n] for n in chains]
    tinv = [eye - lmat[n] for n in chains]
    p = [_bdot(lmat[n], lmat[n]) for n in chains]
    n_sq = 2
    while n_sq < CHUNK // 2:
        x = [_bdot(jnp.concatenate([tinv[n], p[n]], axis=0), p[n]) for n in chains]
        tinv = [tinv[n] + x[n][:CHUNK] for n in chains]
        p = [x[n][CHUNK:] for n in chains]
        n_sq *= 2
    tinv = [tinv[n] + _bdot(tinv[n], p[n]) for n in chains]
    uw = [_bdot(tinv[n], jnp.concatenate([v[n] * beta[n], k[n] * (beta[n] * eg[n])], axis=1)) for n in chains]

    s = [s_ref[n] for n in chains]
    if want_out:
        wq = [_bdot(jnp.concatenate([uw[n][:, DN_D:], q[n] * eg[n]], axis=0), s[n]) for n in chains]
        ws = [wq[n][:CHUNK] for n in chains]
    else:
        ws = [_bdot(uw[n][:, DN_D:], s[n]) for n in chains]
    vb = [(uw[n][:, :DN_D] - ws[n]).astype(BF16) for n in chains]
    for n in chains:
        kd = (k[n] * jnp.exp(gl[n] - gcol[n])).astype(BF16)
        s_ref[n] = s[n] * jnp.exp(gl[n]) + lax.dot_general(kd, vb[n], contract_first,
                                                         preferred_element_type=F32)
    if not want_out:
        return None
    return [wq[n][CHUNK:] + jnp.dot((gram[n][CHUNK:] * decay[n]).astype(BF16), vb[n],
                                    preferred_element_type=F32) for n in chains]


def _delta_kernel(qc_ref, bc_ref, btc_ref, qx_ref, bx_ref, btx_ref, of_ref, ob_ref, s_ref):
    s_ref[...] = jnp.zeros(s_ref.shape, F32)
    row = lax.broadcasted_iota(jnp.int32, (CHUNK, CHUNK), 0)
    col = lax.broadcasted_iota(jnp.int32, (CHUNK, CHUNK), 1)
    masks = ((row >= col, row > col), (row <= col, row < col))
    eye = (row == col).astype(F32)
    nc_ctx = qc_ref.shape[0] // CHUNK
    nc = qx_ref.shape[0] // CHUNK

    def ctx_body(i, carry):
        _delta_step((qc_ref, bc_ref, btc_ref), i, nc_ctx - 1 - i, s_ref, masks, eye, False)
        return carry

    def lat_body(i, carry):
        ib = nc - 1 - i
        o = _delta_step((qx_ref, bx_ref, btx_ref), i, ib, s_ref, masks, eye, True)
        for h in range(DN_HEADS):
            cols = slice(h * DN_D, (h + 1) * DN_D)
            of_ref[pl.ds(pl.multiple_of(i * CHUNK, CHUNK), CHUNK), cols] = o[h]
            ob_ref[pl.ds(pl.multiple_of(ib * CHUNK, CHUNK), CHUNK), cols] = o[DN_HEADS + h]
        return carry

    lax.fori_loop(0, nc_ctx, ctx_body, 0)
    lax.fori_loop(0, nc, lat_body, 0)


def _delta_scan(qkv_c, bgc_c, bgct_c, qkv_x, bgc_x, bgct_x, bsz):
    lc = qkv_c.shape[0] // bsz
    lx = qkv_x.shape[0] // bsz
    row_map = lambda b: (b, 0)
    out = jax.ShapeDtypeStruct((bsz * lx, DN_WIDTH), F32)
    return pl.pallas_call(
        _delta_kernel,
        grid=(bsz,),
        in_specs=[pl.BlockSpec((lc, DN_CONV_DIM), row_map), pl.BlockSpec((lc, LANES), row_map),
                  pl.BlockSpec((lc // CHUNK, 16, CHUNK), lambda b: (b, 0, 0)),
                  pl.BlockSpec((lx, DN_CONV_DIM), row_map), pl.BlockSpec((lx, LANES), row_map),
                  pl.BlockSpec((lx // CHUNK, 16, CHUNK), lambda b: (b, 0, 0))],
        out_specs=[pl.BlockSpec((lx, DN_WIDTH), row_map), pl.BlockSpec((lx, DN_WIDTH), row_map)],
        out_shape=[out, out],
        scratch_shapes=[pltpu.VMEM((2 * DN_HEADS, DN_D, DN_D), F32)],
        compiler_params=_cparams("parallel"),
        name="delta_scan",
    )(qkv_c, bgc_c, bgct_c, qkv_x, bgc_x, bgct_x)


def _pad_rows(a, rows):
    return jnp.pad(a, ((0, rows - a.shape[0]), (0, 0)))


def _mixer_front(x, c, ctx, c_ctx, w_mod, b_mod, g_mix, w_in, conv_w, a_log, dt_bias):
    bsz, seq, d = x.shape
    lc = ctx.shape[1]
    off_z = DN_CONV_DIM
    off_ba = off_z + DN_WIDTH
    off_cf = off_ba + 4 * DN_HEADS

    cc = _pad_rows(jnp.concatenate([c, c_ctx[None, :]], axis=0), -(-(bsz + 1) // 8) * 8)
    mod = _modulation(cc, w_mod, b_mod)
    mods = [mod[:, j * d:(j + 1) * d] for j in range(6)]
    lat = [m[:bsz].reshape(bsz, 1, d) for m in mods]
    con = [m[bsz:bsz + 1].reshape(1, 1, d) for m in mods]

    wqkv = w_in[:, :off_z].astype(BF16)
    wz = w_in[:, off_z:off_ba].astype(BF16)
    wba = jnp.pad(w_in[:, off_ba:off_cf], ((0, 0), (0, LANES - 4 * DN_HEADS))).astype(BF16)
    wcf = w_in[:, off_cf:].astype(BF16)
    zeros4 = jnp.zeros((2, DN_HEADS), F32)
    alog_row = jnp.pad(jnp.concatenate([zeros4, a_log], axis=1).reshape(1, -1), ((0, 0), (0, LANES - 16)))
    dtb_row = jnp.pad(jnp.concatenate([zeros4, dt_bias], axis=1).reshape(1, -1), ((0, 0), (0, LANES - 16)))
    g_row = g_mix.reshape(1, d)

    qkv_c, bgc_c, bgct_c = _in_projection(ctx.reshape(bsz * lc, d), con[0], con[1], g_row, wqkv, wba,
                                          alog_row, dtb_row, None, None, bsz * lc, min(lc, 512))
    qkv_x, bgc_x, bgct_x, sz, y = _in_projection(x.reshape(bsz * seq, d), lat[0], lat[1], g_row, wqkv, wba,
                                                 alog_row, dtb_row, wz, wcf, seq, min(seq, 512))
    qkvn_c = _short_conv(qkv_c, conv_w, lc, min(lc, 512))
    qkvn_x = _short_conv(qkv_x, conv_w, seq, min(seq, 512))
    o_f, o_b = _delta_scan(qkvn_c, bgc_c, bgct_c, qkvn_x, bgc_x, bgct_x, bsz)
    return dict(mod=mod, lat=lat, qkv_x=qkv_x, qkv_c=qkv_c, sz=sz, y=y, qkvn_x=qkvn_x, bgc_x=bgc_x,
                bgct_x=bgct_x, o_f=o_f, o_b=o_b)


CF_ROWS = 32


def _cfconv_kernel(y_ref, w_ref, b_ref, lng_ref, lnb_ref, o_ref, pad_ref):
    n = y_ref.shape[0]
    reach = (CF_K // 2) * GRID_W
    pad_ref[0:reach, :] = jnp.zeros((reach, CF_CH), F32)
    pad_ref[reach:reach + n, :] = y_ref[...]
    pad_ref[reach + n:, :] = jnp.zeros((reach, CF_CH), F32)

    def body(r, carry):
        r0 = pl.multiple_of(r * CF_ROWS, CF_ROWS)
        acc = jnp.zeros((CF_ROWS, CF_CH), F32)
        for k in range(CF_K):
            acc = acc + w_ref[k:k + 1, :] * pad_ref[pl.ds(r0 + k * GRID_W, CF_ROWS), :]
        acc = acc + b_ref[...]
        mu = jnp.mean(acc, axis=-1, keepdims=True)
        xc = acc - mu
        var = jnp.mean(xc * xc, axis=-1, keepdims=True)
        o_ref[pl.ds(r0, CF_ROWS), :] = _silu(xc * lax.rsqrt(var + EPS) * lng_ref[...] + lnb_ref[...])
        return carry

    lax.fori_loop(0, n // CF_ROWS, body, 0, unroll=4)


def _cf_conv(y, dw_w, dw_b, ln_g, ln_b, bsz):
    t, ch = y.shape
    n = t // bsz
    reach = (CF_K // 2) * GRID_W
    const = lambda b: (0, 0)
    return pl.pallas_call(
        _cfconv_kernel,
        grid=(bsz,),
        in_specs=[pl.BlockSpec((n, ch), lambda b: (b, 0)), pl.BlockSpec(dw_w.shape, const),
                  pl.BlockSpec((1, ch), const), pl.BlockSpec((1, ch), const), pl.BlockSpec((1, ch), const)],
        out_specs=pl.BlockSpec((n, ch), lambda b: (b, 0)),
        out_shape=jax.ShapeDtypeStruct((t, ch), F32),
        scratch_shapes=[pltpu.VMEM((n + 2 * reach, ch), F32)],
        compiler_params=_cparams("parallel"),
        name="conformer_conv",
    )(y, dw_w, dw_b.reshape(1, ch), ln_g.reshape(1, ch), ln_b.reshape(1, ch))


def _mixout_kernel(of_ref, ob_ref, sz_ref, cf_ref, x_ref, gt1_ref, sh2_ref, sc2_ref, gt2_ref, ng_ref, wo_ref,
                   gffn_ref, rwt_ref, wsgu_ref, wsd_ref, base_o, hn_o, lg_o):
    o = of_ref[...] + ob_ref[...]
    parts = []
    for h in range(DN_HEADS):
        oh = o[:, h * DN_D:(h + 1) * DN_D]
        parts.append(oh * lax.rsqrt(jnp.mean(oh * oh, axis=-1, keepdims=True) + EPS) * ng_ref[...])
    dn = jnp.concatenate(parts, axis=1) * sz_ref[...]
    heads = jnp.concatenate([dn, cf_ref[...]], axis=1).astype(BF16)
    x1 = x_ref[...] + gt1_ref[0] * jnp.dot(heads, wo_ref[...], preferred_element_type=F32)
    hn = (x1 * lax.rsqrt(jnp.mean(x1 * x1, axis=-1, keepdims=True) + EPS) * gffn_ref[...]
          * (1.0 + sc2_ref[0]) + sh2_ref[0])
    hn_o[...] = _pack_bf16_pairs(hn)
    lg_o[...] = lax.dot_general(rwt_ref[...], hn, (((1,), (1,)), ((), ())), precision=HIGHEST,
                                preferred_element_type=F32)
    gu = jnp.dot(hn.astype(BF16), wsgu_ref[...], preferred_element_type=F32)
    ds = gu.shape[1] // 2
    act = (_silu(gu[:, :ds]) * gu[:, ds:]).astype(BF16)
    base_o[...] = x1 + gt2_ref[0] * jnp.dot(act, wsd_ref[...], preferred_element_type=F32)


def _mixer_out(o_f, o_b, sz, cfo, x2, gt1, sh2, sc2, gt2, norm_g, w_out, g_ffn, rwt, wsgu, wsd, seq, tm):
    t, d = x2.shape
    tiles = seq // tm
    const = lambda i: (0, 0)
    row_map = lambda i: (i, 0)
    mod_map = lambda i: (i // tiles, 0, 0)
    half = pl.BlockSpec((tm, DN_WIDTH), row_map)
    mod_spec = pl.BlockSpec((1, 1, d), mod_map)
    return pl.pallas_call(
        _mixout_kernel,
        grid=(t // tm,),
        in_specs=[half, half, half, half, pl.BlockSpec((tm, d), row_map),
                  mod_spec, mod_spec, mod_spec, mod_spec,
                  pl.BlockSpec((1, DN_D), const), pl.BlockSpec(w_out.shape, const), pl.BlockSpec((1, d), const),
                  pl.BlockSpec(rwt.shape, const), pl.BlockSpec(wsgu.shape, const), pl.BlockSpec(wsd.shape, const)],
        out_specs=[pl.BlockSpec((tm, d), row_map), pl.BlockSpec((tm, d // 2), row_map),
                   pl.BlockSpec((N_EXPERTS, tm), lambda i: (0, i))],
        out_shape=[jax.ShapeDtypeStruct((t, d), F32), jax.ShapeDtypeStruct((t, d // 2), U32),
                   jax.ShapeDtypeStruct((N_EXPERTS, t), F32)],
        compiler_params=_cparams("parallel"),
        name="mixer_out",
    )(o_f, o_b, sz, cfo, x2, gt1, sh2, sc2, gt2, norm_g, w_out, g_ffn, rwt, wsgu, wsd)


def _first_argmax(vals, idx, sentinel):
    m = jnp.max(vals, axis=0, keepdims=True)
    return m, jnp.min(jnp.where(vals == m, idx, sentinel), axis=0, keepdims=True)


def _router_kernel(lg_ref, bias_ref, e_o, r_o, wt_o, cnt_o, carry_ref):
    @pl.when(pl.program_id(0) == 0)
    def _():
        carry_ref[...] = jnp.zeros(carry_ref.shape, F32)

    tt = lg_ref.shape[1]
    neg = -jnp.inf
    scores = jax.nn.sigmoid(lg_ref[...])
    sel = scores + bias_ref[...]
    sub = lax.broadcasted_iota(jnp.int32, (GROUP_SIZE, tt), 0)

    rows = []
    for g in range(N_GROUPS):
        sg = sel[g * GROUP_SIZE:(g + 1) * GROUP_SIZE]
        m1, first = _first_argmax(sg, sub, GROUP_SIZE)
        m2 = jnp.max(jnp.where(sub == first, neg, sg), axis=0, keepdims=True)
        rows.append(m1 + m2)
    cur = jnp.concatenate(rows, axis=0)
    gidx = lax.broadcasted_iota(jnp.int32, (N_GROUPS, tt), 0)
    keep = gidx < 0
    for _ in range(TOPK_GROUPS):
        _, a = _first_argmax(cur, gidx, N_GROUPS)
        pick = gidx == a
        keep = jnp.logical_or(keep, pick)
        cur = jnp.where(pick, neg, cur)
    keep_e = jnp.concatenate([jnp.broadcast_to(keep[g:g + 1], (GROUP_SIZE, tt)) for g in range(N_GROUPS)],
                             axis=0)
    masked = jnp.where(keep_e, sel, neg)

    eidx = lax.broadcasted_iota(jnp.int32, (N_EXPERTS, tt), 0)
    e_rows, w_rows, picks = [], [], []
    for _ in range(TOP_K):
        _, a = _first_argmax(masked, eidx, N_EXPERTS)
        pick = eidx == a
        e_rows.append(a)
        w_rows.append(jnp.sum(jnp.where(pick, scores, 0.0), axis=0, keepdims=True))
        picks.append(pick)
        masked = jnp.where(pick, neg, masked)
    onehot = sum(p.astype(F32) for p in picks)
    scale = ROUTED_SCALE / sum(w_rows)

    tri = (lax.broadcasted_iota(jnp.int32, (tt, tt), 0) < lax.broadcasted_iota(jnp.int32, (tt, tt), 1))
    cum = jnp.dot(onehot.astype(BF16), tri.astype(BF16), preferred_element_type=F32) + carry_ref[:, 0:1]
    r_rows = [jnp.sum(jnp.where(p, cum, 0.0), axis=0, keepdims=True).astype(jnp.int32) for p in picks]
    carry_ref[...] = carry_ref[...] + jnp.sum(onehot, axis=1, keepdims=True)
    cnt_o[...] = carry_ref[...]

    fill = 8 - TOP_K
    e_o[...] = jnp.concatenate(e_rows + [jnp.zeros((fill, tt), jnp.int32)], axis=0)
    r_o[...] = jnp.concatenate(r_rows + [jnp.zeros((fill, tt), jnp.int32)], axis=0)
    w_pad = jnp.concatenate([w * scale for w in w_rows] + [jnp.zeros((LANES - TOP_K, tt), F32)], axis=0)
    wt_o[...] = jnp.transpose(w_pad)


def _router(lg_t, router_bias, tt):
    e, t = lg_t.shape
    idx_spec = pl.BlockSpec((8, tt), lambda i: (0, i))
    return pl.pallas_call(
        _router_kernel,
        grid=(t // tt,),
        in_specs=[pl.BlockSpec((e, tt), lambda i: (0, i)), pl.BlockSpec((e, 1), lambda i: (0, 0))],
        out_specs=[idx_spec, idx_spec, pl.BlockSpec((tt, LANES), lambda i: (i, 0)),
                   pl.BlockSpec((e, LANES), lambda i: (0, 0))],
        out_shape=[jax.ShapeDtypeStruct((8, t), jnp.int32), jax.ShapeDtypeStruct((8, t), jnp.int32),
                   jax.ShapeDtypeStruct((t, LANES), F32), jax.ShapeDtypeStruct((e, LANES), F32)],
        scratch_shapes=[pltpu.VMEM((e, LANES), F32)],
        compiler_params=_cparams("arbitrary"),
        name="router",
    )(lg_t, router_bias.reshape(e, 1))


SLOT_STRIDE = 8
ROW_GROUP = 8


def _slot_kernel(pstart_ref, e_ref, r_ref, o_ref):
    e = e_ref[...]
    acc = r_ref[...]
    for x in range(N_EXPERTS):
        acc = acc + jnp.where(e == x, pstart_ref[x], 0)
    tt = acc.shape[1]
    pad = jnp.concatenate([acc, jnp.zeros((LANES - acc.shape[0], tt), jnp.int32)], axis=0)
    o_ref[...] = jnp.transpose(pad)[:, :SLOT_STRIDE]


def _slot_index(pstart, e_idx, rank, tt):
    rows, t = e_idx.shape
    spec = pl.BlockSpec((rows, tt), lambda i, ps: (0, i))
    slot = pl.pallas_call(
        _slot_kernel,
        grid_spec=pltpu.PrefetchScalarGridSpec(
            num_scalar_prefetch=1, grid=(t // tt,), in_specs=[spec, spec],
            out_specs=pl.BlockSpec((tt, SLOT_STRIDE), lambda i, ps: (i, 0))),
        out_shape=jax.ShapeDtypeStruct((t, SLOT_STRIDE), jnp.int32),
        compiler_params=_cparams("parallel"),
        name="slot_index",
    )(pstart, e_idx, rank)
    return slot.reshape(t * SLOT_STRIDE)


def _row_copies(n_tokens, make_copy):
    def body(g, carry):
        base = g * (ROW_GROUP * SLOT_STRIDE)
        for s in range(ROW_GROUP):
            for j in range(TOP_K):
                make_copy(g, s, j, base + (s * SLOT_STRIDE + j)).start(priority=(s * TOP_K + j) % 2)
        return carry

    lax.fori_loop(0, n_tokens // ROW_GROUP, body, 0)


def _dispatch_kernel(pstart_ref, count_ref, slot_ref, hn_ref, xs_hbm, zero_ref, sem):
    td = hn_ref.shape[0] * ROW_GROUP
    bm = zero_ref.shape[0]

    def zero_fill(e, wait):
        end = pstart_ref[e] + count_ref[e]
        aligned = pl.multiple_of((end + 7) // 8 * 8, 8)
        for i in range(7):
            @pl.when(end + i < aligned)
            def _():
                cp = pltpu.make_async_copy(zero_ref.at[pl.ds(0, 1)], xs_hbm.at[pl.ds(end + i, 1)], sem)
                cp.wait() if wait else cp.start()
        cp = pltpu.make_async_copy(zero_ref, xs_hbm.at[pl.ds(aligned, bm)], sem)
        cp.wait() if wait else cp.start()

    @pl.when(pl.program_id(0) == 0)
    def _():
        zero_ref[...] = jnp.zeros(zero_ref.shape, zero_ref.dtype)

        def start(e, carry):
            zero_fill(e, False)
            return carry

        def wait(e, carry):
            zero_fill(e, True)
            return carry

        lax.fori_loop(0, N_EXPERTS, start, 0)
        lax.fori_loop(0, N_EXPERTS, wait, 0)

        last = N_EXPERTS - 1
        tail = (pstart_ref[last] + count_ref[last] + bm - 1) // bm

        def tail_copy(b):
            return pltpu.make_async_copy(zero_ref, xs_hbm.at[pl.ds(pl.multiple_of(b * bm, bm), bm)], sem)

        def tail_start(b, carry):
            tail_copy(b).start()
            return carry

        def tail_wait(b, carry):
            tail_copy(b).wait()
            return carry

        lax.fori_loop(tail, xs_hbm.shape[0] // bm, tail_start, 0)
        lax.fori_loop(tail, xs_hbm.shape[0] // bm, tail_wait, 0)

    _row_copies(td, lambda g, s, j, i: pltpu.make_async_copy(
        hn_ref.at[g, pl.ds(s, 1)], xs_hbm.at[pl.ds(slot_ref[i], 1)], sem))
    pltpu.make_async_copy(xs_hbm.at[pl.ds(0, td * TOP_K)], xs_hbm.at[pl.ds(0, td * TOP_K)], sem).wait()


def _dispatch(pstart, counts, slot, hn, n_slots, td, bm):
    t, d = hn.shape
    return pl.pallas_call(
        _dispatch_kernel,
        grid_spec=pltpu.PrefetchScalarGridSpec(
            num_scalar_prefetch=2, grid=(t // td,),
            in_specs=[pl.BlockSpec((td * SLOT_STRIDE,), lambda i, ps, cn: (i,), memory_space=pltpu.SMEM),
                      pl.BlockSpec((td // ROW_GROUP, ROW_GROUP, d), lambda i, ps, cn: (i, 0, 0))],
            out_specs=pl.BlockSpec(memory_space=pl.ANY),
            scratch_shapes=[pltpu.VMEM((bm, d), hn.dtype), pltpu.SemaphoreType.DMA(())]),
        out_shape=jax.ShapeDtypeStruct((n_slots, d), hn.dtype),
        compiler_params=_cparams("arbitrary"),
        name="dispatch",
    )(pstart, counts, slot, hn.reshape(t // ROW_GROUP, ROW_GROUP, d))


EXPERT_SUB = 256


def _expert_kernel(bexp_ref, bsrc_ref, nused_ref, xs_ref, wg_ref, wu_ref, wd_ref, y_ref):
    del bexp_ref, bsrc_ref
    i = pl.program_id(0)

    @pl.when(i < nused_ref[0])
    def _():
        wg = wg_ref[0].astype(BF16)
        wu = wu_ref[0].astype(BF16)
        wd = wd_ref[0].astype(BF16)
        for r0 in range(0, xs_ref.shape[0], EXPERT_SUB):
            rows = slice(r0, r0 + EXPERT_SUB)
            xb = jnp.concatenate(_unpack_bf16_pairs(xs_ref[rows, :]), axis=1).astype(BF16)
            g = jnp.dot(xb, wg, preferred_element_type=F32)
            u = jnp.dot(xb, wu, preferred_element_type=F32)
            act = (_silu(g) * u).astype(BF16)
            y_ref[rows, :] = _pack_bf16_pairs(jnp.dot(act, wd, preferred_element_type=F32))

    @pl.when(i >= nused_ref[0])
    def _():
        y_ref[...] = jnp.zeros(y_ref.shape, y_ref.dtype)


def _experts(block_exp, block_src, n_used, xs, w_gate, w_up, w_down, bm):
    n_slots, d = xs.shape
    w_map = lambda i, be, bs, nu: (be[i], 0, 0)
    return pl.pallas_call(
        _expert_kernel,
        grid_spec=pltpu.PrefetchScalarGridSpec(
            num_scalar_prefetch=3, grid=(n_slots // bm,),
            in_specs=[pl.BlockSpec((bm, d), lambda i, be, bs, nu: (bs[i], 0)),
                      pl.BlockSpec((1,) + w_gate.shape[1:], w_map),
                      pl.BlockSpec((1,) + w_up.shape[1:], w_map),
                      pl.BlockSpec((1,) + w_down.shape[1:], w_map)],
            out_specs=pl.BlockSpec((bm, d), lambda i, be, bs, nu: (i, 0))),
        out_shape=jax.ShapeDtypeStruct((n_slots, d), xs.dtype),
        compiler_params=_cparams("arbitrary"),
        name="experts",
    )(block_exp, block_src, n_used, xs, w_gate, w_up, w_down)


def _combine_kernel(slot_ref, wt_ref, base_ref, gt2_ref, gfin_ref, y_hbm, o_ref, buf, sem):
    tg = base_ref.shape[0]
    _row_copies(tg, lambda g, s, j, i: pltpu.make_async_copy(
        y_hbm.at[pl.ds(slot_ref[i], 1)], buf.at[j, g, pl.ds(s, 1)], sem))
    pltpu.make_async_copy(buf, buf, sem).wait()
    wt = wt_ref[...]
    lo = hi = None
    for j in range(TOP_K):
        ylo, yhi = _unpack_bf16_pairs(buf[j].reshape(tg, buf.shape[-1]))
        w = wt[:, j:j + 1]
        lo = w * ylo if lo is None else lo + w * ylo
        hi = w * yhi if hi is None else hi + w * yhi
    xf = base_ref[...] + gt2_ref[0] * jnp.concatenate([lo, hi], axis=1)
    o_ref[...] = xf * lax.rsqrt(jnp.mean(xf * xf, axis=-1, keepdims=True) + EPS) * gfin_ref[...]


def _combine(slot, wt, base, gt2, g_final, y, seq, tg):
    t, d = base.shape
    tiles = seq // tg
    return pl.pallas_call(
        _combine_kernel,
        grid=(t // tg,),
        in_specs=[pl.BlockSpec((tg * SLOT_STRIDE,), lambda i: (i,), memory_space=pltpu.SMEM),
                  pl.BlockSpec((tg, LANES), lambda i: (i, 0)),
                  pl.BlockSpec((tg, d), lambda i: (i, 0)),
                  pl.BlockSpec((1, 1, d), lambda i: (i // tiles, 0, 0)),
                  pl.BlockSpec((1, d), lambda i: (0, 0)),
                  pl.BlockSpec(memory_space=pl.ANY)],
        out_specs=pl.BlockSpec((tg, d), lambda i: (i, 0)),
        out_shape=jax.ShapeDtypeStruct((t, d), F32),
        scratch_shapes=[pltpu.VMEM((TOP_K, tg // ROW_GROUP, ROW_GROUP, y.shape[1]), y.dtype),
                        pltpu.SemaphoreType.DMA(())],
        compiler_params=_cparams("arbitrary"),
        name="combine",
    )(slot, wt, base, gt2, g_final.reshape(1, d), y)


EXPERT_BLOCK = 512
ROW_TILE = 512
ROUTER_TILE = 512
GATHER_TILE = 256
SLOT_TILE = 4096


def _moe_plan(counts, n_tokens, bm):
    padded = (counts + bm - 1) // bm * bm
    pad_end = jnp.cumsum(padded)
    pad_start = (pad_end - padded).astype(jnp.int32)
    n_blocks = -(-(n_tokens * TOP_K) // bm) + N_EXPERTS + 1
    n_used = (pad_end[-1] // bm).astype(jnp.int32)
    block_src = jnp.minimum(jnp.arange(n_blocks, dtype=jnp.int32), jnp.maximum(n_used - 1, 0))
    block_exp = jnp.sum((pad_end[None, :] <= (block_src * bm)[:, None]).astype(jnp.int32), axis=1)
    block_exp = jnp.minimum(block_exp, N_EXPERTS - 1)
    return pad_start, block_exp, block_src, n_used.reshape(1), n_blocks * bm


def kernel(x, c, ctx, c_ctx, w_mod, b_mod, g_mix, g_ffn, w_in, w_out, dn_conv_w, dn_a_log, dn_dt_bias,
           dn_norm_g, cf_dw_w, cf_dw_b, cf_ln_g, cf_ln_b, router_w, router_bias, exp_w_gate, exp_w_up,
           exp_w_down, sh_w_gate, sh_w_up, sh_w_down, g_final):
    assert w_mod.shape[0] == 1, "single-layer block: the context stream is never re-read"
    bsz, seq, d = x.shape
    t = bsz * seq
    x2 = x.reshape(t, d)
    st = _mixer_front(x, c, ctx, c_ctx, w_mod[0], b_mod[0], g_mix[0], w_in[0], dn_conv_w[0], dn_a_log[0],
                      dn_dt_bias[0])
    _, _, gt1, sh2, sc2, gt2 = st['lat']
    cfo = _cf_conv(st['y'], cf_dw_w[0], cf_dw_b[0], cf_ln_g[0], cf_ln_b[0], bsz)
    wsgu = jnp.concatenate([sh_w_gate[0], sh_w_up[0]], axis=1).astype(BF16)
    base, hn, lg_t = _mixer_out(st['o_f'], st['o_b'], st['sz'], cfo, x2, gt1, sh2, sc2, gt2,
                                dn_norm_g[0].reshape(1, DN_D), w_out[0].astype(BF16), g_ffn[0].reshape(1, d),
                                router_w[0].T, wsgu, sh_w_down[0].astype(BF16), seq, min(seq, ROW_TILE))
    e_idx, rank, wt, cnt = _router(lg_t, router_bias[0], min(t, ROUTER_TILE))
    counts = cnt[:, 0].astype(jnp.int32)
    pstart, block_exp, block_src, n_used, n_slots = _moe_plan(counts, t, EXPERT_BLOCK)
    slot = _slot_index(pstart, e_idx, rank, min(t, SLOT_TILE))
    xs = _dispatch(pstart, counts, slot, hn, n_slots, min(t, GATHER_TILE), EXPERT_BLOCK)
    y = _experts(block_exp, block_src, n_used, xs, exp_w_gate[0], exp_w_up[0], exp_w_down[0], EXPERT_BLOCK)
    out = _combine(slot, wt, base, gt2, g_final, y, seq, min(seq, GATHER_TILE))
    return out.reshape(bsz, seq, d)
```

```python
import functools

import jax
import jax.numpy as jnp
from jax import lax
from jax.experimental import pallas as pl
from jax.experimental.pallas import tpu as pltpu

F32 = jnp.float32
U32 = jnp.uint32
BF16 = jnp.bfloat16
HIGHEST = lax.Precision.HIGHEST

EPS = 1e-6
LANES = 128
GRID_W = 64
DN_HEADS = 4
DN_D = 128
DN_QK = DN_HEADS * DN_D
DN_WIDTH = DN_HEADS * DN_D
DN_CONV_DIM = 2 * DN_QK + DN_WIDTH
SHORT_CONV = 7
CHUNK = 64
CF_CH = 512
CF_K = 31
N_EXPERTS = 64
TOP_K = 6
N_GROUPS = 8
GROUP_SIZE = N_EXPERTS // N_GROUPS
TOPK_GROUPS = 4
ROUTED_SCALE = 2.5
HALO = 8
VMEM_LIMIT = 56 * 1024 * 1024


def _cparams(*sem):
    return pltpu.CompilerParams(dimension_semantics=sem, vmem_limit_bytes=VMEM_LIMIT)


def _silu(v):
    return v * jax.nn.sigmoid(v)


def _bdot(a, b):
    return jnp.dot(a.astype(BF16), b.astype(BF16), preferred_element_type=F32)


def _pack_bf16_pairs(v):
    n = v.shape[1] // 2
    lo = lax.bitcast_convert_type(v[:, :n].astype(BF16).astype(F32), U32)
    hi = lax.bitcast_convert_type(v[:, n:].astype(BF16).astype(F32), U32)
    return (lo >> 16) | (hi & jnp.uint32(0xFFFF0000))


def _unpack_bf16_pairs(w):
    lo = lax.bitcast_convert_type(w << 16, F32)
    hi = lax.bitcast_convert_type(w & jnp.uint32(0xFFFF0000), F32)
    return lo, hi


def _mod_kernel(c_ref, w_ref, b_ref, o_ref):
    o_ref[...] = jnp.dot(_silu(c_ref[...]), w_ref[...], preferred_element_type=F32,
                         precision=HIGHEST) + b_ref[...]


def _modulation(cc, w_mod, b_mod):
    rows, d = cc.shape
    n = w_mod.shape[1]
    tn = 1024
    return pl.pallas_call(
        _mod_kernel,
        grid=(n // tn,),
        in_specs=[pl.BlockSpec((rows, d), lambda j: (0, 0)),
                  pl.BlockSpec((d, tn), lambda j: (0, j)),
                  pl.BlockSpec((1, tn), lambda j: (0, j))],
        out_specs=pl.BlockSpec((rows, tn), lambda j: (0, j)),
        out_shape=jax.ShapeDtypeStruct((rows, n), F32),
        compiler_params=_cparams("parallel"),
        name="modulation",
    )(cc, w_mod, b_mod.reshape(1, n))


def _chunk_cumsum(g, reverse):
    n = g.shape[0]
    pos = lax.broadcasted_iota(jnp.int32, g.shape, 0) % CHUNK
    s = 1
    while s < CHUNK:
        if reverse:
            shifted = pltpu.roll(g, n - s, 0)
            ok = pos < CHUNK - s
        else:
            shifted = pltpu.roll(g, s, 0)
            ok = pos >= s
        g = g + jnp.where(ok, shifted, 0.0)
        s *= 2
    return g


def _inproj_kernel(latent, x_ref, sh_ref, sc_ref, g_ref, wqkv_ref, wba_ref, alog_ref, dtb_ref, *rest):
    if latent:
        wz_ref, wcf_ref, qkv_o, bgc_o, bgct_o, sz_o, y_o = rest
    else:
        qkv_o, bgc_o, bgct_o = rest
    x = x_ref[...]
    xn = x * lax.rsqrt(jnp.mean(x * x, axis=-1, keepdims=True) + EPS) * g_ref[...]
    hb = (xn * (1.0 + sc_ref[0]) + sh_ref[0]).astype(BF16)
    qkv_o[...] = jnp.dot(hb, wqkv_ref[...], preferred_element_type=F32)

    ba = jnp.dot(hb, wba_ref[...], preferred_element_type=F32)
    col = lax.broadcasted_iota(jnp.int32, ba.shape, 1)
    is_beta = (col % 8) < DN_HEADS
    g = -jnp.exp(alog_ref[...]) * jax.nn.softplus(ba + dtb_ref[...])
    g = jnp.where(is_beta, 0.0, g)
    gc = jnp.where(col < 8, _chunk_cumsum(g, False), _chunk_cumsum(g, True))
    bgc = jnp.where(is_beta, jax.nn.sigmoid(ba), gc)
    bgc_o[...] = bgc
    for c in range(bgc.shape[0] // CHUNK):
        bgct_o[c] = jnp.transpose(bgc[c * CHUNK:(c + 1) * CHUNK, :])[:16, :]

    if latent:
        sz_o[...] = _silu(jnp.dot(hb, wz_ref[...], preferred_element_type=F32))
        cf = jnp.dot(hb, wcf_ref[...], preferred_element_type=F32)
        y_o[...] = cf[:, :CF_CH] * jax.nn.sigmoid(cf[:, CF_CH:])


def _in_projection(x2, sh, sc, g_mix, wqkv, wba, alog_row, dtb_row, wz, wcf, rows_per_mod, tm):
    t, d = x2.shape
    latent = wz is not None
    tiles_per_mod = rows_per_mod // tm
    const = lambda i: (0, 0)
    mod_map = lambda i: (i // tiles_per_mod, 0, 0)
    row_map = lambda i: (i, 0)
    in_specs = [pl.BlockSpec((tm, d), row_map),
                pl.BlockSpec((1, 1, d), mod_map), pl.BlockSpec((1, 1, d), mod_map),
                pl.BlockSpec((1, d), const),
                pl.BlockSpec(wqkv.shape, const), pl.BlockSpec(wba.shape, const),
                pl.BlockSpec((1, LANES), const), pl.BlockSpec((1, LANES), const)]
    args = [x2, sh, sc, g_mix, wqkv, wba, alog_row, dtb_row]
    out_specs = [pl.BlockSpec((tm, DN_CONV_DIM), row_map), pl.BlockSpec((tm, LANES), row_map),
                 pl.BlockSpec((tm // CHUNK, 16, CHUNK), lambda i: (i, 0, 0))]
    out_shape = [jax.ShapeDtypeStruct((t, DN_CONV_DIM), F32), jax.ShapeDtypeStruct((t, LANES), F32),
                 jax.ShapeDtypeStruct((t // CHUNK, 16, CHUNK), F32)]
    if latent:
        in_specs += [pl.BlockSpec(wz.shape, const), pl.BlockSpec(wcf.shape, const)]
        args += [wz, wcf]
        out_specs += [pl.BlockSpec((tm, DN_WIDTH), row_map), pl.BlockSpec((tm, CF_CH), row_map)]
        out_shape += [jax.ShapeDtypeStruct((t, DN_WIDTH), F32), jax.ShapeDtypeStruct((t, CF_CH), F32)]
    return pl.pallas_call(
        functools.partial(_inproj_kernel, latent),
        grid=(t // tm,),
        in_specs=in_specs, out_specs=out_specs, out_shape=out_shape,
        compiler_params=_cparams("parallel"),
        name="in_projection_latent" if latent else "in_projection_context",
    )(*args)


CONV_ROWS = 64


def _shortconv_kernel(tiles_per_seq, prev_ref, cur_ref, next_ref, w_ref, o_ref, ext_ref):
    i = pl.program_id(0)
    tm = cur_ref.shape[0]
    pos = i % tiles_per_seq
    ext_ref[0:HALO, :] = jnp.where(pos == 0, 0.0, prev_ref[...])
    ext_ref[HALO:HALO + tm, :] = cur_ref[...]
    ext_ref[HALO + tm:, :] = jnp.where(pos == tiles_per_seq - 1, 0.0, next_ref[...])
    reach = SHORT_CONV // 2

    for r0 in range(0, tm, CONV_ROWS):
        for cb in range(DN_CONV_DIM // LANES):
            cols = slice(cb * LANES, (cb + 1) * LANES)
            acc = jnp.zeros((CONV_ROWS, LANES), F32)
            for k in range(SHORT_CONV):
                s = r0 + HALO - reach + k
                acc = acc + w_ref[k:k + 1, cols] * ext_ref[s:s + CONV_ROWS, cols]
            a = _silu(acc)
            if cb < 2 * DN_HEADS:
                a = a * lax.rsqrt(jnp.sum(a * a, axis=-1, keepdims=True) + EPS)
            if cb < DN_HEADS:
                a = a * (DN_D ** -0.5)
            o_ref[r0:r0 + CONV_ROWS, cols] = a


def _short_conv(qkv, conv_w, seq_len, tm):
    t, c = qkv.shape
    tiles_per_seq = seq_len // tm
    hb = tm // HALO
    n_halo_blocks = t // HALO
    return pl.pallas_call(
        functools.partial(_shortconv_kernel, tiles_per_seq),
        grid=(t // tm,),
        in_specs=[pl.BlockSpec((HALO, c), lambda i: (jnp.maximum(i * hb - 1, 0), 0)),
                  pl.BlockSpec((tm, c), lambda i: (i, 0)),
                  pl.BlockSpec((HALO, c), lambda i: (jnp.minimum((i + 1) * hb, n_halo_blocks - 1), 0)),
                  pl.BlockSpec(conv_w.shape, lambda i: (0, 0))],
        out_specs=pl.BlockSpec((tm, c), lambda i: (i, 0)),
        out_shape=jax.ShapeDtypeStruct((t, c), F32),
        scratch_shapes=[pltpu.VMEM((tm + 2 * HALO, c), F32)],
        compiler_params=_cparams("parallel"),
        name="short_conv",
    )(qkv, qkv, qkv, conv_w)


DELTA_STEPS = 2


def _delta_pre(refs, ci_f, ci_b, masks, eye):
    qkv_ref, bgc_ref, bgct_ref = refs
    chains = range(2 * DN_HEADS * len(ci_f))
    q, k, v, beta, gcol, decay, eg, gl = [], [], [], [], [], [], [], []
    for d, ci in [(d, c[step]) for step in range(len(ci_f)) for d, c in enumerate((ci_f, ci_b))]:
        rows = pl.ds(pl.multiple_of(ci * CHUNK, CHUNK), CHUNK)
        bg = bgc_ref[rows, :]
        bgt = bgct_ref[ci]
        for h in range(DN_HEADS):
            q.append(qkv_ref[rows, h * DN_D:(h + 1) * DN_D])
            k.append(qkv_ref[rows, DN_QK + h * DN_D:DN_QK + (h + 1) * DN_D])
            v.append(qkv_ref[rows, 2 * DN_QK + h * DN_D:2 * DN_QK + (h + 1) * DN_D])
            jb = d * 8 + h
            jg = d * 8 + DN_HEADS + h
            beta.append(bg[:, jb:jb + 1])
            gc = bg[:, jg:jg + 1]
            gcol.append(gc)
            decay.append(jnp.exp(jnp.where(masks[d][0], gc - bgt[jg:jg + 1, :], -jnp.inf)))
            eg.append(jnp.exp(gc))
            gl.append(gc[CHUNK - 1:CHUNK] if d == 0 else gc[0:1])

    contract_last = (((1,), (1,)), ((), ()))
    gram = [lax.dot_general(jnp.concatenate([k[n], q[n]], axis=0).astype(BF16), k[n].astype(BF16),
                            contract_last, preferred_element_type=F32) for n in chains]
    lmat = [jnp.where(masks[n // DN_HEADS % 2][1], gram[n][:CHUNK] * decay[n], 0.0) * beta[n] for n in chains]
    tinv = [eye - lmat[n] for n in chains]
    p = [_bdot(lmat[n], lmat[n]) for n in chains]
    n_sq = 2
    while n_sq < CHUNK // 2:
        x = [_bdot(jnp.concatenate([tinv[n], p[n]], axis=0), p[n]) for n in chains]
        tinv = [tinv[n] + x[n][:CHUNK] for n in chains]
        p = [x[n][CHUNK:] for n in chains]
        n_sq *= 2
    tinv = [tinv[n] + _bdot(tinv[n], p[n]) for n in chains]
    uw = [_bdot(tinv[n], jnp.concatenate([v[n] * beta[n], k[n] * (beta[n] * eg[n])], axis=1)) for n in chains]

    return [(uw[n][:, :DN_D], uw[n][:, DN_D:], q[n] * eg[n], k[n] * jnp.exp(gl[n] - gcol[n]),
             gram[n][CHUNK:] * decay[n], jnp.exp(gl[n])) for n in chains]


def _delta_rec(pre, s_ref, want_out):
    chains = range(len(pre))
    contract_first = (((0,), (0,)), ((), ()))
    s = [s_ref[n] for n in chains]
    if want_out:
        wq = [_bdot(jnp.concatenate([pre[n][1], pre[n][2]], axis=0), s[n]) for n in chains]
        ws = [wq[n][:CHUNK] for n in chains]
    else:
        ws = [_bdot(pre[n][1], s[n]) for n in chains]
    vb = [(pre[n][0] - ws[n]).astype(BF16) for n in chains]
    for n in chains:
        s_ref[n] = s[n] * pre[n][5] + lax.dot_general(pre[n][3].astype(BF16), vb[n], contract_first,
                                                      preferred_element_type=F32)
    if not want_out:
        return None
    return [wq[n][CHUNK:] + jnp.dot(pre[n][4].astype(BF16), vb[n], preferred_element_type=F32)
            for n in chains]


def _delta_kernel(qc_ref, bc_ref, btc_ref, qx_ref, bx_ref, btx_ref, of_ref, ob_ref, s_ref, p_ref):
    s_ref[...] = jnp.zeros(s_ref.shape, F32)
    row = lax.broadcasted_iota(jnp.int32, (CHUNK, CHUNK), 0)
    col = lax.broadcasted_iota(jnp.int32, (CHUNK, CHUNK), 1)
    masks = ((row >= col, row > col), (row <= col, row < col))
    eye = (row == col).astype(F32)
    per_step = 2 * DN_HEADS
    n_chains = DELTA_STEPS * per_step

    def stage(pre):
        for n in range(n_chains):
            u, w, qe, kd, amat, egl = pre[n]
            for j, val in enumerate((u, w, qe, kd)):
                p_ref[n, j] = val
            p_ref[n, 4] = jnp.concatenate([amat, jnp.broadcast_to(egl, (CHUNK, DN_D - CHUNK))], axis=1)

    def staged():
        return [(p_ref[n, 0], p_ref[n, 1], p_ref[n, 2], p_ref[n, 3], p_ref[n, 4][:, :CHUNK],
                 p_ref[n, 4][0:1, CHUNK:CHUNK + 1]) for n in range(n_chains)]

    def scan(refs, want_out):
        n_chunks = refs[0].shape[0] // CHUNK
        assert n_chunks % DELTA_STEPS == 0

        def pre(first):
            fwd = [first + s for s in range(DELTA_STEPS)]
            return _delta_pre(refs, fwd, [n_chunks - 1 - c for c in fwd], masks, eye)

        stage(pre(0))

        def body(it, carry):
            i = it * DELTA_STEPS
            cur = staged()
            new = pre(jnp.minimum(i + DELTA_STEPS, n_chunks - DELTA_STEPS))
            for s in range(DELTA_STEPS):
                o = _delta_rec(cur[s * per_step:(s + 1) * per_step], s_ref, want_out)
                if want_out:
                    rf = pl.ds(pl.multiple_of((i + s) * CHUNK, CHUNK), CHUNK)
                    rb = pl.ds(pl.multiple_of((n_chunks - 1 - i - s) * CHUNK, CHUNK), CHUNK)
                    for h in range(DN_HEADS):
                        cols = slice(h * DN_D, (h + 1) * DN_D)
                        of_ref[rf, cols] = o[h]
                        ob_ref[rb, cols] = o[DN_HEADS + h]
            stage(new)
            return carry

        lax.fori_loop(0, n_chunks // DELTA_STEPS, body, 0)

    scan((qc_ref, bc_ref, btc_ref), False)
    scan((qx_ref, bx_ref, btx_ref), True)


def _delta_scan(qkv_c, bgc_c, bgct_c, qkv_x, bgc_x, bgct_x, bsz):
    lc = qkv_c.shape[0] // bsz
    lx = qkv_x.shape[0] // bsz
    row_map = lambda b: (b, 0)
    out = jax.ShapeDtypeStruct((bsz * lx, DN_WIDTH), F32)
    return pl.pallas_call(
        _delta_kernel,
        grid=(bsz,),
        in_specs=[pl.BlockSpec((lc, DN_CONV_DIM), row_map), pl.BlockSpec((lc, LANES), row_map),
                  pl.BlockSpec((lc // CHUNK, 16, CHUNK), lambda b: (b, 0, 0)),
                  pl.BlockSpec((lx, DN_CONV_DIM), row_map), pl.BlockSpec((lx, LANES), row_map),
                  pl.BlockSpec((lx // CHUNK, 16, CHUNK), lambda b: (b, 0, 0))],
        out_specs=[pl.BlockSpec((lx, DN_WIDTH), row_map), pl.BlockSpec((lx, DN_WIDTH), row_map)],
        out_shape=[out, out],
        scratch_shapes=[pltpu.VMEM((2 * DN_HEADS, DN_D, DN_D), F32),
                        pltpu.VMEM((DELTA_STEPS * 2 * DN_HEADS, 5, CHUNK, DN_D), F32)],
        compiler_params=_cparams("parallel"),
        name="delta_scan",
    )(qkv_c, bgc_c, bgct_c, qkv_x, bgc_x, bgct_x)


def _pad_rows(a, rows):
    return jnp.pad(a, ((0, rows - a.shape[0]), (0, 0)))


def _mixer_front(x, c, ctx, c_ctx, w_mod, b_mod, g_mix, w_in, conv_w, a_log, dt_bias):
    bsz, seq, d = x.shape
    lc = ctx.shape[1]
    off_z = DN_CONV_DIM
    off_ba = off_z + DN_WIDTH
    off_cf = off_ba + 4 * DN_HEADS

    cc = _pad_rows(jnp.concatenate([c, c_ctx[None, :]], axis=0), -(-(bsz + 1) // 8) * 8)
    mod = _modulation(cc, w_mod, b_mod)
    mods = [mod[:, j * d:(j + 1) * d] for j in range(6)]
    lat = [m[:bsz].reshape(bsz, 1, d) for m in mods]
    con = [m[bsz:bsz + 1].reshape(1, 1, d) for m in mods]

    wqkv = w_in[:, :off_z].astype(BF16)
    wz = w_in[:, off_z:off_ba].astype(BF16)
    wba = jnp.pad(w_in[:, off_ba:off_cf], ((0, 0), (0, LANES - 4 * DN_HEADS))).astype(BF16)
    wcf = w_in[:, off_cf:].astype(BF16)
    zeros4 = jnp.zeros((2, DN_HEADS), F32)
    alog_row = jnp.pad(jnp.concatenate([zeros4, a_log], axis=1).reshape(1, -1), ((0, 0), (0, LANES - 16)))
    dtb_row = jnp.pad(jnp.concatenate([zeros4, dt_bias], axis=1).reshape(1, -1), ((0, 0), (0, LANES - 16)))
    g_row = g_mix.reshape(1, d)

    qkv_c, bgc_c, bgct_c = _in_projection(ctx.reshape(bsz * lc, d), con[0], con[1], g_row, wqkv, wba,
                                          alog_row, dtb_row, None, None, bsz * lc, min(lc, 512))
    qkv_x, bgc_x, bgct_x, sz, y = _in_projection(x.reshape(bsz * seq, d), lat[0], lat[1], g_row, wqkv, wba,
                                                 alog_row, dtb_row, wz, wcf, seq, min(seq, 512))
    qkvn_c = _short_conv(qkv_c, conv_w, lc, min(lc, 512))
    qkvn_x = _short_conv(qkv_x, conv_w, seq, min(seq, 512))
    o_f, o_b = _delta_scan(qkvn_c, bgc_c, bgct_c, qkvn_x, bgc_x, bgct_x, bsz)
    return dict(mod=mod, lat=lat, qkv_x=qkv_x, qkv_c=qkv_c, sz=sz, y=y, qkvn_x=qkvn_x, bgc_x=bgc_x,
                bgct_x=bgct_x, o_f=o_f, o_b=o_b)


CF_ROWS = 32


def _cfconv_kernel(y_ref, w_ref, b_ref, lng_ref, lnb_ref, o_ref, pad_ref):
    n = y_ref.shape[0]
    reach = (CF_K // 2) * GRID_W
    pad_ref[0:reach, :] = jnp.zeros((reach, CF_CH), F32)
    pad_ref[reach:reach + n, :] = y_ref[...]
    pad_ref[reach + n:, :] = jnp.zeros((reach, CF_CH), F32)

    def body(r, carry):
        r0 = pl.multiple_of(r * CF_ROWS, CF_ROWS)
        acc = jnp.zeros((CF_ROWS, CF_CH), F32)
        for k in range(CF_K):
            acc = acc + w_ref[k:k + 1, :] * pad_ref[pl.ds(r0 + k * GRID_W, CF_ROWS), :]
        acc = acc + b_ref[...]
        mu = jnp.mean(acc, axis=-1, keepdims=True)
        xc = acc - mu
        var = jnp.mean(xc * xc, axis=-1, keepdims=True)
        o_ref[pl.ds(r0, CF_ROWS), :] = _silu(xc * lax.rsqrt(var + EPS) * lng_ref[...] + lnb_ref[...])
        return carry

    lax.fori_loop(0, n // CF_ROWS, body, 0, unroll=4)


def _cf_conv(y, dw_w, dw_b, ln_g, ln_b, bsz):
    t, ch = y.shape
    n = t // bsz
    reach = (CF_K // 2) * GRID_W
    const = lambda b: (0, 0)
    return pl.pallas_call(
        _cfconv_kernel,
        grid=(bsz,),
        in_specs=[pl.BlockSpec((n, ch), lambda b: (b, 0)), pl.BlockSpec(dw_w.shape, const),
                  pl.BlockSpec((1, ch), const), pl.BlockSpec((1, ch), const), pl.BlockSpec((1, ch), const)],
        out_specs=pl.BlockSpec((n, ch), lambda b: (b, 0)),
        out_shape=jax.ShapeDtypeStruct((t, ch), F32),
        scratch_shapes=[pltpu.VMEM((n + 2 * reach, ch), F32)],
        compiler_params=_cparams("parallel"),
        name="conformer_conv",
    )(y, dw_w, dw_b.reshape(1, ch), ln_g.reshape(1, ch), ln_b.reshape(1, ch))


def _mixout_kernel(of_ref, ob_ref, sz_ref, cf_ref, x_ref, gt1_ref, sh2_ref, sc2_ref, gt2_ref, ng_ref, wo_ref,
                   gffn_ref, rwt_ref, wsgu_ref, wsd_ref, base_o, hn_o, lg_o):
    o = of_ref[...] + ob_ref[...]
    parts = []
    for h in range(DN_HEADS):
        oh = o[:, h * DN_D:(h + 1) * DN_D]
        parts.append(oh * lax.rsqrt(jnp.mean(oh * oh, axis=-1, keepdims=True) + EPS) * ng_ref[...])
    dn = jnp.concatenate(parts, axis=1) * sz_ref[...]
    heads = jnp.concatenate([dn, cf_ref[...]], axis=1).astype(BF16)
    x1 = x_ref[...] + gt1_ref[0] * jnp.dot(heads, wo_ref[...], preferred_element_type=F32)
    hn = (x1 * lax.rsqrt(jnp.mean(x1 * x1, axis=-1, keepdims=True) + EPS) * gffn_ref[...]
          * (1.0 + sc2_ref[0]) + sh2_ref[0])
    hn_o[...] = _pack_bf16_pairs(hn)
    lg_o[...] = lax.dot_general(rwt_ref[...], hn, (((1,), (1,)), ((), ())), precision=HIGHEST,
                                preferred_element_type=F32)
    gu = jnp.dot(hn.astype(BF16), wsgu_ref[...], preferred_element_type=F32)
    ds = gu.shape[1] // 2
    act = (_silu(gu[:, :ds]) * gu[:, ds:]).astype(BF16)
    base_o[...] = x1 + gt2_ref[0] * jnp.dot(act, wsd_ref[...], preferred_element_type=F32)


def _mixer_out(o_f, o_b, sz, cfo, x2, gt1, sh2, sc2, gt2, norm_g, w_out, g_ffn, rwt, wsgu, wsd, seq, tm):
    t, d = x2.shape
    tiles = seq // tm
    const = lambda i: (0, 0)
    row_map = lambda i: (i, 0)
    mod_map = lambda i: (i // tiles, 0, 0)
    half = pl.BlockSpec((tm, DN_WIDTH), row_map)
    mod_spec = pl.BlockSpec((1, 1, d), mod_map)
    return pl.pallas_call(
        _mixout_kernel,
        grid=(t // tm,),
        in_specs=[half, half, half, half, pl.BlockSpec((tm, d), row_map),
                  mod_spec, mod_spec, mod_spec, mod_spec,
                  pl.BlockSpec((1, DN_D), const), pl.BlockSpec(w_out.shape, const), pl.BlockSpec((1, d), const),
                  pl.BlockSpec(rwt.shape, const), pl.BlockSpec(wsgu.shape, const), pl.BlockSpec(wsd.shape, const)],
        out_specs=[pl.BlockSpec((tm, d), row_map), pl.BlockSpec((tm, d // 2), row_map),
                   pl.BlockSpec((N_EXPERTS, tm), lambda i: (0, i))],
        out_shape=[jax.ShapeDtypeStruct((t, d), F32), jax.ShapeDtypeStruct((t, d // 2), U32),
                   jax.ShapeDtypeStruct((N_EXPERTS, t), F32)],
        compiler_params=_cparams("parallel"),
        name="mixer_out",
    )(o_f, o_b, sz, cfo, x2, gt1, sh2, sc2, gt2, norm_g, w_out, g_ffn, rwt, wsgu, wsd)


def _first_argmax(vals, idx, sentinel):
    m = jnp.max(vals, axis=0, keepdims=True)
    return m, jnp.min(jnp.where(vals == m, idx, sentinel), axis=0, keepdims=True)


def _router_kernel(lg_ref, bias_ref, e_o, r_o, wt_o, cnt_o, carry_ref):
    @pl.when(pl.program_id(0) == 0)
    def _():
        carry_ref[...] = jnp.zeros(carry_ref.shape, F32)

    tt = lg_ref.shape[1]
    neg = -jnp.inf
    scores = jax.nn.sigmoid(lg_ref[...])
    sel = scores + bias_ref[...]
    sub = lax.broadcasted_iota(jnp.int32, (GROUP_SIZE, tt), 0)

    rows = []
    for g in range(N_GROUPS):
        sg = sel[g * GROUP_SIZE:(g + 1) * GROUP_SIZE]
        m1, first = _first_argmax(sg, sub, GROUP_SIZE)
        m2 = jnp.max(jnp.where(sub == first, neg, sg), axis=0, keepdims=True)
        rows.append(m1 + m2)
    cur = jnp.concatenate(rows, axis=0)
    gidx = lax.broadcasted_iota(jnp.int32, (N_GROUPS, tt), 0)
    keep = gidx < 0
    for _ in range(TOPK_GROUPS):
        _, a = _first_argmax(cur, gidx, N_GROUPS)
        pick = gidx == a
        keep = jnp.logical_or(keep, pick)
        cur = jnp.where(pick, neg, cur)
    keep_e = jnp.concatenate([jnp.broadcast_to(keep[g:g + 1], (GROUP_SIZE, tt)) for g in range(N_GROUPS)],
                             axis=0)
    masked = jnp.where(keep_e, sel, neg)

    eidx = lax.broadcasted_iota(jnp.int32, (N_EXPERTS, tt), 0)
    e_rows, w_rows, picks = [], [], []
    for _ in range(TOP_K):
        _, a = _first_argmax(masked, eidx, N_EXPERTS)
        pick = eidx == a
        e_rows.append(a)
        w_rows.append(jnp.sum(jnp.where(pick, scores, 0.0), axis=0, keepdims=True))
        picks.append(pick)
        masked = jnp.where(pick, neg, masked)
    onehot = sum(p.astype(F32) for p in picks)
    scale = ROUTED_SCALE / sum(w_rows)

    tri = (lax.broadcasted_iota(jnp.int32, (tt, tt), 0) < lax.broadcasted_iota(jnp.int32, (tt, tt), 1))
    cum = jnp.dot(onehot.astype(BF16), tri.astype(BF16), preferred_element_type=F32) + carry_ref[:, 0:1]
    r_rows = [jnp.sum(jnp.where(p, cum, 0.0), axis=0, keepdims=True).astype(jnp.int32) for p in picks]
    carry_ref[...] = carry_ref[...] + jnp.sum(onehot, axis=1, keepdims=True)
    cnt_o[...] = carry_ref[...]

    fill = 8 - TOP_K
    e_o[...] = jnp.concatenate(e_rows + [jnp.zeros((fill, tt), jnp.int32)], axis=0)
    r_o[...] = jnp.concatenate(r_rows + [jnp.zeros((fill, tt), jnp.int32)], axis=0)
    w_pad = jnp.concatenate([w * scale for w in w_rows] + [jnp.zeros((LANES - TOP_K, tt), F32)], axis=0)
    wt_o[...] = jnp.transpose(w_pad)


def _router(lg_t, router_bias, tt):
    e, t = lg_t.shape
    idx_spec = pl.BlockSpec((8, tt), lambda i: (0, i))
    return pl.pallas_call(
        _router_kernel,
        grid=(t // tt,),
        in_specs=[pl.BlockSpec((e, tt), lambda i: (0, i)), pl.BlockSpec((e, 1), lambda i: (0, 0))],
        out_specs=[idx_spec, idx_spec, pl.BlockSpec((tt, LANES), lambda i: (i, 0)),
                   pl.BlockSpec((e, LANES), lambda i: (0, 0))],
        out_shape=[jax.ShapeDtypeStruct((8, t), jnp.int32), jax.ShapeDtypeStruct((8, t), jnp.int32),
                   jax.ShapeDtypeStruct((t, LANES), F32), jax.ShapeDtypeStruct((e, LANES), F32)],
        scratch_shapes=[pltpu.VMEM((e, LANES), F32)],
        compiler_params=_cparams("arbitrary"),
        name="router",
    )(lg_t, router_bias.reshape(e, 1))


SLOT_STRIDE = 8
ROW_GROUP = 8


def _slot_kernel(pstart_ref, e_ref, r_ref, o_ref):
    e = e_ref[...]
    acc = r_ref[...]
    for x in range(N_EXPERTS):
        acc = acc + jnp.where(e == x, pstart_ref[x], 0)
    tt = acc.shape[1]
    pad = jnp.concatenate([acc, jnp.zeros((LANES - acc.shape[0], tt), jnp.int32)], axis=0)
    o_ref[...] = jnp.transpose(pad)[:, :SLOT_STRIDE]


def _slot_index(pstart, e_idx, rank, tt):
    rows, t = e_idx.shape
    spec = pl.BlockSpec((rows, tt), lambda i, ps: (0, i))
    slot = pl.pallas_call(
        _slot_kernel,
        grid_spec=pltpu.PrefetchScalarGridSpec(
            num_scalar_prefetch=1, grid=(t // tt,), in_specs=[spec, spec],
            out_specs=pl.BlockSpec((tt, SLOT_STRIDE), lambda i, ps: (i, 0))),
        out_shape=jax.ShapeDtypeStruct((t, SLOT_STRIDE), jnp.int32),
        compiler_params=_cparams("parallel"),
        name="slot_index",
    )(pstart, e_idx, rank)
    return slot.reshape(t * SLOT_STRIDE)


def _row_copies(n_tokens, make_copy):
    def body(g, carry):
        base = g * (ROW_GROUP * SLOT_STRIDE)
        for s in range(ROW_GROUP):
            for j in range(TOP_K):
                make_copy(g, s, j, base + (s * SLOT_STRIDE + j)).start(priority=(s * TOP_K + j) % 2)
        return carry

    lax.fori_loop(0, n_tokens // ROW_GROUP, body, 0)


def _dispatch_kernel(pstart_ref, count_ref, slot_ref, hn_ref, xs_hbm, zero_ref, sem):
    td = hn_ref.shape[0] * ROW_GROUP
    bm = zero_ref.shape[0]

    def zero_fill(e, wait):
        end = pstart_ref[e] + count_ref[e]
        aligned = pl.multiple_of((end + 7) // 8 * 8, 8)
        for i in range(7):
            @pl.when(end + i < aligned)
            def _():
                cp = pltpu.make_async_copy(zero_ref.at[pl.ds(0, 1)], xs_hbm.at[pl.ds(end + i, 1)], sem)
                cp.wait() if wait else cp.start()
        cp = pltpu.make_async_copy(zero_ref, xs_hbm.at[pl.ds(aligned, bm)], sem)
        cp.wait() if wait else cp.start()

    @pl.when(pl.program_id(0) == 0)
    def _():
        zero_ref[...] = jnp.zeros(zero_ref.shape, zero_ref.dtype)

        def start(e, carry):
            zero_fill(e, False)
            return carry

        def wait(e, carry):
            zero_fill(e, True)
            return carry

        lax.fori_loop(0, N_EXPERTS, start, 0)
        lax.fori_loop(0, N_EXPERTS, wait, 0)

        last = N_EXPERTS - 1
        tail = (pstart_ref[last] + count_ref[last] + bm - 1) // bm

        def tail_copy(b):
            return pltpu.make_async_copy(zero_ref, xs_hbm.at[pl.ds(pl.multiple_of(b * bm, bm), bm)], sem)

        def tail_start(b, carry):
            tail_copy(b).start()
            return carry

        def tail_wait(b, carry):
            tail_copy(b).wait()
            return carry

        lax.fori_loop(tail, xs_hbm.shape[0] // bm, tail_start, 0)
        lax.fori_loop(tail, xs_hbm.shape[0] // bm, tail_wait, 0)

    _row_copies(td, lambda g, s, j, i: pltpu.make_async_copy(
        hn_ref.at[g, pl.ds(s, 1)], xs_hbm.at[pl.ds(slot_ref[i], 1)], sem))
    pltpu.make_async_copy(xs_hbm.at[pl.ds(0, td * TOP_K)], xs_hbm.at[pl.ds(0, td * TOP_K)], sem).wait()


def _dispatch(pstart, counts, slot, hn, n_slots, td, bm):
    t, d = hn.shape
    return pl.pallas_call(
        _dispatch_kernel,
        grid_spec=pltpu.PrefetchScalarGridSpec(
            num_scalar_prefetch=2, grid=(t // td,),
            in_specs=[pl.BlockSpec((td * SLOT_STRIDE,), lambda i, ps, cn: (i,), memory_space=pltpu.SMEM),
                      pl.BlockSpec((td // ROW_GROUP, ROW_GROUP, d), lambda i, ps, cn: (i, 0, 0))],
            out_specs=pl.BlockSpec(memory_space=pl.ANY),
            scratch_shapes=[pltpu.VMEM((bm, d), hn.dtype), pltpu.SemaphoreType.DMA(())]),
        out_shape=jax.ShapeDtypeStruct((n_slots, d), hn.dtype),
        compiler_params=_cparams("arbitrary"),
        name="dispatch",
    )(pstart, counts, slot, hn.reshape(t // ROW_GROUP, ROW_GROUP, d))


EXPERT_SUB = 256


def _expert_kernel(bexp_ref, bsrc_ref, nused_ref, xs_ref, wg_ref, wu_ref, wd_ref, y_ref):
    del bexp_ref, bsrc_ref
    i = pl.program_id(0)

    @pl.when(i < nused_ref[0])
    def _():
        wg = wg_ref[0].astype(BF16)
        wu = wu_ref[0].astype(BF16)
        wd = wd_ref[0].astype(BF16)
        for r0 in range(0, xs_ref.shape[0], EXPERT_SUB):
            rows = slice(r0, r0 + EXPERT_SUB)
            xb = jnp.concatenate(_unpack_bf16_pairs(xs_ref[rows, :]), axis=1).astype(BF16)
            g = jnp.dot(xb, wg, preferred_element_type=F32)
            u = jnp.dot(xb, wu, preferred_element_type=F32)
            act = (_silu(g) * u).astype(BF16)
            y_ref[rows, :] = _pack_bf16_pairs(jnp.dot(act, wd, preferred_element_type=F32))

    @pl.when(i >= nused_ref[0])
    def _():
        y_ref[...] = jnp.zeros(y_ref.shape, y_ref.dtype)


def _experts(block_exp, block_src, n_used, xs, w_gate, w_up, w_down, bm):
    n_slots, d = xs.shape
    w_map = lambda i, be, bs, nu: (be[i], 0, 0)
    return pl.pallas_call(
        _expert_kernel,
        grid_spec=pltpu.PrefetchScalarGridSpec(
            num_scalar_prefetch=3, grid=(n_slots // bm,),
            in_specs=[pl.BlockSpec((bm, d), lambda i, be, bs, nu: (bs[i], 0)),
                      pl.BlockSpec((1,) + w_gate.shape[1:], w_map),
                      pl.BlockSpec((1,) + w_up.shape[1:], w_map),
                      pl.BlockSpec((1,) + w_down.shape[1:], w_map)],
            out_specs=pl.BlockSpec((bm, d), lambda i, be, bs, nu: (i, 0))),
        out_shape=jax.ShapeDtypeStruct((n_slots, d), xs.dtype),
        compiler_params=_cparams("arbitrary"),
        name="experts",
    )(block_exp, block_src, n_used, xs, w_gate, w_up, w_down)


def _combine_kernel(slot_ref, wt_ref, base_ref, gt2_ref, gfin_ref, y_hbm, o_ref, buf, sem):
    tg = base_ref.shape[0]
    _row_copies(tg, lambda g, s, j, i: pltpu.make_async_copy(
        y_hbm.at[pl.ds(slot_ref[i], 1)], buf.at[j, g, pl.ds(s, 1)], sem))
    pltpu.make_async_copy(buf, buf, sem).wait()
    wt = wt_ref[...]
    lo = hi = None
    for j in range(TOP_K):
        ylo, yhi = _unpack_bf16_pairs(buf[j].reshape(tg, buf.shape[-1]))
        w = wt[:, j:j + 1]
        lo = w * ylo if lo is None else lo + w * ylo
        hi = w * yhi if hi is None else hi + w * yhi
    xf = base_ref[...] + gt2_ref[0] * jnp.concatenate([lo, hi], axis=1)
    o_ref[...] = xf * lax.rsqrt(jnp.mean(xf * xf, axis=-1, keepdims=True) + EPS) * gfin_ref[...]


def _combine(slot, wt, base, gt2, g_final, y, seq, tg):
    t, d = base.shape
    tiles = seq // tg
    return pl.pallas_call(
        _combine_kernel,
        grid=(t // tg,),
        in_specs=[pl.BlockSpec((tg * SLOT_STRIDE,), lambda i: (i,), memory_space=pltpu.SMEM),
                  pl.BlockSpec((tg, LANES), lambda i: (i, 0)),
                  pl.BlockSpec((tg, d), lambda i: (i, 0)),
                  pl.BlockSpec((1, 1, d), lambda i: (i // tiles, 0, 0)),
                  pl.BlockSpec((1, d), lambda i: (0, 0)),
                  pl.BlockSpec(memory_space=pl.ANY)],
        out_specs=pl.BlockSpec((tg, d), lambda i: (i, 0)),
        out_shape=jax.ShapeDtypeStruct((t, d), F32),
        scratch_shapes=[pltpu.VMEM((TOP_K, tg // ROW_GROUP, ROW_GROUP, y.shape[1]), y.dtype),
                        pltpu.SemaphoreType.DMA(())],
        compiler_params=_cparams("arbitrary"),
        name="combine",
    )(slot, wt, base, gt2, g_final.reshape(1, d), y)


EXPERT_BLOCK = 512
ROW_TILE = 512
ROUTER_TILE = 512
GATHER_TILE = 512
SLOT_TILE = 4096


def _moe_plan(counts, n_tokens, bm):
    padded = (counts + bm - 1) // bm * bm
    pad_end = jnp.cumsum(padded)
    pad_start = (pad_end - padded).astype(jnp.int32)
    n_blocks = -(-(n_tokens * TOP_K) // bm) + N_EXPERTS + 1
    n_used = (pad_end[-1] // bm).astype(jnp.int32)
    block_src = jnp.minimum(jnp.arange(n_blocks, dtype=jnp.int32), jnp.maximum(n_used - 1, 0))
    block_exp = jnp.sum((pad_end[None, :] <= (block_src * bm)[:, None]).astype(jnp.int32), axis=1)
    block_exp = jnp.minimum(block_exp, N_EXPERTS - 1)
    return pad_start, block_exp, block_src, n_used.reshape(1), n_blocks * bm


def kernel(x, c, ctx, c_ctx, w_mod, b_mod, g_mix, g_ffn, w_in, w_out, dn_conv_w, dn_a_log, dn_dt_bias,
           dn_norm_g, cf_dw_w, cf_dw_b, cf_ln_g, cf_ln_b, router_w, router_bias, exp_w_gate, exp_w_up,
           exp_w_down, sh_w_gate, sh_w_up, sh_w_down, g_final):
    assert w_mod.shape[0] == 1, "single-layer block: the context stream is never re-read"
    bsz, seq, d = x.shape
    t = bsz * seq
    x2 = x.reshape(t, d)
    st = _mixer_front(x, c, ctx, c_ctx, w_mod[0], b_mod[0], g_mix[0], w_in[0], dn_conv_w[0], dn_a_log[0],
                      dn_dt_bias[0])
    _, _, gt1, sh2, sc2, gt2 = st['lat']
    cfo = _cf_conv(st['y'], cf_dw_w[0], cf_dw_b[0], cf_ln_g[0], cf_ln_b[0], bsz)
    wsgu = jnp.concatenate([sh_w_gate[0], sh_w_up[0]], axis=1).astype(BF16)
    base, hn, lg_t = _mixer_out(st['o_f'], st['o_b'], st['sz'], cfo, x2, gt1, sh2, sc2, gt2,
                                dn_norm_g[0].reshape(1, DN_D), w_out[0].astype(BF16), g_ffn[0].reshape(1, d),
                                router_w[0].T, wsgu, sh_w_down[0].astype(BF16), seq, min(seq, ROW_TILE))
    e_idx, rank, wt, cnt = _router(lg_t, router_bias[0], min(t, ROUTER_TILE))
    counts = cnt[:, 0].astype(jnp.int32)
    pstart, block_exp, block_src, n_used, n_slots = _moe_plan(counts, t, EXPERT_BLOCK)
    slot = _slot_index(pstart, e_idx, rank, min(t, SLOT_TILE))
    xs = _dispatch(pstart, counts, slot, hn, n_slots, min(t, GATHER_TILE), EXPERT_BLOCK)
    y = _experts(block_exp, block_src, n_used, xs, exp_w_gate[0], exp_w_up[0], exp_w_down[0], EXPERT_BLOCK)
    out = _combine(slot, wt, base, gt2, g_final, y, seq, min(seq, GATHER_TILE))
    return out.reshape(bsz, seq, d)
```

```python
import functools

import jax
import jax.numpy as jnp
from jax import lax
from jax.experimental import pallas as pl
from jax.experimental.pallas import tpu as pltpu

F32 = jnp.float32
U32 = jnp.uint32
BF16 = jnp.bfloat16
HIGHEST = lax.Precision.HIGHEST

EPS = 1e-6
LANES = 128
GRID_W = 64
DN_HEADS = 4
DN_D = 128
DN_QK = DN_HEADS * DN_D
DN_WIDTH = DN_HEADS * DN_D
DN_CONV_DIM = 2 * DN_QK + DN_WIDTH
SHORT_CONV = 7
CHUNK = 64
CF_CH = 512
CF_K = 31
N_EXPERTS = 64
TOP_K = 6
N_GROUPS = 8
GROUP_SIZE = N_EXPERTS // N_GROUPS
TOPK_GROUPS = 4
ROUTED_SCALE = 2.5
HALO = 8
VMEM_LIMIT = 56 * 1024 * 1024


def _cparams(*sem):
    return pltpu.CompilerParams(dimension_semantics=sem, vmem_limit_bytes=VMEM_LIMIT)


def _silu(v):
    return v * jax.nn.sigmoid(v)


def _bdot(a, b):
    return jnp.dot(a.astype(BF16), b.astype(BF16), preferred_element_type=F32)


def _pack_bf16_pairs(v):
    n = v.shape[1] // 2
    lo = lax.bitcast_convert_type(v[:, :n].astype(BF16).astype(F32), U32)
    hi = lax.bitcast_convert_type(v[:, n:].astype(BF16).astype(F32), U32)
    return (lo >> 16) | (hi & jnp.uint32(0xFFFF0000))


def _unpack_bf16_pairs(w):
    lo = lax.bitcast_convert_type(w << 16, F32)
    hi = lax.bitcast_convert_type(w & jnp.uint32(0xFFFF0000), F32)
    return lo, hi


def _mod_kernel(c_ref, w_ref, b_ref, o_ref):
    o_ref[...] = jnp.dot(_silu(c_ref[...]), w_ref[...], preferred_element_type=F32,
                         precision=HIGHEST) + b_ref[...]


def _modulation(cc, w_mod, b_mod):
    rows, d = cc.shape
    n = w_mod.shape[1]
    tn = 1024
    return pl.pallas_call(
        _mod_kernel,
        grid=(n // tn,),
        in_specs=[pl.BlockSpec((rows, d), lambda j: (0, 0)),
                  pl.BlockSpec((d, tn), lambda j: (0, j)),
                  pl.BlockSpec((1, tn), lambda j: (0, j))],
        out_specs=pl.BlockSpec((rows, tn), lambda j: (0, j)),
        out_shape=jax.ShapeDtypeStruct((rows, n), F32),
        compiler_params=_cparams("parallel"),
        name="modulation",
    )(cc, w_mod, b_mod.reshape(1, n))


def _chunk_cumsum(g, reverse):
    n = g.shape[0]
    pos = lax.broadcasted_iota(jnp.int32, g.shape, 0) % CHUNK
    s = 1
    while s < CHUNK:
        if reverse:
            shifted = pltpu.roll(g, n - s, 0)
            ok = pos < CHUNK - s
        else:
            shifted = pltpu.roll(g, s, 0)
            ok = pos >= s
        g = g + jnp.where(ok, shifted, 0.0)
        s *= 2
    return g


def _inproj_kernel(latent, x_ref, sh_ref, sc_ref, g_ref, wqkv_ref, wba_ref, alog_ref, dtb_ref, *rest):
    if latent:
        wz_ref, wcf_ref, qkv_o, bgc_o, bgct_o, sz_o, y_o = rest
    else:
        qkv_o, bgc_o, bgct_o = rest
    x = x_ref[...]
    xn = x * lax.rsqrt(jnp.mean(x * x, axis=-1, keepdims=True) + EPS) * g_ref[...]
    hb = (xn * (1.0 + sc_ref[0]) + sh_ref[0]).astype(BF16)
    qkv_o[...] = jnp.dot(hb, wqkv_ref[...], preferred_element_type=F32)

    ba = jnp.dot(hb, wba_ref[...], preferred_element_type=F32)
    col = lax.broadcasted_iota(jnp.int32, ba.shape, 1)
    is_beta = (col % 8) < DN_HEADS
    g = -jnp.exp(alog_ref[...]) * jax.nn.softplus(ba + dtb_ref[...])
    g = jnp.where(is_beta, 0.0, g)
    gc = jnp.where(col < 8, _chunk_cumsum(g, False), _chunk_cumsum(g, True))
    bgc = jnp.where(is_beta, jax.nn.sigmoid(ba), gc)
    bgc_o[...] = bgc
    for c in range(bgc.shape[0] // CHUNK):
        bgct_o[c] = jnp.transpose(bgc[c * CHUNK:(c + 1) * CHUNK, :])[:16, :]

    if latent:
        sz_o[...] = _silu(jnp.dot(hb, wz_ref[...], preferred_element_type=F32))
        cf = jnp.dot(hb, wcf_ref[...], preferred_element_type=F32)
        y_o[...] = cf[:, :CF_CH] * jax.nn.sigmoid(cf[:, CF_CH:])


def _in_projection(x2, sh, sc, g_mix, wqkv, wba, alog_row, dtb_row, wz, wcf, rows_per_mod, tm):
    t, d = x2.shape
    latent = wz is not None
    tiles_per_mod = rows_per_mod // tm
    const = lambda i: (0, 0)
    mod_map = lambda i: (i // tiles_per_mod, 0, 0)
    row_map = lambda i: (i, 0)
    in_specs = [pl.BlockSpec((tm, d), row_map),
                pl.BlockSpec((1, 1, d), mod_map), pl.BlockSpec((1, 1, d), mod_map),
                pl.BlockSpec((1, d), const),
                pl.BlockSpec(wqkv.shape, const), pl.BlockSpec(wba.shape, const),
                pl.BlockSpec((1, LANES), const), pl.BlockSpec((1, LANES), const)]
    args = [x2, sh, sc, g_mix, wqkv, wba, alog_row, dtb_row]
    out_specs = [pl.BlockSpec((tm, DN_CONV_DIM), row_map), pl.BlockSpec((tm, LANES), row_map),
                 pl.BlockSpec((tm // CHUNK, 16, CHUNK), lambda i: (i, 0, 0))]
    out_shape = [jax.ShapeDtypeStruct((t, DN_CONV_DIM), F32), jax.ShapeDtypeStruct((t, LANES), F32),
                 jax.ShapeDtypeStruct((t // CHUNK, 16, CHUNK), F32)]
    if latent:
        in_specs += [pl.BlockSpec(wz.shape, const), pl.BlockSpec(wcf.shape, const)]
        args += [wz, wcf]
        out_specs += [pl.BlockSpec((tm, DN_WIDTH), row_map), pl.BlockSpec((tm, CF_CH), row_map)]
        out_shape += [jax.ShapeDtypeStruct((t, DN_WIDTH), F32), jax.ShapeDtypeStruct((t, CF_CH), F32)]
    return pl.pallas_call(
        functools.partial(_inproj_kernel, latent),
        grid=(t // tm,),
        in_specs=in_specs, out_specs=out_specs, out_shape=out_shape,
        compiler_params=_cparams("parallel"),
        name="in_projection_latent" if latent else "in_projection_context",
    )(*args)


CONV_ROWS = 64


def _shortconv_kernel(tiles_per_seq, prev_ref, cur_ref, next_ref, w_ref, o_ref, ext_ref):
    i = pl.program_id(0)
    tm = cur_ref.shape[0]
    pos = i % tiles_per_seq
    ext_ref[0:HALO, :] = jnp.where(pos == 0, 0.0, prev_ref[...])
    ext_ref[HALO:HALO + tm, :] = cur_ref[...]
    ext_ref[HALO + tm:, :] = jnp.where(pos == tiles_per_seq - 1, 0.0, next_ref[...])
    reach = SHORT_CONV // 2

    for r0 in range(0, tm, CONV_ROWS):
        for cb in range(DN_CONV_DIM // LANES):
            cols = slice(cb * LANES, (cb + 1) * LANES)
            acc = jnp.zeros((CONV_ROWS, LANES), F32)
            for k in range(SHORT_CONV):
                s = r0 + HALO - reach + k
                acc = acc + w_ref[k:k + 1, cols] * ext_ref[s:s + CONV_ROWS, cols]
            a = _silu(acc)
            if cb < 2 * DN_HEADS:
                a = a * lax.rsqrt(jnp.sum(a * a, axis=-1, keepdims=True) + EPS)
            if cb < DN_HEADS:
                a = a * (DN_D ** -0.5)
            o_ref[r0:r0 + CONV_ROWS, cols] = a


def _short_conv(qkv, conv_w, seq_len, tm):
    t, c = qkv.shape
    tiles_per_seq = seq_len // tm
    hb = tm // HALO
    n_halo_blocks = t // HALO
    return pl.pallas_call(
        functools.partial(_shortconv_kernel, tiles_per_seq),
        grid=(t // tm,),
        in_specs=[pl.BlockSpec((HALO, c), lambda i: (jnp.maximum(i * hb - 1, 0), 0)),
                  pl.BlockSpec((tm, c), lambda i: (i, 0)),
                  pl.BlockSpec((HALO, c), lambda i: (jnp.minimum((i + 1) * hb, n_halo_blocks - 1), 0)),
                  pl.BlockSpec(conv_w.shape, lambda i: (0, 0))],
        out_specs=pl.BlockSpec((tm, c), lambda i: (i, 0)),
        out_shape=jax.ShapeDtypeStruct((t, c), F32),
        scratch_shapes=[pltpu.VMEM((tm + 2 * HALO, c), F32)],
        compiler_params=_cparams("parallel"),
        name="short_conv",
    )(qkv, qkv, qkv, conv_w)


DELTA_STEPS = 2


def _delta_pre(refs, ci_f, ci_b, masks, eye):
    qkv_ref, bgc_ref, bgct_ref = refs
    chains = range(2 * DN_HEADS * len(ci_f))
    q, k, v, beta, gcol, decay, eg, gl = [], [], [], [], [], [], [], []
    for d, ci in [(d, c[step]) for step in range(len(ci_f)) for d, c in enumerate((ci_f, ci_b))]:
        rows = pl.ds(pl.multiple_of(ci * CHUNK, CHUNK), CHUNK)
        bg = bgc_ref[rows, :]
        bgt = bgct_ref[ci]
        for h in range(DN_HEADS):
            q.append(qkv_ref[rows, h * DN_D:(h + 1) * DN_D])
            k.append(qkv_ref[rows, DN_QK + h * DN_D:DN_QK + (h + 1) * DN_D])
            v.append(qkv_ref[rows, 2 * DN_QK + h * DN_D:2 * DN_QK + (h + 1) * DN_D])
            jb = d * 8 + h
            jg = d * 8 + DN_HEADS + h
            beta.append(bg[:, jb:jb + 1])
            gc = bg[:, jg:jg + 1]
            gcol.append(gc)
            decay.append(jnp.exp(jnp.where(masks[d][0], gc - bgt[jg:jg + 1, :], -jnp.inf)))
            eg.append(jnp.exp(gc))
            gl.append(gc[CHUNK - 1:CHUNK] if d == 0 else gc[0:1])

    contract_last = (((1,), (1,)), ((), ()))
    gram = [lax.dot_general(jnp.concatenate([k[n], q[n]], axis=0).astype(BF16), k[n].astype(BF16),
                            contract_last, preferred_element_type=F32) for n in chains]
    lmat = [jnp.where(masks[n // DN_HEADS % 2][1], gram[n][:CHUNK] * decay[n], 0.0) * beta[n] for n in chains]
    tinv = [eye - lmat[n] for n in chains]
    p = [_bdot(lmat[n], lmat[n]) for n in chains]
    n_sq = 2
    while n_sq < CHUNK // 2:
        x = [_bdot(jnp.concatenate([tinv[n], p[n]], axis=0), p[n]) for n in chains]
        tinv = [tinv[n] + x[n][:CHUNK] for n in chains]
        p = [x[n][CHUNK:] for n in chains]
        n_sq *= 2
    tinv = [tinv[n] + _bdot(tinv[n], p[n]) for n in chains]
    uw = [_bdot(tinv[n], jnp.concatenate([v[n] * beta[n], k[n] * (beta[n] * eg[n])], axis=1)) for n in chains]

    return [(uw[n][:, :DN_D], uw[n][:, DN_D:], q[n] * eg[n], k[n] * jnp.exp(gl[n] - gcol[n]),
             gram[n][CHUNK:] * decay[n], jnp.exp(gl[n])) for n in chains]


def _delta_rec(pre, s_ref, want_out):
    chains = range(len(pre))
    contract_first = (((0,), (0,)), ((), ()))
    s = [s_ref[n] for n in chains]
    if want_out:
        wq = [_bdot(jnp.concatenate([pre[n][1], pre[n][2]], axis=0), s[n]) for n in chains]
        ws = [wq[n][:CHUNK] for n in chains]
    else:
        ws = [_bdot(pre[n][1], s[n]) for n in chains]
    vb = [(pre[n][0] - ws[n]).astype(BF16) for n in chains]
    for n in chains:
        s_ref[n] = s[n] * pre[n][5] + lax.dot_general(pre[n][3].astype(BF16), vb[n], contract_first,
                                                      preferred_element_type=F32)
    if not want_out:
        return None
    return [wq[n][CHUNK:] + jnp.dot(pre[n][4].astype(BF16), vb[n], preferred_element_type=F32)
            for n in chains]


def _delta_kernel(qc_ref, bc_ref, btc_ref, qx_ref, bx_ref, btx_ref, of_ref, ob_ref, s_ref, p_ref):
    s_ref[...] = jnp.zeros(s_ref.shape, F32)
    row = lax.broadcasted_iota(jnp.int32, (CHUNK, CHUNK), 0)
    col = lax.broadcasted_iota(jnp.int32, (CHUNK, CHUNK), 1)
    masks = ((row >= col, row > col), (row <= col, row < col))
    eye = (row == col).astype(F32)
    per_step = 2 * DN_HEADS
    n_chains = DELTA_STEPS * per_step

    def stage(pre):
        for n in range(n_chains):
            u, w, qe, kd, amat, egl = pre[n]
            for j, val in enumerate((u, w, qe, kd)):
                p_ref[n, j] = val
            p_ref[n, 4] = jnp.concatenate([amat, jnp.broadcast_to(egl, (CHUNK, DN_D - CHUNK))], axis=1)

    def staged():
        return [(p_ref[n, 0], p_ref[n, 1], p_ref[n, 2], p_ref[n, 3], p_ref[n, 4][:, :CHUNK],
                 p_ref[n, 4][0:1, CHUNK:CHUNK + 1]) for n in range(n_chains)]

    def scan(refs, want_out):
        n_chunks = refs[0].shape[0] // CHUNK
        assert n_chunks % DELTA_STEPS == 0

        def pre(first):
            fwd = [first + s for s in range(DELTA_STEPS)]
            return _delta_pre(refs, fwd, [n_chunks - 1 - c for c in fwd], masks, eye)

        stage(pre(0))

        def body(it, carry):
            i = it * DELTA_STEPS
            cur = staged()
            new = pre(jnp.minimum(i + DELTA_STEPS, n_chunks - DELTA_STEPS))
            for s in range(DELTA_STEPS):
                o = _delta_rec(cur[s * per_step:(s + 1) * per_step], s_ref, want_out)
                if want_out:
                    rf = pl.ds(pl.multiple_of((i + s) * CHUNK, CHUNK), CHUNK)
                    rb = pl.ds(pl.multiple_of((n_chunks - 1 - i - s) * CHUNK, CHUNK), CHUNK)
                    for h in range(DN_HEADS):
                        cols = slice(h * DN_D, (h + 1) * DN_D)
                        of_ref[rf, cols] = o[h]
                        ob_ref[rb, cols] = o[DN_HEADS + h]
            stage(new)
            return carry

        lax.fori_loop(0, n_chunks // DELTA_STEPS, body, 0)

    scan((qc_ref, bc_ref, btc_ref), False)
    scan((qx_ref, bx_ref, btx_ref), True)


def _delta_scan(qkv_c, bgc_c, bgct_c, qkv_x, bgc_x, bgct_x, bsz):
    lc = qkv_c.shape[0] // bsz
    lx = qkv_x.shape[0] // bsz
    row_map = lambda b: (b, 0)
    out = jax.ShapeDtypeStruct((bsz * lx, DN_WIDTH), F32)
    return pl.pallas_call(
        _delta_kernel,
        grid=(bsz,),
        in_specs=[pl.BlockSpec((lc, DN_CONV_DIM), row_map), pl.BlockSpec((lc, LANES), row_map),
                  pl.BlockSpec((lc // CHUNK, 16, CHUNK), lambda b: (b, 0, 0)),
                  pl.BlockSpec((lx, DN_CONV_DIM), row_map), pl.BlockSpec((lx, LANES), row_map),
                  pl.BlockSpec((lx // CHUNK, 16, CHUNK), lambda b: (b, 0, 0))],
        out_specs=[pl.BlockSpec((lx, DN_WIDTH), row_map), pl.BlockSpec((lx, DN_WIDTH), row_map)],
        out_shape=[out, out],
        scratch_shapes=[pltpu.VMEM((2 * DN_HEADS, DN_D, DN_D), F32),
                        pltpu.VMEM((DELTA_STEPS * 2 * DN_HEADS, 5, CHUNK, DN_D), F32)],
        compiler_params=_cparams("parallel"),
        name="delta_scan",
    )(qkv_c, bgc_c, bgct_c, qkv_x, bgc_x, bgct_x)


def _pad_rows(a, rows):
    return jnp.pad(a, ((0, rows - a.shape[0]), (0, 0)))


def _mixer_front(x, c, ctx, c_ctx, w_mod, b_mod, g_mix, w_in, conv_w, a_log, dt_bias):
    bsz, seq, d = x.shape
    lc = ctx.shape[1]
    off_z = DN_CONV_DIM
    off_ba = off_z + DN_WIDTH
    off_cf = off_ba + 4 * DN_HEADS

    cc = _pad_rows(jnp.concatenate([c, c_ctx[None, :]], axis=0), -(-(bsz + 1) // 8) * 8)
    mod = _modulation(cc, w_mod, b_mod)
    mods = [mod[:, j * d:(j + 1) * d] for j in range(6)]
    lat = [m[:bsz].reshape(bsz, 1, d) for m in mods]
    con = [m[bsz:bsz + 1].reshape(1, 1, d) for m in mods]

    wqkv = w_in[:, :off_z].astype(BF16)
    wz = w_in[:, off_z:off_ba].astype(BF16)
    wba = jnp.pad(w_in[:, off_ba:off_cf], ((0, 0), (0, LANES - 4 * DN_HEADS))).astype(BF16)
    wcf = w_in[:, off_cf:].astype(BF16)
    zeros4 = jnp.zeros((2, DN_HEADS), F32)
    alog_row = jnp.pad(jnp.concatenate([zeros4, a_log], axis=1).reshape(1, -1), ((0, 0), (0, LANES - 16)))
    dtb_row = jnp.pad(jnp.concatenate([zeros4, dt_bias], axis=1).reshape(1, -1), ((0, 0), (0, LANES - 16)))
    g_row = g_mix.reshape(1, d)

    qkv_c, bgc_c, bgct_c = _in_projection(ctx.reshape(bsz * lc, d), con[0], con[1], g_row, wqkv, wba,
                                          alog_row, dtb_row, None, None, bsz * lc, min(lc, 512))
    qkv_x, bgc_x, bgct_x, sz, y = _in_projection(x.reshape(bsz * seq, d), lat[0], lat[1], g_row, wqkv, wba,
                                                 alog_row, dtb_row, wz, wcf, seq, min(seq, 512))
    qkvn_c = _short_conv(qkv_c, conv_w, lc, min(lc, 512))
    qkvn_x = _short_conv(qkv_x, conv_w, seq, min(seq, 512))
    o_f, o_b = _delta_scan(qkvn_c, bgc_c, bgct_c, qkvn_x, bgc_x, bgct_x, bsz)
    return dict(mod=mod, lat=lat, qkv_x=qkv_x, qkv_c=qkv_c, sz=sz, y=y, qkvn_x=qkvn_x, bgc_x=bgc_x,
                bgct_x=bgct_x, o_f=o_f, o_b=o_b)


CF_ROWS = 32


def _cfconv_kernel(y_ref, w_ref, b_ref, lng_ref, lnb_ref, o_ref, pad_ref):
    n = y_ref.shape[0]
    reach = (CF_K // 2) * GRID_W
    pad_ref[0:reach, :] = jnp.zeros((reach, CF_CH), F32)
    pad_ref[reach:reach + n, :] = y_ref[...]
    pad_ref[reach + n:, :] = jnp.zeros((reach, CF_CH), F32)

    def body(r, carry):
        r0 = pl.multiple_of(r * CF_ROWS, CF_ROWS)
        acc = jnp.zeros((CF_ROWS, CF_CH), F32)
        for k in range(CF_K):
            acc = acc + w_ref[k:k + 1, :] * pad_ref[pl.ds(r0 + k * GRID_W, CF_ROWS), :]
        acc = acc + b_ref[...]
        mu = jnp.mean(acc, axis=-1, keepdims=True)
        xc = acc - mu
        var = jnp.mean(xc * xc, axis=-1, keepdims=True)
        o_ref[pl.ds(r0, CF_ROWS), :] = _silu(xc * lax.rsqrt(var + EPS) * lng_ref[...] + lnb_ref[...])
        return carry

    lax.fori_loop(0, n // CF_ROWS, body, 0, unroll=4)


def _cf_conv(y, dw_w, dw_b, ln_g, ln_b, bsz):
    t, ch = y.shape
    n = t // bsz
    reach = (CF_K // 2) * GRID_W
    const = lambda b: (0, 0)
    return pl.pallas_call(
        _cfconv_kernel,
        grid=(bsz,),
        in_specs=[pl.BlockSpec((n, ch), lambda b: (b, 0)), pl.BlockSpec(dw_w.shape, const),
                  pl.BlockSpec((1, ch), const), pl.BlockSpec((1, ch), const), pl.BlockSpec((1, ch), const)],
        out_specs=pl.BlockSpec((n, ch), lambda b: (b, 0)),
        out_shape=jax.ShapeDtypeStruct((t, ch), F32),
        scratch_shapes=[pltpu.VMEM((n + 2 * reach, ch), F32)],
        compiler_params=_cparams("parallel"),
        name="conformer_conv",
    )(y, dw_w, dw_b.reshape(1, ch), ln_g.reshape(1, ch), ln_b.reshape(1, ch))


def _mixout_kernel(of_ref, ob_ref, sz_ref, cf_ref, x_ref, gt1_ref, sh2_ref, sc2_ref, gt2_ref, ng_ref, wo_ref,
                   gffn_ref, rwt_ref, wsgu_ref, wsd_ref, base_o, hn_o, lg_o):
    contract_last = (((1,), (1,)), ((), ()))
    rw = rwt_ref[...]
    rw_hi = rw.astype(BF16)
    rw_lo = (rw - rw_hi.astype(F32)).astype(BF16)
    rw_both = jnp.concatenate([rw_hi, rw_lo], axis=0)
    tm = x_ref.shape[0]
    for r0 in range(0, tm, tm // 2):
        rows = slice(r0, r0 + tm // 2)
        o = of_ref[rows, :] + ob_ref[rows, :]
        parts = []
        for h in range(DN_HEADS):
            oh = o[:, h * DN_D:(h + 1) * DN_D]
            parts.append(oh * lax.rsqrt(jnp.mean(oh * oh, axis=-1, keepdims=True) + EPS) * ng_ref[...])
        dn = jnp.concatenate(parts, axis=1) * sz_ref[rows, :]
        heads = jnp.concatenate([dn, cf_ref[rows, :]], axis=1).astype(BF16)
        x1 = x_ref[rows, :] + gt1_ref[0] * jnp.dot(heads, wo_ref[...], preferred_element_type=F32)
        hn = (x1 * lax.rsqrt(jnp.mean(x1 * x1, axis=-1, keepdims=True) + EPS) * gffn_ref[...]
              * (1.0 + sc2_ref[0]) + sh2_ref[0])
        hn_o[rows, :] = _pack_bf16_pairs(hn)
        hn_hi = hn.astype(BF16)
        hn_lo = (hn - hn_hi.astype(F32)).astype(BF16)
        both = lax.dot_general(rw_both, hn_hi, contract_last, preferred_element_type=F32)
        lg_o[:, rows] = (both[:N_EXPERTS] + both[N_EXPERTS:]
                         + lax.dot_general(rw_hi, hn_lo, contract_last, preferred_element_type=F32))
        gu = jnp.dot(hn_hi, wsgu_ref[...], preferred_element_type=F32)
        ds = gu.shape[1] // 2
        act = (_silu(gu[:, :ds]) * gu[:, ds:]).astype(BF16)
        base_o[rows, :] = x1 + gt2_ref[0] * jnp.dot(act, wsd_ref[...], preferred_element_type=F32)


def _mixer_out(o_f, o_b, sz, cfo, x2, gt1, sh2, sc2, gt2, norm_g, w_out, g_ffn, rwt, wsgu, wsd, seq, tm):
    t, d = x2.shape
    tiles = seq // tm
    const = lambda i: (0, 0)
    row_map = lambda i: (i, 0)
    mod_map = lambda i: (i // tiles, 0, 0)
    half = pl.BlockSpec((tm, DN_WIDTH), row_map)
    mod_spec = pl.BlockSpec((1, 1, d), mod_map)
    return pl.pallas_call(
        _mixout_kernel,
        grid=(t // tm,),
        in_specs=[half, half, half, half, pl.BlockSpec((tm, d), row_map),
                  mod_spec, mod_spec, mod_spec, mod_spec,
                  pl.BlockSpec((1, DN_D), const), pl.BlockSpec(w_out.shape, const), pl.BlockSpec((1, d), const),
                  pl.BlockSpec(rwt.shape, const), pl.BlockSpec(wsgu.shape, const), pl.BlockSpec(wsd.shape, const)],
        out_specs=[pl.BlockSpec((tm, d), row_map), pl.BlockSpec((tm, d // 2), row_map),
                   pl.BlockSpec((N_EXPERTS, tm), lambda i: (0, i))],
        out_shape=[jax.ShapeDtypeStruct((t, d), F32), jax.ShapeDtypeStruct((t, d // 2), U32),
                   jax.ShapeDtypeStruct((N_EXPERTS, t), F32)],
        compiler_params=_cparams("parallel"),
        name="mixer_out",
    )(o_f, o_b, sz, cfo, x2, gt1, sh2, sc2, gt2, norm_g, w_out, g_ffn, rwt, wsgu, wsd)


def _first_argmax(vals, idx, sentinel):
    m = jnp.max(vals, axis=0, keepdims=True)
    return m, jnp.min(jnp.where(vals == m, idx, sentinel), axis=0, keepdims=True)


def _router_kernel(lg_ref, bias_ref, e_o, r_o, wt_o, cnt_o, carry_ref):
    @pl.when(pl.program_id(0) == 0)
    def _():
        carry_ref[...] = jnp.zeros(carry_ref.shape, F32)

    tt = lg_ref.shape[1]
    neg = -jnp.inf
    scores = jax.nn.sigmoid(lg_ref[...])
    sel = scores + bias_ref[...]
    sub = lax.broadcasted_iota(jnp.int32, (GROUP_SIZE, tt), 0)

    rows = []
    for g in range(N_GROUPS):
        sg = sel[g * GROUP_SIZE:(g + 1) * GROUP_SIZE]
        m1, first = _first_argmax(sg, sub, GROUP_SIZE)
        m2 = jnp.max(jnp.where(sub == first, neg, sg), axis=0, keepdims=True)
        rows.append(m1 + m2)
    cur = jnp.concatenate(rows, axis=0)
    gidx = lax.broadcasted_iota(jnp.int32, (N_GROUPS, tt), 0)
    keep = gidx < 0
    for _ in range(TOPK_GROUPS):
        _, a = _first_argmax(cur, gidx, N_GROUPS)
        pick = gidx == a
        keep = jnp.logical_or(keep, pick)
        cur = jnp.where(pick, neg, cur)
    keep_e = jnp.concatenate([jnp.broadcast_to(keep[g:g + 1], (GROUP_SIZE, tt)) for g in range(N_GROUPS)],
                             axis=0)
    masked = jnp.where(keep_e, sel, neg)

    eidx = lax.broadcasted_iota(jnp.int32, (N_EXPERTS, tt), 0)
    e_rows, w_rows, picks = [], [], []
    for _ in range(TOP_K):
        _, a = _first_argmax(masked, eidx, N_EXPERTS)
        pick = eidx == a
        e_rows.append(a)
        w_rows.append(jnp.sum(jnp.where(pick, scores, 0.0), axis=0, keepdims=True))
        picks.append(pick)
        masked = jnp.where(pick, neg, masked)
    onehot = sum(p.astype(F32) for p in picks)
    scale = ROUTED_SCALE / sum(w_rows)

    tri = (lax.broadcasted_iota(jnp.int32, (tt, tt), 0) < lax.broadcasted_iota(jnp.int32, (tt, tt), 1))
    cum = jnp.dot(onehot.astype(BF16), tri.astype(BF16), preferred_element_type=F32) + carry_ref[:, 0:1]
    r_rows = [jnp.sum(jnp.where(p, cum, 0.0), axis=0, keepdims=True).astype(jnp.int32) for p in picks]
    carry_ref[...] = carry_ref[...] + jnp.sum(onehot, axis=1, keepdims=True)
    cnt_o[...] = carry_ref[...]

    fill = 8 - TOP_K
    e_o[...] = jnp.concatenate(e_rows + [jnp.zeros((fill, tt), jnp.int32)], axis=0)
    r_o[...] = jnp.concatenate(r_rows + [jnp.zeros((fill, tt), jnp.int32)], axis=0)
    w_pad = jnp.concatenate([w * scale for w in w_rows] + [jnp.zeros((LANES - TOP_K, tt), F32)], axis=0)
    wt_o[...] = jnp.transpose(w_pad)


def _router(lg_t, router_bias, tt):
    e, t = lg_t.shape
    idx_spec = pl.BlockSpec((8, tt), lambda i: (0, i))
    return pl.pallas_call(
        _router_kernel,
        grid=(t // tt,),
        in_specs=[pl.BlockSpec((e, tt), lambda i: (0, i)), pl.BlockSpec((e, 1), lambda i: (0, 0))],
        out_specs=[idx_spec, idx_spec, pl.BlockSpec((tt, LANES), lambda i: (i, 0)),
                   pl.BlockSpec((e, LANES), lambda i: (0, 0))],
        out_shape=[jax.ShapeDtypeStruct((8, t), jnp.int32), jax.ShapeDtypeStruct((8, t), jnp.int32),
                   jax.ShapeDtypeStruct((t, LANES), F32), jax.ShapeDtypeStruct((e, LANES), F32)],
        scratch_shapes=[pltpu.VMEM((e, LANES), F32)],
        compiler_params=_cparams("arbitrary"),
        name="router",
    )(lg_t, router_bias.reshape(e, 1))


SLOT_STRIDE = 8
ROW_GROUP = 8


def _slot_kernel(pstart_ref, e_ref, r_ref, o_ref):
    e = e_ref[...]
    acc = r_ref[...]
    for x in range(N_EXPERTS):
        acc = acc + jnp.where(e == x, pstart_ref[x], 0)
    tt = acc.shape[1]
    pad = jnp.concatenate([acc, jnp.zeros((LANES - acc.shape[0], tt), jnp.int32)], axis=0)
    o_ref[...] = jnp.transpose(pad)[:, :SLOT_STRIDE]


def _slot_index(pstart, e_idx, rank, tt):
    rows, t = e_idx.shape
    spec = pl.BlockSpec((rows, tt), lambda i, ps: (0, i))
    slot = pl.pallas_call(
        _slot_kernel,
        grid_spec=pltpu.PrefetchScalarGridSpec(
            num_scalar_prefetch=1, grid=(t // tt,), in_specs=[spec, spec],
            out_specs=pl.BlockSpec((tt, SLOT_STRIDE), lambda i, ps: (i, 0))),
        out_shape=jax.ShapeDtypeStruct((t, SLOT_STRIDE), jnp.int32),
        compiler_params=_cparams("parallel"),
        name="slot_index",
    )(pstart, e_idx, rank)
    return slot.reshape(t * SLOT_STRIDE)


def _row_copies(n_tokens, make_copy):
    def body(g, carry):
        base = g * (ROW_GROUP * SLOT_STRIDE)
        for s in range(ROW_GROUP):
            for j in range(TOP_K):
                make_copy(g, s, j, base + (s * SLOT_STRIDE + j)).start(priority=(s * TOP_K + j) % 2)
        return carry

    lax.fori_loop(0, n_tokens // ROW_GROUP, body, 0)


def _dispatch_kernel(pstart_ref, count_ref, slot_ref, hn_ref, xs_hbm, zero_ref, sem):
    td = hn_ref.shape[0] * ROW_GROUP
    bm = zero_ref.shape[0]

    def zero_fill(e, wait):
        end = pstart_ref[e] + count_ref[e]
        aligned = pl.multiple_of((end + 7) // 8 * 8, 8)
        for i in range(7):
            @pl.when(end + i < aligned)
            def _():
                cp = pltpu.make_async_copy(zero_ref.at[pl.ds(0, 1)], xs_hbm.at[pl.ds(end + i, 1)], sem)
                cp.wait() if wait else cp.start()
        cp = pltpu.make_async_copy(zero_ref, xs_hbm.at[pl.ds(aligned, bm)], sem)
        cp.wait() if wait else cp.start()

    @pl.when(pl.program_id(0) == 0)
    def _():
        zero_ref[...] = jnp.zeros(zero_ref.shape, zero_ref.dtype)

        def start(e, carry):
            zero_fill(e, False)
            return carry

        def wait(e, carry):
            zero_fill(e, True)
            return carry

        lax.fori_loop(0, N_EXPERTS, start, 0)
        lax.fori_loop(0, N_EXPERTS, wait, 0)

        last = N_EXPERTS - 1
        tail = (pstart_ref[last] + count_ref[last] + bm - 1) // bm

        def tail_copy(b):
            return pltpu.make_async_copy(zero_ref, xs_hbm.at[pl.ds(pl.multiple_of(b * bm, bm), bm)], sem)

        def tail_start(b, carry):
            tail_copy(b).start()
            return carry

        def tail_wait(b, carry):
            tail_copy(b).wait()
            return carry

        lax.fori_loop(tail, xs_hbm.shape[0] // bm, tail_start, 0)
        lax.fori_loop(tail, xs_hbm.shape[0] // bm, tail_wait, 0)

    _row_copies(td, lambda g, s, j, i: pltpu.make_async_copy(
        hn_ref.at[g, pl.ds(s, 1)], xs_hbm.at[pl.ds(slot_ref[i], 1)], sem))
    pltpu.make_async_copy(xs_hbm.at[pl.ds(0, td * TOP_K)], xs_hbm.at[pl.ds(0, td * TOP_K)], sem).wait()


def _dispatch(pstart, counts, slot, hn, n_slots, td, bm):
    t, d = hn.shape
    return pl.pallas_call(
        _dispatch_kernel,
        grid_spec=pltpu.PrefetchScalarGridSpec(
            num_scalar_prefetch=2, grid=(t // td,),
            in_specs=[pl.BlockSpec((td * SLOT_STRIDE,), lambda i, ps, cn: (i,), memory_space=pltpu.SMEM),
                      pl.BlockSpec((td // ROW_GROUP, ROW_GROUP, d), lambda i, ps, cn: (i, 0, 0))],
            out_specs=pl.BlockSpec(memory_space=pl.ANY),
            scratch_shapes=[pltpu.VMEM((bm, d), hn.dtype), pltpu.SemaphoreType.DMA(())]),
        out_shape=jax.ShapeDtypeStruct((n_slots, d), hn.dtype),
        compiler_params=_cparams("arbitrary"),
        name="dispatch",
    )(pstart, counts, slot, hn.reshape(t // ROW_GROUP, ROW_GROUP, d))


EXPERT_SUB = 256


def _expert_kernel(bexp_ref, bsrc_ref, nused_ref, xs_ref, wg_ref, wu_ref, wd_ref, y_ref):
    del bexp_ref, bsrc_ref
    i = pl.program_id(0)

    @pl.when(i < nused_ref[0])
    def _():
        wg = wg_ref[0].astype(BF16)
        wu = wu_ref[0].astype(BF16)
        wd = wd_ref[0].astype(BF16)
        for r0 in range(0, xs_ref.shape[0], EXPERT_SUB):
            rows = slice(r0, r0 + EXPERT_SUB)
            xb = jnp.concatenate(_unpack_bf16_pairs(xs_ref[rows, :]), axis=1).astype(BF16)
            g = jnp.dot(xb, wg, preferred_element_type=F32)
            u = jnp.dot(xb, wu, preferred_element_type=F32)
            act = (_silu(g) * u).astype(BF16)
            y_ref[rows, :] = _pack_bf16_pairs(jnp.dot(act, wd, preferred_element_type=F32))

    @pl.when(i >= nused_ref[0])
    def _():
        y_ref[...] = jnp.zeros(y_ref.shape, y_ref.dtype)


def _experts(block_exp, block_src, n_used, xs, w_gate, w_up, w_down, bm):
    n_slots, d = xs.shape
    w_map = lambda i, be, bs, nu: (be[i], 0, 0)
    return pl.pallas_call(
        _expert_kernel,
        grid_spec=pltpu.PrefetchScalarGridSpec(
            num_scalar_prefetch=3, grid=(n_slots // bm,),
            in_specs=[pl.BlockSpec((bm, d), lambda i, be, bs, nu: (bs[i], 0)),
                      pl.BlockSpec((1,) + w_gate.shape[1:], w_map),
                      pl.BlockSpec((1,) + w_up.shape[1:], w_map),
                      pl.BlockSpec((1,) + w_down.shape[1:], w_map)],
            out_specs=pl.BlockSpec((bm, d), lambda i, be, bs, nu: (i, 0))),
        out_shape=jax.ShapeDtypeStruct((n_slots, d), xs.dtype),
        compiler_params=_cparams("arbitrary"),
        name="experts",
    )(block_exp, block_src, n_used, xs, w_gate, w_up, w_down)


def _combine_kernel(slot_ref, wt_ref, base_ref, gt2_ref, gfin_ref, y_hbm, o_ref, buf, sem):
    tg = base_ref.shape[0]
    _row_copies(tg, lambda g, s, j, i: pltpu.make_async_copy(
        y_hbm.at[pl.ds(slot_ref[i], 1)], buf.at[j, g, pl.ds(s, 1)], sem))
    pltpu.make_async_copy(buf, buf, sem).wait()
    wt = wt_ref[...]
    lo = hi = None
    for j in range(TOP_K):
        ylo, yhi = _unpack_bf16_pairs(buf[j].reshape(tg, buf.shape[-1]))
        w = wt[:, j:j + 1]
        lo = w * ylo if lo is None else lo + w * ylo
        hi = w * yhi if hi is None else hi + w * yhi
    xf = base_ref[...] + gt2_ref[0] * jnp.concatenate([lo, hi], axis=1)
    o_ref[...] = xf * lax.rsqrt(jnp.mean(xf * xf, axis=-1, keepdims=True) + EPS) * gfin_ref[...]


def _combine(slot, wt, base, gt2, g_final, y, seq, tg):
    t, d = base.shape
    tiles = seq // tg
    return pl.pallas_call(
        _combine_kernel,
        grid=(t // tg,),
        in_specs=[pl.BlockSpec((tg * SLOT_STRIDE,), lambda i: (i,), memory_space=pltpu.SMEM),
                  pl.BlockSpec((tg, LANES), lambda i: (i, 0)),
                  pl.BlockSpec((tg, d), lambda i: (i, 0)),
                  pl.BlockSpec((1, 1, d), lambda i: (i // tiles, 0, 0)),
                  pl.BlockSpec((1, d), lambda i: (0, 0)),
                  pl.BlockSpec(memory_space=pl.ANY)],
        out_specs=pl.BlockSpec((tg, d), lambda i: (i, 0)),
        out_shape=jax.ShapeDtypeStruct((t, d), F32),
        scratch_shapes=[pltpu.VMEM((TOP_K, tg // ROW_GROUP, ROW_GROUP, y.shape[1]), y.dtype),
                        pltpu.SemaphoreType.DMA(())],
        compiler_params=_cparams("arbitrary"),
        name="combine",
    )(slot, wt, base, gt2, g_final.reshape(1, d), y)


EXPERT_BLOCK = 512
ROW_TILE = 512
ROUTER_TILE = 512
GATHER_TILE = 512
SLOT_TILE = 4096


def _moe_plan(counts, n_tokens, bm):
    padded = (counts + bm - 1) // bm * bm
    pad_end = jnp.cumsum(padded)
    pad_start = (pad_end - padded).astype(jnp.int32)
    n_blocks = -(-(n_tokens * TOP_K) // bm) + N_EXPERTS + 1
    n_used = (pad_end[-1] // bm).astype(jnp.int32)
    block_src = jnp.minimum(jnp.arange(n_blocks, dtype=jnp.int32), jnp.maximum(n_used - 1, 0))
    block_exp = jnp.sum((pad_end[None, :] <= (block_src * bm)[:, None]).astype(jnp.int32), axis=1)
    block_exp = jnp.minimum(block_exp, N_EXPERTS - 1)
    return pad_start, block_exp, block_src, n_used.reshape(1), n_blocks * bm


def kernel(x, c, ctx, c_ctx, w_mod, b_mod, g_mix, g_ffn, w_in, w_out, dn_conv_w, dn_a_log, dn_dt_bias,
           dn_norm_g, cf_dw_w, cf_dw_b, cf_ln_g, cf_ln_b, router_w, router_bias, exp_w_gate, exp_w_up,
           exp_w_down, sh_w_gate, sh_w_up, sh_w_down, g_final):
    assert w_mod.shape[0] == 1, "single-layer block: the context stream is never re-read"
    bsz, seq, d = x.shape
    t = bsz * seq
    x2 = x.reshape(t, d)
    st = _mixer_front(x, c, ctx, c_ctx, w_mod[0], b_mod[0], g_mix[0], w_in[0], dn_conv_w[0], dn_a_log[0],
                      dn_dt_bias[0])
    _, _, gt1, sh2, sc2, gt2 = st['lat']
    cfo = _cf_conv(st['y'], cf_dw_w[0], cf_dw_b[0], cf_ln_g[0], cf_ln_b[0], bsz)
    wsgu = jnp.concatenate([sh_w_gate[0], sh_w_up[0]], axis=1).astype(BF16)
    base, hn, lg_t = _mixer_out(st['o_f'], st['o_b'], st['sz'], cfo, x2, gt1, sh2, sc2, gt2,
                                dn_norm_g[0].reshape(1, DN_D), w_out[0].astype(BF16), g_ffn[0].reshape(1, d),
                                router_w[0].T, wsgu, sh_w_down[0].astype(BF16), seq, min(seq, ROW_TILE))
    e_idx, rank, wt, cnt = _router(lg_t, router_bias[0], min(t, ROUTER_TILE))
    counts = cnt[:, 0].astype(jnp.int32)
    pstart, block_exp, block_src, n_used, n_slots = _moe_plan(counts, t, EXPERT_BLOCK)
    slot = _slot_index(pstart, e_idx, rank, min(t, SLOT_TILE))
    xs = _dispatch(pstart, counts, slot, hn, n_slots, min(t, GATHER_TILE), EXPERT_BLOCK)
    y = _experts(block_exp, block_src, n_used, xs, exp_w_gate[0], exp_w_up[0], exp_w_down[0], EXPERT_BLOCK)
    out = _combine(slot, wt, base, gt2, g_final, y, seq, min(seq, GATHER_TILE))
    return out.reshape(bsz, seq, d)
```

```python
import functools

import jax
import jax.numpy as jnp
from jax import lax
from jax.experimental import pallas as pl
from jax.experimental.pallas import tpu as pltpu

F32 = jnp.float32
U32 = jnp.uint32
BF16 = jnp.bfloat16
HIGHEST = lax.Precision.HIGHEST

EPS = 1e-6
LANES = 128
GRID_W = 64
DN_HEADS = 4
DN_D = 128
DN_QK = DN_HEADS * DN_D
DN_WIDTH = DN_HEADS * DN_D
DN_CONV_DIM = 2 * DN_QK + DN_WIDTH
SHORT_CONV = 7
CHUNK = 64
CF_CH = 512
CF_K = 31
N_EXPERTS = 64
TOP_K = 6
N_GROUPS = 8
GROUP_SIZE = N_EXPERTS // N_GROUPS
TOPK_GROUPS = 4
ROUTED_SCALE = 2.5
HALO = 8
VMEM_LIMIT = 56 * 1024 * 1024


def _cparams(*sem):
    return pltpu.CompilerParams(dimension_semantics=sem, vmem_limit_bytes=VMEM_LIMIT)


def _silu(v):
    return v * jax.nn.sigmoid(v)


def _bdot(a, b):
    return jnp.dot(a.astype(BF16), b.astype(BF16), preferred_element_type=F32)


def _pack_bf16_pairs(v):
    n = v.shape[1] // 2
    lo = lax.bitcast_convert_type(v[:, :n].astype(BF16).astype(F32), U32)
    hi = lax.bitcast_convert_type(v[:, n:].astype(BF16).astype(F32), U32)
    return (lo >> 16) | (hi & jnp.uint32(0xFFFF0000))


def _unpack_bf16_pairs(w):
    lo = lax.bitcast_convert_type(w << 16, F32)
    hi = lax.bitcast_convert_type(w & jnp.uint32(0xFFFF0000), F32)
    return lo, hi


def _mod_kernel(c_ref, w_ref, b_ref, o_ref):
    o_ref[...] = jnp.dot(_silu(c_ref[...]), w_ref[...], preferred_element_type=F32,
                         precision=HIGHEST) + b_ref[...]


def _modulation(cc, w_mod, b_mod):
    rows, d = cc.shape
    n = w_mod.shape[1]
    tn = 1024
    return pl.pallas_call(
        _mod_kernel,
        grid=(n // tn,),
        in_specs=[pl.BlockSpec((rows, d), lambda j: (0, 0)),
                  pl.BlockSpec((d, tn), lambda j: (0, j)),
                  pl.BlockSpec((1, tn), lambda j: (0, j))],
        out_specs=pl.BlockSpec((rows, tn), lambda j: (0, j)),
        out_shape=jax.ShapeDtypeStruct((rows, n), F32),
        compiler_params=_cparams("parallel"),
        name="modulation",
    )(cc, w_mod, b_mod.reshape(1, n))


def _chunk_cumsum(g, reverse):
    n = g.shape[0]
    pos = lax.broadcasted_iota(jnp.int32, g.shape, 0) % CHUNK
    s = 1
    while s < CHUNK:
        if reverse:
            shifted = pltpu.roll(g, n - s, 0)
            ok = pos < CHUNK - s
        else:
            shifted = pltpu.roll(g, s, 0)
            ok = pos >= s
        g = g + jnp.where(ok, shifted, 0.0)
        s *= 2
    return g


def _inproj_kernel(latent, x_ref, sh_ref, sc_ref, g_ref, wqkv_ref, wba_ref, alog_ref, dtb_ref, *rest):
    if latent:
        wz_ref, wcf_ref, qkv_o, bgc_o, bgct_o, sz_o, y_o = rest
    else:
        qkv_o, bgc_o, bgct_o = rest
    x = x_ref[...]
    xn = x * lax.rsqrt(jnp.mean(x * x, axis=-1, keepdims=True) + EPS) * g_ref[...]
    hb = (xn * (1.0 + sc_ref[0]) + sh_ref[0]).astype(BF16)
    qkv_o[...] = jnp.dot(hb, wqkv_ref[...], preferred_element_type=F32)

    ba = jnp.dot(hb, wba_ref[...], preferred_element_type=F32)
    col = lax.broadcasted_iota(jnp.int32, ba.shape, 1)
    is_beta = (col % 8) < DN_HEADS
    g = -jnp.exp(alog_ref[...]) * jax.nn.softplus(ba + dtb_ref[...])
    g = jnp.where(is_beta, 0.0, g)
    gc = jnp.where(col < 8, _chunk_cumsum(g, False), _chunk_cumsum(g, True))
    bgc = jnp.where(is_beta, jax.nn.sigmoid(ba), gc)
    bgc_o[...] = bgc
    for c in range(bgc.shape[0] // CHUNK):
        bgct_o[c] = jnp.transpose(bgc[c * CHUNK:(c + 1) * CHUNK, :])[:16, :]

    if latent:
        sz_o[...] = _silu(jnp.dot(hb, wz_ref[...], preferred_element_type=F32))
        cf = jnp.dot(hb, wcf_ref[...], preferred_element_type=F32)
        y_o[...] = cf[:, :CF_CH] * jax.nn.sigmoid(cf[:, CF_CH:])


def _in_projection(x2, sh, sc, g_mix, wqkv, wba, alog_row, dtb_row, wz, wcf, rows_per_mod, tm):
    t, d = x2.shape
    latent = wz is not None
    tiles_per_mod = rows_per_mod // tm
    const = lambda i: (0, 0)
    mod_map = lambda i: (i // tiles_per_mod, 0, 0)
    row_map = lambda i: (i, 0)
    in_specs = [pl.BlockSpec((tm, d), row_map),
                pl.BlockSpec((1, 1, d), mod_map), pl.BlockSpec((1, 1, d), mod_map),
                pl.BlockSpec((1, d), const),
                pl.BlockSpec(wqkv.shape, const), pl.BlockSpec(wba.shape, const),
                pl.BlockSpec((1, LANES), const), pl.BlockSpec((1, LANES), const)]
    args = [x2, sh, sc, g_mix, wqkv, wba, alog_row, dtb_row]
    out_specs = [pl.BlockSpec((tm, DN_CONV_DIM), row_map), pl.BlockSpec((tm, LANES), row_map),
                 pl.BlockSpec((tm // CHUNK, 16, CHUNK), lambda i: (i, 0, 0))]
    out_shape = [jax.ShapeDtypeStruct((t, DN_CONV_DIM), F32), jax.ShapeDtypeStruct((t, LANES), F32),
                 jax.ShapeDtypeStruct((t // CHUNK, 16, CHUNK), F32)]
    if latent:
        in_specs += [pl.BlockSpec(wz.shape, const), pl.BlockSpec(wcf.shape, const)]
        args += [wz, wcf]
        out_specs += [pl.BlockSpec((tm, DN_WIDTH), row_map), pl.BlockSpec((tm, CF_CH), row_map)]
        out_shape += [jax.ShapeDtypeStruct((t, DN_WIDTH), F32), jax.ShapeDtypeStruct((t, CF_CH), F32)]
    return pl.pallas_call(
        functools.partial(_inproj_kernel, latent),
        grid=(t // tm,),
        in_specs=in_specs, out_specs=out_specs, out_shape=out_shape,
        compiler_params=_cparams("parallel"),
        name="in_projection_latent" if latent else "in_projection_context",
    )(*args)


CONV_ROWS = 64


def _shortconv_kernel(tiles_per_seq, prev_ref, cur_ref, next_ref, w_ref, o_ref, ext_ref):
    i = pl.program_id(0)
    tm = cur_ref.shape[0]
    pos = i % tiles_per_seq
    ext_ref[0:HALO, :] = jnp.where(pos == 0, 0.0, prev_ref[...])
    ext_ref[HALO:HALO + tm, :] = cur_ref[...]
    ext_ref[HALO + tm:, :] = jnp.where(pos == tiles_per_seq - 1, 0.0, next_ref[...])
    reach = SHORT_CONV // 2

    for r0 in range(0, tm, CONV_ROWS):
        for cb in range(DN_CONV_DIM // LANES):
            cols = slice(cb * LANES, (cb + 1) * LANES)
            acc = jnp.zeros((CONV_ROWS, LANES), F32)
            for k in range(SHORT_CONV):
                s = r0 + HALO - reach + k
                acc = acc + w_ref[k:k + 1, cols] * ext_ref[s:s + CONV_ROWS, cols]
            a = _silu(acc)
            if cb < 2 * DN_HEADS:
                a = a * lax.rsqrt(jnp.sum(a * a, axis=-1, keepdims=True) + EPS)
            if cb < DN_HEADS:
                a = a * (DN_D ** -0.5)
            o_ref[r0:r0 + CONV_ROWS, cols] = a


def _short_conv(qkv, conv_w, seq_len, tm):
    t, c = qkv.shape
    tiles_per_seq = seq_len // tm
    hb = tm // HALO
    n_halo_blocks = t // HALO
    return pl.pallas_call(
        functools.partial(_shortconv_kernel, tiles_per_seq),
        grid=(t // tm,),
        in_specs=[pl.BlockSpec((HALO, c), lambda i: (jnp.maximum(i * hb - 1, 0), 0)),
                  pl.BlockSpec((tm, c), lambda i: (i, 0)),
                  pl.BlockSpec((HALO, c), lambda i: (jnp.minimum((i + 1) * hb, n_halo_blocks - 1), 0)),
                  pl.BlockSpec(conv_w.shape, lambda i: (0, 0))],
        out_specs=pl.BlockSpec((tm, c), lambda i: (i, 0)),
        out_shape=jax.ShapeDtypeStruct((t, c), F32),
        scratch_shapes=[pltpu.VMEM((tm + 2 * HALO, c), F32)],
        compiler_params=_cparams("parallel"),
        name="short_conv",
    )(qkv, qkv, qkv, conv_w)


DELTA_STEPS = 2
INV_BLOCK = 16


def _delta_pre(refs, ci_f, ci_b, masks, blocks, eye):
    qkv_ref, bgc_ref, bgct_ref = refs
    chains = range(2 * DN_HEADS * len(ci_f))
    q, k, v, beta, gcol, decay, eg, gl = [], [], [], [], [], [], [], []
    for d, ci in [(d, c[step]) for step in range(len(ci_f)) for d, c in enumerate((ci_f, ci_b))]:
        rows = pl.ds(pl.multiple_of(ci * CHUNK, CHUNK), CHUNK)
        bg = bgc_ref[rows, :]
        bgt = bgct_ref[ci]
        for h in range(DN_HEADS):
            q.append(qkv_ref[rows, h * DN_D:(h + 1) * DN_D])
            k.append(qkv_ref[rows, DN_QK + h * DN_D:DN_QK + (h + 1) * DN_D])
            v.append(qkv_ref[rows, 2 * DN_QK + h * DN_D:2 * DN_QK + (h + 1) * DN_D])
            jb = d * 8 + h
            jg = d * 8 + DN_HEADS + h
            beta.append(bg[:, jb:jb + 1])
            gc = bg[:, jg:jg + 1]
            gcol.append(gc)
            decay.append(jnp.exp(jnp.where(masks[d][0], gc - bgt[jg:jg + 1, :], -jnp.inf)))
            eg.append(jnp.exp(gc))
            gl.append(gc[CHUNK - 1:CHUNK] if d == 0 else gc[0:1])

    contract_last = (((1,), (1,)), ((), ()))
    gram = [lax.dot_general(jnp.concatenate([k[n], q[n]], axis=0).astype(BF16), k[n].astype(BF16),
                            contract_last, preferred_element_type=F32) for n in chains]
    lmat = [jnp.where(masks[n // DN_HEADS % 2][1], gram[n][:CHUNK] * decay[n], 0.0) * beta[n] for n in chains]
    same_small, same_half = blocks
    dmat = [jnp.where(same_small, lmat[n], 0.0) for n in chains]
    tinv = [eye - dmat[n] for n in chains]
    p = [_bdot(dmat[n], dmat[n]) for n in chains]
    n_sq = 2
    while n_sq < INV_BLOCK // 2:
        x = [_bdot(jnp.concatenate([tinv[n], p[n]], axis=0), p[n]) for n in chains]
        tinv = [tinv[n] + x[n][:CHUNK] for n in chains]
        p = [x[n][CHUNK:] for n in chains]
        n_sq *= 2
    tinv = [tinv[n] + _bdot(tinv[n], p[n]) for n in chains]
    for off_mask in (jnp.logical_and(same_half, jnp.logical_not(same_small)), jnp.logical_not(same_half)):
        x = [_bdot(jnp.where(off_mask, lmat[n], 0.0), tinv[n]) for n in chains]
        tinv = [tinv[n] - _bdot(tinv[n], x[n]) for n in chains]
    uw = [_bdot(tinv[n], jnp.concatenate([v[n] * beta[n], k[n] * (beta[n] * eg[n])], axis=1)) for n in chains]

    return [(uw[n][:, :DN_D], uw[n][:, DN_D:], q[n] * eg[n], k[n] * jnp.exp(gl[n] - gcol[n]),
             gram[n][CHUNK:] * decay[n], jnp.exp(gl[n])) for n in chains]


def _delta_rec(pre, s_ref, want_out):
    chains = range(len(pre))
    contract_first = (((0,), (0,)), ((), ()))
    s = [s_ref[n] for n in chains]
    if want_out:
        wq = [_bdot(jnp.concatenate([pre[n][1], pre[n][2]], axis=0), s[n]) for n in chains]
        ws = [wq[n][:CHUNK] for n in chains]
    else:
        ws = [_bdot(pre[n][1], s[n]) for n in chains]
    vb = [(pre[n][0] - ws[n]).astype(BF16) for n in chains]
    for n in chains:
        s_ref[n] = s[n] * pre[n][5] + lax.dot_general(pre[n][3].astype(BF16), vb[n], contract_first,
                                                      preferred_element_type=F32)
    if not want_out:
        return None
    return [wq[n][CHUNK:] + jnp.dot(pre[n][4].astype(BF16), vb[n], preferred_element_type=F32)
            for n in chains]


def _delta_kernel(qc_ref, bc_ref, btc_ref, qx_ref, bx_ref, btx_ref, of_ref, ob_ref, s_ref, p_ref):
    s_ref[...] = jnp.zeros(s_ref.shape, F32)
    row = lax.broadcasted_iota(jnp.int32, (CHUNK, CHUNK), 0)
    col = lax.broadcasted_iota(jnp.int32, (CHUNK, CHUNK), 1)
    masks = ((row >= col, row > col), (row <= col, row < col))
    eye = (row == col).astype(F32)
    blocks = (row // INV_BLOCK == col // INV_BLOCK, row // (CHUNK // 2) == col // (CHUNK // 2))
    per_step = 2 * DN_HEADS
    n_chains = DELTA_STEPS * per_step

    def stage(pre):
        for n in range(n_chains):
            u, w, qe, kd, amat, egl = pre[n]
            for j, val in enumerate((u, w, qe, kd)):
                p_ref[n, j] = val
            p_ref[n, 4] = jnp.concatenate([amat, jnp.broadcast_to(egl, (CHUNK, DN_D - CHUNK))], axis=1)

    def staged():
        return [(p_ref[n, 0], p_ref[n, 1], p_ref[n, 2], p_ref[n, 3], p_ref[n, 4][:, :CHUNK],
                 p_ref[n, 4][0:1, CHUNK:CHUNK + 1]) for n in range(n_chains)]

    def scan(refs, want_out):
        n_chunks = refs[0].shape[0] // CHUNK
        assert n_chunks % DELTA_STEPS == 0

        def pre(first):
            fwd = [first + s for s in range(DELTA_STEPS)]
            return _delta_pre(refs, fwd, [n_chunks - 1 - c for c in fwd], masks, blocks, eye)

        stage(pre(0))

        def body(it, carry):
            i = it * DELTA_STEPS
            cur = staged()
            new = pre(jnp.minimum(i + DELTA_STEPS, n_chunks - DELTA_STEPS))
            for s in range(DELTA_STEPS):
                o = _delta_rec(cur[s * per_step:(s + 1) * per_step], s_ref, want_out)
                if want_out:
                    rf = pl.ds(pl.multiple_of((i + s) * CHUNK, CHUNK), CHUNK)
                    rb = pl.ds(pl.multiple_of((n_chunks - 1 - i - s) * CHUNK, CHUNK), CHUNK)
                    for h in range(DN_HEADS):
                        cols = slice(h * DN_D, (h + 1) * DN_D)
                        of_ref[rf, cols] = o[h]
                        ob_ref[rb, cols] = o[DN_HEADS + h]
            stage(new)
            return carry

        lax.fori_loop(0, n_chunks // DELTA_STEPS, body, 0)

    scan((qc_ref, bc_ref, btc_ref), False)
    scan((qx_ref, bx_ref, btx_ref), True)


def _delta_scan(qkv_c, bgc_c, bgct_c, qkv_x, bgc_x, bgct_x, bsz):
    lc = qkv_c.shape[0] // bsz
    lx = qkv_x.shape[0] // bsz
    row_map = lambda b: (b, 0)
    out = jax.ShapeDtypeStruct((bsz * lx, DN_WIDTH), F32)
    return pl.pallas_call(
        _delta_kernel,
        grid=(bsz,),
        in_specs=[pl.BlockSpec((lc, DN_CONV_DIM), row_map), pl.BlockSpec((lc, LANES), row_map),
                  pl.BlockSpec((lc // CHUNK, 16, CHUNK), lambda b: (b, 0, 0)),
                  pl.BlockSpec((lx, DN_CONV_DIM), row_map), pl.BlockSpec((lx, LANES), row_map),
                  pl.BlockSpec((lx // CHUNK, 16, CHUNK), lambda b: (b, 0, 0))],
        out_specs=[pl.BlockSpec((lx, DN_WIDTH), row_map), pl.BlockSpec((lx, DN_WIDTH), row_map)],
        out_shape=[out, out],
        scratch_shapes=[pltpu.VMEM((2 * DN_HEADS, DN_D, DN_D), F32),
                        pltpu.VMEM((DELTA_STEPS * 2 * DN_HEADS, 5, CHUNK, DN_D), F32)],
        compiler_params=_cparams("parallel"),
        name="delta_scan",
    )(qkv_c, bgc_c, bgct_c, qkv_x, bgc_x, bgct_x)


def _pad_rows(a, rows):
    return jnp.pad(a, ((0, rows - a.shape[0]), (0, 0)))


def _mixer_front(x, c, ctx, c_ctx, w_mod, b_mod, g_mix, w_in, conv_w, a_log, dt_bias):
    bsz, seq, d = x.shape
    lc = ctx.shape[1]
    off_z = DN_CONV_DIM
    off_ba = off_z + DN_WIDTH
    off_cf = off_ba + 4 * DN_HEADS

    cc = _pad_rows(jnp.concatenate([c, c_ctx[None, :]], axis=0), -(-(bsz + 1) // 8) * 8)
    mod = _modulation(cc, w_mod, b_mod)
    mods = [mod[:, j * d:(j + 1) * d] for j in range(6)]
    lat = [m[:bsz].reshape(bsz, 1, d) for m in mods]
    con = [m[bsz:bsz + 1].reshape(1, 1, d) for m in mods]

    wqkv = w_in[:, :off_z].astype(BF16)
    wz = w_in[:, off_z:off_ba].astype(BF16)
    wba = jnp.pad(w_in[:, off_ba:off_cf], ((0, 0), (0, LANES - 4 * DN_HEADS))).astype(BF16)
    wcf = w_in[:, off_cf:].astype(BF16)
    zeros4 = jnp.zeros((2, DN_HEADS), F32)
    alog_row = jnp.pad(jnp.concatenate([zeros4, a_log], axis=1).reshape(1, -1), ((0, 0), (0, LANES - 16)))
    dtb_row = jnp.pad(jnp.concatenate([zeros4, dt_bias], axis=1).reshape(1, -1), ((0, 0), (0, LANES - 16)))
    g_row = g_mix.reshape(1, d)

    qkv_c, bgc_c, bgct_c = _in_projection(ctx.reshape(bsz * lc, d), con[0], con[1], g_row, wqkv, wba,
                                          alog_row, dtb_row, None, None, bsz * lc, min(lc, 512))
    qkv_x, bgc_x, bgct_x, sz, y = _in_projection(x.reshape(bsz * seq, d), lat[0], lat[1], g_row, wqkv, wba,
                                                 alog_row, dtb_row, wz, wcf, seq, min(seq, 512))
    qkvn_c = _short_conv(qkv_c, conv_w, lc, min(lc, 512))
    qkvn_x = _short_conv(qkv_x, conv_w, seq, min(seq, 512))
    o_f, o_b = _delta_scan(qkvn_c, bgc_c, bgct_c, qkvn_x, bgc_x, bgct_x, bsz)
    return dict(mod=mod, lat=lat, qkv_x=qkv_x, qkv_c=qkv_c, sz=sz, y=y, qkvn_x=qkvn_x, bgc_x=bgc_x,
                bgct_x=bgct_x, o_f=o_f, o_b=o_b)


CF_ROWS = 32


def _cfconv_kernel(y_ref, w_ref, b_ref, lng_ref, lnb_ref, o_ref, pad_ref):
    n = y_ref.shape[0]
    reach = (CF_K // 2) * GRID_W
    pad_ref[0:reach, :] = jnp.zeros((reach, CF_CH), F32)
    pad_ref[reach:reach + n, :] = y_ref[...]
    pad_ref[reach + n:, :] = jnp.zeros((reach, CF_CH), F32)

    def body(r, carry):
        r0 = pl.multiple_of(r * CF_ROWS, CF_ROWS)
        acc = jnp.zeros((CF_ROWS, CF_CH), F32)
        for k in range(CF_K):
            acc = acc + w_ref[k:k + 1, :] * pad_ref[pl.ds(r0 + k * GRID_W, CF_ROWS), :]
        acc = acc + b_ref[...]
        mu = jnp.mean(acc, axis=-1, keepdims=True)
        xc = acc - mu
        var = jnp.mean(xc * xc, axis=-1, keepdims=True)
        o_ref[pl.ds(r0, CF_ROWS), :] = _silu(xc * lax.rsqrt(var + EPS) * lng_ref[...] + lnb_ref[...])
        return carry

    lax.fori_loop(0, n // CF_ROWS, body, 0, unroll=4)


def _cf_conv(y, dw_w, dw_b, ln_g, ln_b, bsz):
    t, ch = y.shape
    n = t // bsz
    reach = (CF_K // 2) * GRID_W
    const = lambda b: (0, 0)
    return pl.pallas_call(
        _cfconv_kernel,
        grid=(bsz,),
        in_specs=[pl.BlockSpec((n, ch), lambda b: (b, 0)), pl.BlockSpec(dw_w.shape, const),
                  pl.BlockSpec((1, ch), const), pl.BlockSpec((1, ch), const), pl.BlockSpec((1, ch), const)],
        out_specs=pl.BlockSpec((n, ch), lambda b: (b, 0)),
        out_shape=jax.ShapeDtypeStruct((t, ch), F32),
        scratch_shapes=[pltpu.VMEM((n + 2 * reach, ch), F32)],
        compiler_params=_cparams("parallel"),
        name="conformer_conv",
    )(y, dw_w, dw_b.reshape(1, ch), ln_g.reshape(1, ch), ln_b.reshape(1, ch))


def _mixout_kernel(of_ref, ob_ref, sz_ref, cf_ref, x_ref, gt1_ref, sh2_ref, sc2_ref, gt2_ref, ng_ref, wo_ref,
                   gffn_ref, rwt_ref, wsgu_ref, wsd_ref, base_o, hn_o, lg_o):
    contract_last = (((1,), (1,)), ((), ()))
    rw = rwt_ref[...]
    rw_hi = rw.astype(BF16)
    rw_lo = (rw - rw_hi.astype(F32)).astype(BF16)
    rw_both = jnp.concatenate([rw_hi, rw_lo], axis=0)
    tm = x_ref.shape[0]
    for r0 in range(0, tm, tm // 2):
        rows = slice(r0, r0 + tm // 2)
        o = of_ref[rows, :] + ob_ref[rows, :]
        parts = []
        for h in range(DN_HEADS):
            oh = o[:, h * DN_D:(h + 1) * DN_D]
            parts.append(oh * lax.rsqrt(jnp.mean(oh * oh, axis=-1, keepdims=True) + EPS) * ng_ref[...])
        dn = jnp.concatenate(parts, axis=1) * sz_ref[rows, :]
        heads = jnp.concatenate([dn, cf_ref[rows, :]], axis=1).astype(BF16)
        x1 = x_ref[rows, :] + gt1_ref[0] * jnp.dot(heads, wo_ref[...], preferred_element_type=F32)
        hn = (x1 * lax.rsqrt(jnp.mean(x1 * x1, axis=-1, keepdims=True) + EPS) * gffn_ref[...]
              * (1.0 + sc2_ref[0]) + sh2_ref[0])
        hn_o[rows, :] = _pack_bf16_pairs(hn)
        hn_hi = hn.astype(BF16)
        hn_lo = (hn - hn_hi.astype(F32)).astype(BF16)
        both = lax.dot_general(rw_both, hn_hi, contract_last, preferred_element_type=F32)
        lg_o[:, rows] = (both[:N_EXPERTS] + both[N_EXPERTS:]
                         + lax.dot_general(rw_hi, hn_lo, contract_last, preferred_element_type=F32))
        gu = jnp.dot(hn_hi, wsgu_ref[...], preferred_element_type=F32)
        ds = gu.shape[1] // 2
        act = (_silu(gu[:, :ds]) * gu[:, ds:]).astype(BF16)
        base_o[rows, :] = x1 + gt2_ref[0] * jnp.dot(act, wsd_ref[...], preferred_element_type=F32)


def _mixer_out(o_f, o_b, sz, cfo, x2, gt1, sh2, sc2, gt2, norm_g, w_out, g_ffn, rwt, wsgu, wsd, seq, tm):
    t, d = x2.shape
    tiles = seq // tm
    const = lambda i: (0, 0)
    row_map = lambda i: (i, 0)
    mod_map = lambda i: (i // tiles, 0, 0)
    half = pl.BlockSpec((tm, DN_WIDTH), row_map)
    mod_spec = pl.BlockSpec((1, 1, d), mod_map)
    return pl.pallas_call(
        _mixout_kernel,
        grid=(t // tm,),
        in_specs=[half, half, half, half, pl.BlockSpec((tm, d), row_map),
                  mod_spec, mod_spec, mod_spec, mod_spec,
                  pl.BlockSpec((1, DN_D), const), pl.BlockSpec(w_out.shape, const), pl.BlockSpec((1, d), const),
                  pl.BlockSpec(rwt.shape, const), pl.BlockSpec(wsgu.shape, const), pl.BlockSpec(wsd.shape, const)],
        out_specs=[pl.BlockSpec((tm, d), row_map), pl.BlockSpec((tm, d // 2), row_map),
                   pl.BlockSpec((N_EXPERTS, tm), lambda i: (0, i))],
        out_shape=[jax.ShapeDtypeStruct((t, d), F32), jax.ShapeDtypeStruct((t, d // 2), U32),
                   jax.ShapeDtypeStruct((N_EXPERTS, t), F32)],
        compiler_params=_cparams("parallel"),
        name="mixer_out",
    )(o_f, o_b, sz, cfo, x2, gt1, sh2, sc2, gt2, norm_g, w_out, g_ffn, rwt, wsgu, wsd)


def _first_argmax(vals, idx, sentinel):
    m = jnp.max(vals, axis=0, keepdims=True)
    return m, jnp.min(jnp.where(vals == m, idx, sentinel), axis=0, keepdims=True)


def _router_kernel(lg_ref, bias_ref, e_o, r_o, wt_o, cnt_o, carry_ref):
    @pl.when(pl.program_id(0) == 0)
    def _():
        carry_ref[...] = jnp.zeros(carry_ref.shape, F32)

    tt = lg_ref.shape[1]
    neg = -jnp.inf
    scores = jax.nn.sigmoid(lg_ref[...])
    sel = scores + bias_ref[...]
    sub = lax.broadcasted_iota(jnp.int32, (GROUP_SIZE, tt), 0)

    rows = []
    for g in range(N_GROUPS):
        sg = sel[g * GROUP_SIZE:(g + 1) * GROUP_SIZE]
        m1, first = _first_argmax(sg, sub, GROUP_SIZE)
        m2 = jnp.max(jnp.where(sub == first, neg, sg), axis=0, keepdims=True)
        rows.append(m1 + m2)
    cur = jnp.concatenate(rows, axis=0)
    gidx = lax.broadcasted_iota(jnp.int32, (N_GROUPS, tt), 0)
    keep = gidx < 0
    for _ in range(TOPK_GROUPS):
        _, a = _first_argmax(cur, gidx, N_GROUPS)
        pick = gidx == a
        keep = jnp.logical_or(keep, pick)
        cur = jnp.where(pick, neg, cur)
    keep_e = jnp.concatenate([jnp.broadcast_to(keep[g:g + 1], (GROUP_SIZE, tt)) for g in range(N_GROUPS)],
                             axis=0)
    masked = jnp.where(keep_e, sel, neg)

    eidx = lax.broadcasted_iota(jnp.int32, (N_EXPERTS, tt), 0)
    e_rows, w_rows, picks = [], [], []
    for _ in range(TOP_K):
        _, a = _first_argmax(masked, eidx, N_EXPERTS)
        pick = eidx == a
        e_rows.append(a)
        w_rows.append(jnp.sum(jnp.where(pick, scores, 0.0), axis=0, keepdims=True))
        picks.append(pick)
        masked = jnp.where(pick, neg, masked)
    onehot = sum(p.astype(F32) for p in picks)
    scale = ROUTED_SCALE / sum(w_rows)

    tri = (lax.broadcasted_iota(jnp.int32, (tt, tt), 0) < lax.broadcasted_iota(jnp.int32, (tt, tt), 1))
    cum = jnp.dot(onehot.astype(BF16), tri.astype(BF16), preferred_element_type=F32) + carry_ref[:, 0:1]
    r_rows = [jnp.sum(jnp.where(p, cum, 0.0), axis=0, keepdims=True).astype(jnp.int32) for p in picks]
    carry_ref[...] = carry_ref[...] + jnp.sum(onehot, axis=1, keepdims=True)
    cnt_o[...] = carry_ref[...]

    fill = 8 - TOP_K
    e_o[...] = jnp.concatenate(e_rows + [jnp.zeros((fill, tt), jnp.int32)], axis=0)
    r_o[...] = jnp.concatenate(r_rows + [jnp.zeros((fill, tt), jnp.int32)], axis=0)
    w_pad = jnp.concatenate([w * scale for w in w_rows] + [jnp.zeros((LANES - TOP_K, tt), F32)], axis=0)
    wt_o[...] = jnp.transpose(w_pad)


def _router(lg_t, router_bias, tt):
    e, t = lg_t.shape
    idx_spec = pl.BlockSpec((8, tt), lambda i: (0, i))
    return pl.pallas_call(
        _router_kernel,
        grid=(t // tt,),
        in_specs=[pl.BlockSpec((e, tt), lambda i: (0, i)), pl.BlockSpec((e, 1), lambda i: (0, 0))],
        out_specs=[idx_spec, idx_spec, pl.BlockSpec((tt, LANES), lambda i: (i, 0)),
                   pl.BlockSpec((e, LANES), lambda i: (0, 0))],
        out_shape=[jax.ShapeDtypeStruct((8, t), jnp.int32), jax.ShapeDtypeStruct((8, t), jnp.int32),
                   jax.ShapeDtypeStruct((t, LANES), F32), jax.ShapeDtypeStruct((e, LANES), F32)],
        scratch_shapes=[pltpu.VMEM((e, LANES), F32)],
        compiler_params=_cparams("arbitrary"),
        name="router",
    )(lg_t, router_bias.reshape(e, 1))


SLOT_STRIDE = 8
ROW_GROUP = 8


def _slot_kernel(pstart_ref, e_ref, r_ref, o_ref):
    e = e_ref[...]
    acc = r_ref[...]
    for x in range(N_EXPERTS):
        acc = acc + jnp.where(e == x, pstart_ref[x], 0)
    tt = acc.shape[1]
    pad = jnp.concatenate([acc, jnp.zeros((LANES - acc.shape[0], tt), jnp.int32)], axis=0)
    o_ref[...] = jnp.transpose(pad)[:, :SLOT_STRIDE]


def _slot_index(pstart, e_idx, rank, tt):
    rows, t = e_idx.shape
    spec = pl.BlockSpec((rows, tt), lambda i, ps: (0, i))
    slot = pl.pallas_call(
        _slot_kernel,
        grid_spec=pltpu.PrefetchScalarGridSpec(
            num_scalar_prefetch=1, grid=(t // tt,), in_specs=[spec, spec],
            out_specs=pl.BlockSpec((tt, SLOT_STRIDE), lambda i, ps: (i, 0))),
        out_shape=jax.ShapeDtypeStruct((t, SLOT_STRIDE), jnp.int32),
        compiler_params=_cparams("parallel"),
        name="slot_index",
    )(pstart, e_idx, rank)
    return slot.reshape(t * SLOT_STRIDE)


def _row_copies(n_tokens, make_copy):
    def body(g, carry):
        base = g * (ROW_GROUP * SLOT_STRIDE)
        for s in range(ROW_GROUP):
            for j in range(TOP_K):
                make_copy(g, s, j, base + (s * SLOT_STRIDE + j)).start(priority=(s * TOP_K + j) % 2)
        return carry

    lax.fori_loop(0, n_tokens // ROW_GROUP, body, 0)


def _dispatch_kernel(pstart_ref, count_ref, slot_ref, hn_ref, xs_hbm, zero_ref, sem):
    td = hn_ref.shape[0] * ROW_GROUP
    bm = zero_ref.shape[0]

    def zero_fill(e, wait):
        end = pstart_ref[e] + count_ref[e]
        aligned = pl.multiple_of((end + 7) // 8 * 8, 8)
        for i in range(7):
            @pl.when(end + i < aligned)
            def _():
                cp = pltpu.make_async_copy(zero_ref.at[pl.ds(0, 1)], xs_hbm.at[pl.ds(end + i, 1)], sem)
                cp.wait() if wait else cp.start()
        cp = pltpu.make_async_copy(zero_ref, xs_hbm.at[pl.ds(aligned, bm)], sem)
        cp.wait() if wait else cp.start()

    @pl.when(pl.program_id(0) == 0)
    def _():
        zero_ref[...] = jnp.zeros(zero_ref.shape, zero_ref.dtype)

        def start(e, carry):
            zero_fill(e, False)
            return carry

        def wait(e, carry):
            zero_fill(e, True)
            return carry

        lax.fori_loop(0, N_EXPERTS, start, 0)
        lax.fori_loop(0, N_EXPERTS, wait, 0)

        last = N_EXPERTS - 1
        tail = (pstart_ref[last] + count_ref[last] + bm - 1) // bm

        def tail_copy(b):
            return pltpu.make_async_copy(zero_ref, xs_hbm.at[pl.ds(pl.multiple_of(b * bm, bm), bm)], sem)

        def tail_start(b, carry):
            tail_copy(b).start()
            return carry

        def tail_wait(b, carry):
            tail_copy(b).wait()
            return carry

        lax.fori_loop(tail, xs_hbm.shape[0] // bm, tail_start, 0)
        lax.fori_loop(tail, xs_hbm.shape[0] // bm, tail_wait, 0)

    _row_copies(td, lambda g, s, j, i: pltpu.make_async_copy(
        hn_ref.at[g, pl.ds(s, 1)], xs_hbm.at[pl.ds(slot_ref[i], 1)], sem))
    pltpu.make_async_copy(xs_hbm.at[pl.ds(0, td * TOP_K)], xs_hbm.at[pl.ds(0, td * TOP_K)], sem).wait()


def _dispatch(pstart, counts, slot, hn, n_slots, td, bm):
    t, d = hn.shape
    return pl.pallas_call(
        _dispatch_kernel,
        grid_spec=pltpu.PrefetchScalarGridSpec(
            num_scalar_prefetch=2, grid=(t // td,),
            in_specs=[pl.BlockSpec((td * SLOT_STRIDE,), lambda i, ps, cn: (i,), memory_space=pltpu.SMEM),
                      pl.BlockSpec((td // ROW_GROUP, ROW_GROUP, d), lambda i, ps, cn: (i, 0, 0))],
            out_specs=pl.BlockSpec(memory_space=pl.ANY),
            scratch_shapes=[pltpu.VMEM((bm, d), hn.dtype), pltpu.SemaphoreType.DMA(())]),
        out_shape=jax.ShapeDtypeStruct((n_slots, d), hn.dtype),
        compiler_params=_cparams("arbitrary"),
        name="dispatch",
    )(pstart, counts, slot, hn.reshape(t // ROW_GROUP, ROW_GROUP, d))


EXPERT_SUB = 256


def _expert_kernel(bexp_ref, bsrc_ref, nused_ref, xs_ref, wg_ref, wu_ref, wd_ref, y_ref):
    del bexp_ref, bsrc_ref
    i = pl.program_id(0)

    @pl.when(i < nused_ref[0])
    def _():
        wg = wg_ref[0].astype(BF16)
        wu = wu_ref[0].astype(BF16)
        wd = wd_ref[0].astype(BF16)
        for r0 in range(0, xs_ref.shape[0], EXPERT_SUB):
            rows = slice(r0, r0 + EXPERT_SUB)
            xb = jnp.concatenate(_unpack_bf16_pairs(xs_ref[rows, :]), axis=1).astype(BF16)
            g = jnp.dot(xb, wg, preferred_element_type=F32)
            u = jnp.dot(xb, wu, preferred_element_type=F32)
            act = (_silu(g) * u).astype(BF16)
            y_ref[rows, :] = _pack_bf16_pairs(jnp.dot(act, wd, preferred_element_type=F32))

    @pl.when(i >= nused_ref[0])
    def _():
        y_ref[...] = jnp.zeros(y_ref.shape, y_ref.dtype)


def _experts(block_exp, block_src, n_used, xs, w_gate, w_up, w_down, bm):
    n_slots, d = xs.shape
    w_map = lambda i, be, bs, nu: (be[i], 0, 0)
    return pl.pallas_call(
        _expert_kernel,
        grid_spec=pltpu.PrefetchScalarGridSpec(
            num_scalar_prefetch=3, grid=(n_slots // bm,),
            in_specs=[pl.BlockSpec((bm, d), lambda i, be, bs, nu: (bs[i], 0)),
                      pl.BlockSpec((1,) + w_gate.shape[1:], w_map),
                      pl.BlockSpec((1,) + w_up.shape[1:], w_map),
                      pl.BlockSpec((1,) + w_down.shape[1:], w_map)],
            out_specs=pl.BlockSpec((bm, d), lambda i, be, bs, nu: (i, 0))),
        out_shape=jax.ShapeDtypeStruct((n_slots, d), xs.dtype),
        compiler_params=_cparams("arbitrary"),
        name="experts",
    )(block_exp, block_src, n_used, xs, w_gate, w_up, w_down)


def _combine_kernel(slot_ref, wt_ref, base_ref, gt2_ref, gfin_ref, y_hbm, o_ref, buf, sem):
    tg = base_ref.shape[0]
    _row_copies(tg, lambda g, s, j, i: pltpu.make_async_copy(
        y_hbm.at[pl.ds(slot_ref[i], 1)], buf.at[j, g, pl.ds(s, 1)], sem))
    pltpu.make_async_copy(buf, buf, sem).wait()
    wt = wt_ref[...]
    lo = hi = None
    for j in range(TOP_K):
        ylo, yhi = _unpack_bf16_pairs(buf[j].reshape(tg, buf.shape[-1]))
        w = wt[:, j:j + 1]
        lo = w * ylo if lo is None else lo + w * ylo
        hi = w * yhi if hi is None else hi + w * yhi
    xf = base_ref[...] + gt2_ref[0] * jnp.concatenate([lo, hi], axis=1)
    o_ref[...] = xf * lax.rsqrt(jnp.mean(xf * xf, axis=-1, keepdims=True) + EPS) * gfin_ref[...]


def _combine(slot, wt, base, gt2, g_final, y, seq, tg):
    t, d = base.shape
    tiles = seq // tg
    return pl.pallas_call(
        _combine_kernel,
        grid=(t // tg,),
        in_specs=[pl.BlockSpec((tg * SLOT_STRIDE,), lambda i: (i,), memory_space=pltpu.SMEM),
                  pl.BlockSpec((tg, LANES), lambda i: (i, 0)),
                  pl.BlockSpec((tg, d), lambda i: (i, 0)),
                  pl.BlockSpec((1, 1, d), lambda i: (i // tiles, 0, 0)),
                  pl.BlockSpec((1, d), lambda i: (0, 0)),
                  pl.BlockSpec(memory_space=pl.ANY)],
        out_specs=pl.BlockSpec((tg, d), lambda i: (i, 0)),
        out_shape=jax.ShapeDtypeStruct((t, d), F32),
        scratch_shapes=[pltpu.VMEM((TOP_K, tg // ROW_GROUP, ROW_GROUP, y.shape[1]), y.dtype),
                        pltpu.SemaphoreType.DMA(())],
        compiler_params=_cparams("arbitrary"),
        name="combine",
    )(slot, wt, base, gt2, g_final.reshape(1, d), y)


EXPERT_BLOCK = 512
ROW_TILE = 512
ROUTER_TILE = 512
GATHER_TILE = 512
SLOT_TILE = 4096


def _moe_plan(counts, n_tokens, bm):
    padded = (counts + bm - 1) // bm * bm
    pad_end = jnp.cumsum(padded)
    pad_start = (pad_end - padded).astype(jnp.int32)
    n_blocks = -(-(n_tokens * TOP_K) // bm) + N_EXPERTS + 1
    n_used = (pad_end[-1] // bm).astype(jnp.int32)
    block_src = jnp.minimum(jnp.arange(n_blocks, dtype=jnp.int32), jnp.maximum(n_used - 1, 0))
    block_exp = jnp.sum((pad_end[None, :] <= (block_src * bm)[:, None]).astype(jnp.int32), axis=1)
    block_exp = jnp.minimum(block_exp, N_EXPERTS - 1)
    return pad_start, block_exp, block_src, n_used.reshape(1), n_blocks * bm


def kernel(x, c, ctx, c_ctx, w_mod, b_mod, g_mix, g_ffn, w_in, w_out, dn_conv_w, dn_a_log, dn_dt_bias,
           dn_norm_g, cf_dw_w, cf_dw_b, cf_ln_g, cf_ln_b, router_w, router_bias, exp_w_gate, exp_w_up,
           exp_w_down, sh_w_gate, sh_w_up, sh_w_down, g_final):
    assert w_mod.shape[0] == 1, "single-layer block: the context stream is never re-read"
    bsz, seq, d = x.shape
    t = bsz * seq
    x2 = x.reshape(t, d)
    st = _mixer_front(x, c, ctx, c_ctx, w_mod[0], b_mod[0], g_mix[0], w_in[0], dn_conv_w[0], dn_a_log[0],
                      dn_dt_bias[0])
    _, _, gt1, sh2, sc2, gt2 = st['lat']
    cfo = _cf_conv(st['y'], cf_dw_w[0], cf_dw_b[0], cf_ln_g[0], cf_ln_b[0], bsz)
    wsgu = jnp.concatenate([sh_w_gate[0], sh_w_up[0]], axis=1).astype(BF16)
    base, hn, lg_t = _mixer_out(st['o_f'], st['o_b'], st['sz'], cfo, x2, gt1, sh2, sc2, gt2,
                                dn_norm_g[0].reshape(1, DN_D), w_out[0].astype(BF16), g_ffn[0].reshape(1, d),
                                router_w[0].T, wsgu, sh_w_down[0].astype(BF16), seq, min(seq, ROW_TILE))
    e_idx, rank, wt, cnt = _router(lg_t, router_bias[0], min(t, ROUTER_TILE))
    counts = cnt[:, 0].astype(jnp.int32)
    pstart, block_exp, block_src, n_used, n_slots = _moe_plan(counts, t, EXPERT_BLOCK)
    slot = _slot_index(pstart, e_idx, rank, min(t, SLOT_TILE))
    xs = _dispatch(pstart, counts, slot, hn, n_slots, min(t, GATHER_TILE), EXPERT_BLOCK)
    y = _experts(block_exp, block_src, n_used, xs, exp_w_gate[0], exp_w_up[0], exp_w_down[0], EXPERT_BLOCK)
    out = _combine(slot, wt, base, gt2, g_final, y, seq, min(seq, GATHER_TILE))
    return out.reshape(bsz, seq, d)
```

```python
import functools

import jax
import jax.numpy as jnp
from jax import lax
from jax.experimental import pallas as pl
from jax.experimental.pallas import tpu as pltpu

F32 = jnp.float32
U32 = jnp.uint32
BF16 = jnp.bfloat16
HIGHEST = lax.Precision.HIGHEST

EPS = 1e-6
LANES = 128
GRID_W = 64
DN_HEADS = 4
DN_D = 128
DN_QK = DN_HEADS * DN_D
DN_WIDTH = DN_HEADS * DN_D
DN_CONV_DIM = 2 * DN_QK + DN_WIDTH
SHORT_CONV = 7
CHUNK = 64
CF_CH = 512
CF_K = 31
N_EXPERTS = 64
TOP_K = 6
N_GROUPS = 8
GROUP_SIZE = N_EXPERTS // N_GROUPS
TOPK_GROUPS = 4
ROUTED_SCALE = 2.5
HALO = 8
VMEM_LIMIT = 56 * 1024 * 1024


def _cparams(*sem):
    return pltpu.CompilerParams(dimension_semantics=sem, vmem_limit_bytes=VMEM_LIMIT)


def _silu(v):
    return v * jax.nn.sigmoid(v)


def _bdot(a, b):
    return jnp.dot(a.astype(BF16), b.astype(BF16), preferred_element_type=F32)


def _pack_bf16_pairs(v):
    n = v.shape[1] // 2
    lo = lax.bitcast_convert_type(v[:, :n].astype(BF16).astype(F32), U32)
    hi = lax.bitcast_convert_type(v[:, n:].astype(BF16).astype(F32), U32)
    return (lo >> 16) | (hi & jnp.uint32(0xFFFF0000))


def _unpack_bf16_pairs(w):
    lo = lax.bitcast_convert_type(w << 16, F32)
    hi = lax.bitcast_convert_type(w & jnp.uint32(0xFFFF0000), F32)
    return lo, hi


def _mod_kernel(c_ref, w_ref, b_ref, o_ref):
    o_ref[...] = jnp.dot(_silu(c_ref[...]), w_ref[...], preferred_element_type=F32,
                         precision=HIGHEST) + b_ref[...]


def _modulation(cc, w_mod, b_mod):
    rows, d = cc.shape
    n = w_mod.shape[1]
    tn = 1024
    return pl.pallas_call(
        _mod_kernel,
        grid=(n // tn,),
        in_specs=[pl.BlockSpec((rows, d), lambda j: (0, 0)),
                  pl.BlockSpec((d, tn), lambda j: (0, j)),
                  pl.BlockSpec((1, tn), lambda j: (0, j))],
        out_specs=pl.BlockSpec((rows, tn), lambda j: (0, j)),
        out_shape=jax.ShapeDtypeStruct((rows, n), F32),
        compiler_params=_cparams("parallel"),
        name="modulation",
    )(cc, w_mod, b_mod.reshape(1, n))


def _chunk_cumsum(g, reverse):
    n = g.shape[0]
    pos = lax.broadcasted_iota(jnp.int32, g.shape, 0) % CHUNK
    s = 1
    while s < CHUNK:
        if reverse:
            shifted = pltpu.roll(g, n - s, 0)
            ok = pos < CHUNK - s
        else:
            shifted = pltpu.roll(g, s, 0)
            ok = pos >= s
        g = g + jnp.where(ok, shifted, 0.0)
        s *= 2
    return g


def _inproj_kernel(latent, x_ref, sh_ref, sc_ref, g_ref, wqkv_ref, wba_ref, alog_ref, dtb_ref, *rest):
    if latent:
        wz_ref, wcf_ref, qkv_o, bgc_o, bgct_o, sz_o, y_o = rest
    else:
        qkv_o, bgc_o, bgct_o = rest
    x = x_ref[...]
    xn = x * lax.rsqrt(jnp.mean(x * x, axis=-1, keepdims=True) + EPS) * g_ref[...]
    hb = (xn * (1.0 + sc_ref[0]) + sh_ref[0]).astype(BF16)
    qkv_o[...] = jnp.dot(hb, wqkv_ref[...], preferred_element_type=F32)

    ba = jnp.dot(hb, wba_ref[...], preferred_element_type=F32)
    col = lax.broadcasted_iota(jnp.int32, ba.shape, 1)
    is_beta = (col % 8) < DN_HEADS
    g = -jnp.exp(alog_ref[...]) * jax.nn.softplus(ba + dtb_ref[...])
    g = jnp.where(is_beta, 0.0, g)
    gc = jnp.where(col < 8, _chunk_cumsum(g, False), _chunk_cumsum(g, True))
    bgc = jnp.where(is_beta, jax.nn.sigmoid(ba), gc)
    bgc_o[...] = bgc
    for c in range(bgc.shape[0] // CHUNK):
        bgct_o[c] = jnp.transpose(bgc[c * CHUNK:(c + 1) * CHUNK, :])[:16, :]

    if latent:
        sz_o[...] = _silu(jnp.dot(hb, wz_ref[...], preferred_element_type=F32))
        cf = jnp.dot(hb, wcf_ref[...], preferred_element_type=F32)
        y_o[...] = cf[:, :CF_CH] * jax.nn.sigmoid(cf[:, CF_CH:])


def _in_projection(x2, sh, sc, g_mix, wqkv, wba, alog_row, dtb_row, wz, wcf, rows_per_mod, tm):
    t, d = x2.shape
    latent = wz is not None
    tiles_per_mod = rows_per_mod // tm
    const = lambda i: (0, 0)
    mod_map = lambda i: (i // tiles_per_mod, 0, 0)
    row_map = lambda i: (i, 0)
    in_specs = [pl.BlockSpec((tm, d), row_map),
                pl.BlockSpec((1, 1, d), mod_map), pl.BlockSpec((1, 1, d), mod_map),
                pl.BlockSpec((1, d), const),
                pl.BlockSpec(wqkv.shape, const), pl.BlockSpec(wba.shape, const),
                pl.BlockSpec((1, LANES), const), pl.BlockSpec((1, LANES), const)]
    args = [x2, sh, sc, g_mix, wqkv, wba, alog_row, dtb_row]
    out_specs = [pl.BlockSpec((tm, DN_CONV_DIM), row_map), pl.BlockSpec((tm, LANES), row_map),
                 pl.BlockSpec((tm // CHUNK, 16, CHUNK), lambda i: (i, 0, 0))]
    out_shape = [jax.ShapeDtypeStruct((t, DN_CONV_DIM), F32), jax.ShapeDtypeStruct((t, LANES), F32),
                 jax.ShapeDtypeStruct((t // CHUNK, 16, CHUNK), F32)]
    if latent:
        in_specs += [pl.BlockSpec(wz.shape, const), pl.BlockSpec(wcf.shape, const)]
        args += [wz, wcf]
        out_specs += [pl.BlockSpec((tm, DN_WIDTH), row_map), pl.BlockSpec((tm, CF_CH), row_map)]
        out_shape += [jax.ShapeDtypeStruct((t, DN_WIDTH), F32), jax.ShapeDtypeStruct((t, CF_CH), F32)]
    return pl.pallas_call(
        functools.partial(_inproj_kernel, latent),
        grid=(t // tm,),
        in_specs=in_specs, out_specs=out_specs, out_shape=out_shape,
        compiler_params=_cparams("parallel"),
        name="in_projection_latent" if latent else "in_projection_context",
    )(*args)


CONV_ROWS = 64


def _shortconv_kernel(tiles_per_seq, prev_ref, cur_ref, next_ref, w_ref, o_ref, ext_ref):
    i = pl.program_id(0)
    tm = cur_ref.shape[0]
    pos = i % tiles_per_seq
    ext_ref[0:HALO, :] = jnp.where(pos == 0, 0.0, prev_ref[...])
    ext_ref[HALO:HALO + tm, :] = cur_ref[...]
    ext_ref[HALO + tm:, :] = jnp.where(pos == tiles_per_seq - 1, 0.0, next_ref[...])
    reach = SHORT_CONV // 2

    for r0 in range(0, tm, CONV_ROWS):
        for cb in range(DN_CONV_DIM // LANES):
            cols = slice(cb * LANES, (cb + 1) * LANES)
            acc = jnp.zeros((CONV_ROWS, LANES), F32)
            for k in range(SHORT_CONV):
                s = r0 + HALO - reach + k
                acc = acc + w_ref[k:k + 1, cols] * ext_ref[s:s + CONV_ROWS, cols]
            a = _silu(acc)
            if cb < 2 * DN_HEADS:
                a = a * lax.rsqrt(jnp.sum(a * a, axis=-1, keepdims=True) + EPS)
            if cb < DN_HEADS:
                a = a * (DN_D ** -0.5)
            o_ref[r0:r0 + CONV_ROWS, cols] = a


def _short_conv(qkv, conv_w, seq_len, tm):
    t, c = qkv.shape
    tiles_per_seq = seq_len // tm
    hb = tm // HALO
    n_halo_blocks = t // HALO
    return pl.pallas_call(
        functools.partial(_shortconv_kernel, tiles_per_seq),
        grid=(t // tm,),
        in_specs=[pl.BlockSpec((HALO, c), lambda i: (jnp.maximum(i * hb - 1, 0), 0)),
                  pl.BlockSpec((tm, c), lambda i: (i, 0)),
                  pl.BlockSpec((HALO, c), lambda i: (jnp.minimum((i + 1) * hb, n_halo_blocks - 1), 0)),
                  pl.BlockSpec(conv_w.shape, lambda i: (0, 0))],
        out_specs=pl.BlockSpec((tm, c), lambda i: (i, 0)),
        out_shape=jax.ShapeDtypeStruct((t, c), F32),
        scratch_shapes=[pltpu.VMEM((tm + 2 * HALO, c), F32)],
        compiler_params=_cparams("parallel"),
        name="short_conv",
    )(qkv, qkv, qkv, conv_w)


DELTA_STEPS = 2
INV_BLOCK = 16


def _delta_pre(refs, ci_f, ci_b, masks, blocks, eye):
    qkv_ref, bgc_ref, bgct_ref = refs
    chains = range(2 * DN_HEADS * len(ci_f))
    q, k, v, beta, gcol, decay, eg, gl = [], [], [], [], [], [], [], []
    for d, ci in [(d, c[step]) for step in range(len(ci_f)) for d, c in enumerate((ci_f, ci_b))]:
        rows = pl.ds(pl.multiple_of(ci * CHUNK, CHUNK), CHUNK)
        bg = bgc_ref[rows, :]
        bgt = bgct_ref[ci]
        for h in range(DN_HEADS):
            q.append(qkv_ref[rows, h * DN_D:(h + 1) * DN_D])
            k.append(qkv_ref[rows, DN_QK + h * DN_D:DN_QK + (h + 1) * DN_D])
            v.append(qkv_ref[rows, 2 * DN_QK + h * DN_D:2 * DN_QK + (h + 1) * DN_D])
            jb = d * 8 + h
            jg = d * 8 + DN_HEADS + h
            beta.append(bg[:, jb:jb + 1])
            gc = bg[:, jg:jg + 1]
            gcol.append(gc)
            decay.append(jnp.exp(jnp.where(masks[d][0], gc - bgt[jg:jg + 1, :], -jnp.inf)))
            eg.append(jnp.exp(gc))
            gl.append(gc[CHUNK - 1:CHUNK] if d == 0 else gc[0:1])

    contract_last = (((1,), (1,)), ((), ()))
    gram = [lax.dot_general(jnp.concatenate([k[n], q[n]], axis=0).astype(BF16), k[n].astype(BF16),
                            contract_last, preferred_element_type=F32) for n in chains]
    lmat = [jnp.where(masks[n // DN_HEADS % 2][1], gram[n][:CHUNK] * decay[n], 0.0) * beta[n] for n in chains]
    same_small, same_half = blocks
    dmat = [jnp.where(same_small, lmat[n], 0.0) for n in chains]
    tinv = [eye - dmat[n] for n in chains]
    p = [_bdot(dmat[n], dmat[n]) for n in chains]
    n_sq = 2
    while n_sq < INV_BLOCK // 2:
        x = [_bdot(jnp.concatenate([tinv[n], p[n]], axis=0), p[n]) for n in chains]
        tinv = [tinv[n] + x[n][:CHUNK] for n in chains]
        p = [x[n][CHUNK:] for n in chains]
        n_sq *= 2
    tinv = [tinv[n] + _bdot(tinv[n], p[n]) for n in chains]
    for off_mask in (jnp.logical_and(same_half, jnp.logical_not(same_small)), jnp.logical_not(same_half)):
        x = [_bdot(jnp.where(off_mask, lmat[n], 0.0), tinv[n]) for n in chains]
        tinv = [tinv[n] - _bdot(tinv[n], x[n]) for n in chains]
    uw = [_bdot(tinv[n], jnp.concatenate([v[n] * beta[n], k[n] * (beta[n] * eg[n])], axis=1)) for n in chains]

    return [(uw[n][:, :DN_D], uw[n][:, DN_D:], q[n] * eg[n], k[n] * jnp.exp(gl[n] - gcol[n]),
             gram[n][CHUNK:] * decay[n], jnp.exp(gl[n])) for n in chains]


def _delta_rec(pre, s_ref, want_out):
    chains = range(len(pre))
    contract_first = (((0,), (0,)), ((), ()))
    s = [s_ref[n] for n in chains]
    if want_out:
        wq = [_bdot(jnp.concatenate([pre[n][1], pre[n][2]], axis=0), s[n]) for n in chains]
        ws = [wq[n][:CHUNK] for n in chains]
    else:
        ws = [_bdot(pre[n][1], s[n]) for n in chains]
    vb = [(pre[n][0] - ws[n]).astype(BF16) for n in chains]
    for n in chains:
        s_ref[n] = s[n] * pre[n][5] + lax.dot_general(pre[n][3].astype(BF16), vb[n], contract_first,
                                                      preferred_element_type=F32)
    if not want_out:
        return None
    return [wq[n][CHUNK:] + jnp.dot(pre[n][4].astype(BF16), vb[n], preferred_element_type=F32)
            for n in chains]


def _delta_kernel(qc_ref, bc_ref, btc_ref, qx_ref, bx_ref, btx_ref, of_ref, ob_ref, s_ref, p_ref):
    s_ref[...] = jnp.zeros(s_ref.shape, F32)
    row = lax.broadcasted_iota(jnp.int32, (CHUNK, CHUNK), 0)
    col = lax.broadcasted_iota(jnp.int32, (CHUNK, CHUNK), 1)
    masks = ((row >= col, row > col), (row <= col, row < col))
    eye = (row == col).astype(F32)
    blocks = (row // INV_BLOCK == col // INV_BLOCK, row // (CHUNK // 2) == col // (CHUNK // 2))
    per_step = 2 * DN_HEADS
    n_chains = DELTA_STEPS * per_step

    def stage(pre):
        for n in range(n_chains):
            u, w, qe, kd, amat, egl = pre[n]
            for j, val in enumerate((u, w, qe, kd)):
                p_ref[n, j] = val
            p_ref[n, 4] = jnp.concatenate([amat, jnp.broadcast_to(egl, (CHUNK, DN_D - CHUNK))], axis=1)

    def staged():
        return [(p_ref[n, 0], p_ref[n, 1], p_ref[n, 2], p_ref[n, 3], p_ref[n, 4][:, :CHUNK],
                 p_ref[n, 4][0:1, CHUNK:CHUNK + 1]) for n in range(n_chains)]

    def scan(refs, want_out):
        n_chunks = refs[0].shape[0] // CHUNK
        assert n_chunks % DELTA_STEPS == 0

        def pre(first):
            fwd = [first + s for s in range(DELTA_STEPS)]
            return _delta_pre(refs, fwd, [n_chunks - 1 - c for c in fwd], masks, blocks, eye)

        stage(pre(0))

        def update(i, cur):
            for s in range(DELTA_STEPS):
                o = _delta_rec(cur[s * per_step:(s + 1) * per_step], s_ref, want_out)
                if want_out:
                    rf = pl.ds(pl.multiple_of((i + s) * CHUNK, CHUNK), CHUNK)
                    rb = pl.ds(pl.multiple_of((n_chunks - 1 - i - s) * CHUNK, CHUNK), CHUNK)
                    for h in range(DN_HEADS):
                        cols = slice(h * DN_D, (h + 1) * DN_D)
                        of_ref[rf, cols] = o[h]
                        ob_ref[rb, cols] = o[DN_HEADS + h]

        def body(it, carry):
            i = it * DELTA_STEPS
            cur = staged()
            new = pre(i + DELTA_STEPS)
            update(i, cur)
            stage(new)
            return carry

        n_iter = n_chunks // DELTA_STEPS
        lax.fori_loop(0, n_iter - 1, body, 0)
        update((n_iter - 1) * DELTA_STEPS, staged())

    scan((qc_ref, bc_ref, btc_ref), False)
    scan((qx_ref, bx_ref, btx_ref), True)


def _delta_scan(qkv_c, bgc_c, bgct_c, qkv_x, bgc_x, bgct_x, bsz):
    lc = qkv_c.shape[0] // bsz
    lx = qkv_x.shape[0] // bsz
    row_map = lambda b: (b, 0)
    out = jax.ShapeDtypeStruct((bsz * lx, DN_WIDTH), F32)
    return pl.pallas_call(
        _delta_kernel,
        grid=(bsz,),
        in_specs=[pl.BlockSpec((lc, DN_CONV_DIM), row_map), pl.BlockSpec((lc, LANES), row_map),
                  pl.BlockSpec((lc // CHUNK, 16, CHUNK), lambda b: (b, 0, 0)),
                  pl.BlockSpec((lx, DN_CONV_DIM), row_map), pl.BlockSpec((lx, LANES), row_map),
                  pl.BlockSpec((lx // CHUNK, 16, CHUNK), lambda b: (b, 0, 0))],
        out_specs=[pl.BlockSpec((lx, DN_WIDTH), row_map), pl.BlockSpec((lx, DN_WIDTH), row_map)],
        out_shape=[out, out],
        scratch_shapes=[pltpu.VMEM((2 * DN_HEADS, DN_D, DN_D), F32),
                        pltpu.VMEM((DELTA_STEPS * 2 * DN_HEADS, 5, CHUNK, DN_D), F32)],
        compiler_params=_cparams("parallel"),
        name="delta_scan",
    )(qkv_c, bgc_c, bgct_c, qkv_x, bgc_x, bgct_x)


def _pad_rows(a, rows):
    return jnp.pad(a, ((0, rows - a.shape[0]), (0, 0)))


def _mixer_front(x, c, ctx, c_ctx, w_mod, b_mod, g_mix, w_in, conv_w, a_log, dt_bias):
    bsz, seq, d = x.shape
    lc = ctx.shape[1]
    off_z = DN_CONV_DIM
    off_ba = off_z + DN_WIDTH
    off_cf = off_ba + 4 * DN_HEADS

    cc = _pad_rows(jnp.concatenate([c, c_ctx[None, :]], axis=0), -(-(bsz + 1) // 8) * 8)
    mod = _modulation(cc, w_mod, b_mod)
    mods = [mod[:, j * d:(j + 1) * d] for j in range(6)]
    lat = [m[:bsz].reshape(bsz, 1, d) for m in mods]
    con = [m[bsz:bsz + 1].reshape(1, 1, d) for m in mods]

    wqkv = w_in[:, :off_z].astype(BF16)
    wz = w_in[:, off_z:off_ba].astype(BF16)
    wba = jnp.pad(w_in[:, off_ba:off_cf], ((0, 0), (0, LANES - 4 * DN_HEADS))).astype(BF16)
    wcf = w_in[:, off_cf:].astype(BF16)
    zeros4 = jnp.zeros((2, DN_HEADS), F32)
    alog_row = jnp.pad(jnp.concatenate([zeros4, a_log], axis=1).reshape(1, -1), ((0, 0), (0, LANES - 16)))
    dtb_row = jnp.pad(jnp.concatenate([zeros4, dt_bias], axis=1).reshape(1, -1), ((0, 0), (0, LANES - 16)))
    g_row = g_mix.reshape(1, d)

    qkv_c, bgc_c, bgct_c = _in_projection(ctx.reshape(bsz * lc, d), con[0], con[1], g_row, wqkv, wba,
                                          alog_row, dtb_row, None, None, bsz * lc, min(lc, 512))
    qkv_x, bgc_x, bgct_x, sz, y = _in_projection(x.reshape(bsz * seq, d), lat[0], lat[1], g_row, wqkv, wba,
                                                 alog_row, dtb_row, wz, wcf, seq, min(seq, 512))
    qkvn_c = _short_conv(qkv_c, conv_w, lc, min(lc, 512))
    qkvn_x = _short_conv(qkv_x, conv_w, seq, min(seq, 512))
    o_f, o_b = _delta_scan(qkvn_c, bgc_c, bgct_c, qkvn_x, bgc_x, bgct_x, bsz)
    return dict(mod=mod, lat=lat, qkv_x=qkv_x, qkv_c=qkv_c, sz=sz, y=y, qkvn_x=qkvn_x, bgc_x=bgc_x,
                bgct_x=bgct_x, o_f=o_f, o_b=o_b)


CF_ROWS = 32


def _cfconv_kernel(y_ref, w_ref, b_ref, lng_ref, lnb_ref, o_ref, pad_ref):
    n = y_ref.shape[0]
    reach = (CF_K // 2) * GRID_W
    pad_ref[0:reach, :] = jnp.zeros((reach, CF_CH), F32)
    pad_ref[reach:reach + n, :] = y_ref[...]
    pad_ref[reach + n:, :] = jnp.zeros((reach, CF_CH), F32)

    def body(r, carry):
        r0 = pl.multiple_of(r * CF_ROWS, CF_ROWS)
        acc = jnp.zeros((CF_ROWS, CF_CH), F32)
        for k in range(CF_K):
            acc = acc + w_ref[k:k + 1, :] * pad_ref[pl.ds(r0 + k * GRID_W, CF_ROWS), :]
        acc = acc + b_ref[...]
        mu = jnp.mean(acc, axis=-1, keepdims=True)
        xc = acc - mu
        var = jnp.mean(xc * xc, axis=-1, keepdims=True)
        o_ref[pl.ds(r0, CF_ROWS), :] = _silu(xc * lax.rsqrt(var + EPS) * lng_ref[...] + lnb_ref[...])
        return carry

    lax.fori_loop(0, n // CF_ROWS, body, 0, unroll=4)


def _cf_conv(y, dw_w, dw_b, ln_g, ln_b, bsz):
    t, ch = y.shape
    n = t // bsz
    reach = (CF_K // 2) * GRID_W
    const = lambda b: (0, 0)
    return pl.pallas_call(
        _cfconv_kernel,
        grid=(bsz,),
        in_specs=[pl.BlockSpec((n, ch), lambda b: (b, 0)), pl.BlockSpec(dw_w.shape, const),
                  pl.BlockSpec((1, ch), const), pl.BlockSpec((1, ch), const), pl.BlockSpec((1, ch), const)],
        out_specs=pl.BlockSpec((n, ch), lambda b: (b, 0)),
        out_shape=jax.ShapeDtypeStruct((t, ch), F32),
        scratch_shapes=[pltpu.VMEM((n + 2 * reach, ch), F32)],
        compiler_params=_cparams("parallel"),
        name="conformer_conv",
    )(y, dw_w, dw_b.reshape(1, ch), ln_g.reshape(1, ch), ln_b.reshape(1, ch))


def _mixout_kernel(of_ref, ob_ref, sz_ref, cf_ref, x_ref, gt1_ref, sh2_ref, sc2_ref, gt2_ref, ng_ref, wo_ref,
                   gffn_ref, rwt_ref, wsgu_ref, wsd_ref, base_o, hn_o, lg_o):
    contract_last = (((1,), (1,)), ((), ()))
    rw = rwt_ref[...]
    rw_hi = rw.astype(BF16)
    rw_lo = (rw - rw_hi.astype(F32)).astype(BF16)
    rw_both = jnp.concatenate([rw_hi, rw_lo], axis=0)
    tm = x_ref.shape[0]
    for r0 in range(0, tm, tm // 2):
        rows = slice(r0, r0 + tm // 2)
        o = of_ref[rows, :] + ob_ref[rows, :]
        parts = []
        for h in range(DN_HEADS):
            oh = o[:, h * DN_D:(h + 1) * DN_D]
            parts.append(oh * lax.rsqrt(jnp.mean(oh * oh, axis=-1, keepdims=True) + EPS) * ng_ref[...])
        dn = jnp.concatenate(parts, axis=1) * sz_ref[rows, :]
        heads = jnp.concatenate([dn, cf_ref[rows, :]], axis=1).astype(BF16)
        x1 = x_ref[rows, :] + gt1_ref[0] * jnp.dot(heads, wo_ref[...], preferred_element_type=F32)
        hn = (x1 * lax.rsqrt(jnp.mean(x1 * x1, axis=-1, keepdims=True) + EPS) * gffn_ref[...]
              * (1.0 + sc2_ref[0]) + sh2_ref[0])
        hn_o[rows, :] = _pack_bf16_pairs(hn)
        hn_hi = hn.astype(BF16)
        hn_lo = (hn - hn_hi.astype(F32)).astype(BF16)
        both = lax.dot_general(rw_both, hn_hi, contract_last, preferred_element_type=F32)
        lg_o[:, rows] = (both[:N_EXPERTS] + both[N_EXPERTS:]
                         + lax.dot_general(rw_hi, hn_lo, contract_last, preferred_element_type=F32))
        gu = jnp.dot(hn_hi, wsgu_ref[...], preferred_element_type=F32)
        ds = gu.shape[1] // 2
        act = (_silu(gu[:, :ds]) * gu[:, ds:]).astype(BF16)
        base_o[rows, :] = x1 + gt2_ref[0] * jnp.dot(act, wsd_ref[...], preferred_element_type=F32)


def _mixer_out(o_f, o_b, sz, cfo, x2, gt1, sh2, sc2, gt2, norm_g, w_out, g_ffn, rwt, wsgu, wsd, seq, tm):
    t, d = x2.shape
    tiles = seq // tm
    const = lambda i: (0, 0)
    row_map = lambda i: (i, 0)
    mod_map = lambda i: (i // tiles, 0, 0)
    half = pl.BlockSpec((tm, DN_WIDTH), row_map)
    mod_spec = pl.BlockSpec((1, 1, d), mod_map)
    return pl.pallas_call(
        _mixout_kernel,
        grid=(t // tm,),
        in_specs=[half, half, half, half, pl.BlockSpec((tm, d), row_map),
                  mod_spec, mod_spec, mod_spec, mod_spec,
                  pl.BlockSpec((1, DN_D), const), pl.BlockSpec(w_out.shape, const), pl.BlockSpec((1, d), const),
                  pl.BlockSpec(rwt.shape, const), pl.BlockSpec(wsgu.shape, const), pl.BlockSpec(wsd.shape, const)],
        out_specs=[pl.BlockSpec((tm, d), row_map), pl.BlockSpec((tm, d // 2), row_map),
                   pl.BlockSpec((N_EXPERTS, tm), lambda i: (0, i))],
        out_shape=[jax.ShapeDtypeStruct((t, d), F32), jax.ShapeDtypeStruct((t, d // 2), U32),
                   jax.ShapeDtypeStruct((N_EXPERTS, t), F32)],
        compiler_params=_cparams("parallel"),
        name="mixer_out",
    )(o_f, o_b, sz, cfo, x2, gt1, sh2, sc2, gt2, norm_g, w_out, g_ffn, rwt, wsgu, wsd)


def _first_argmax(vals, idx, sentinel):
    m = jnp.max(vals, axis=0, keepdims=True)
    return m, jnp.min(jnp.where(vals == m, idx, sentinel), axis=0, keepdims=True)


def _router_kernel(lg_ref, bias_ref, e_o, r_o, wt_o, cnt_o, carry_ref):
    @pl.when(pl.program_id(0) == 0)
    def _():
        carry_ref[...] = jnp.zeros(carry_ref.shape, F32)

    tt = lg_ref.shape[1]
    neg = -jnp.inf
    scores = jax.nn.sigmoid(lg_ref[...])
    sel = scores + bias_ref[...]
    sub = lax.broadcasted_iota(jnp.int32, (GROUP_SIZE, tt), 0)

    rows = []
    for g in range(N_GROUPS):
        sg = sel[g * GROUP_SIZE:(g + 1) * GROUP_SIZE]
        m1, first = _first_argmax(sg, sub, GROUP_SIZE)
        m2 = jnp.max(jnp.where(sub == first, neg, sg), axis=0, keepdims=True)
        rows.append(m1 + m2)
    cur = jnp.concatenate(rows, axis=0)
    gidx = lax.broadcasted_iota(jnp.int32, (N_GROUPS, tt), 0)
    keep = gidx < 0
    for _ in range(TOPK_GROUPS):
        _, a = _first_argmax(cur, gidx, N_GROUPS)
        pick = gidx == a
        keep = jnp.logical_or(keep, pick)
        cur = jnp.where(pick, neg, cur)
    keep_e = jnp.concatenate([jnp.broadcast_to(keep[g:g + 1], (GROUP_SIZE, tt)) for g in range(N_GROUPS)],
                             axis=0)
    masked = jnp.where(keep_e, sel, neg)

    eidx = lax.broadcasted_iota(jnp.int32, (N_EXPERTS, tt), 0)
    e_rows, w_rows, picks = [], [], []
    for _ in range(TOP_K):
        _, a = _first_argmax(masked, eidx, N_EXPERTS)
        pick = eidx == a
        e_rows.append(a)
        w_rows.append(jnp.sum(jnp.where(pick, scores, 0.0), axis=0, keepdims=True))
        picks.append(pick)
        masked = jnp.where(pick, neg, masked)
    onehot = sum(p.astype(F32) for p in picks)
    scale = ROUTED_SCALE / sum(w_rows)

    tri = (lax.broadcasted_iota(jnp.int32, (tt, tt), 0) < lax.broadcasted_iota(jnp.int32, (tt, tt), 1))
    cum = jnp.dot(onehot.astype(BF16), tri.astype(BF16), preferred_element_type=F32) + carry_ref[:, 0:1]
    r_rows = [jnp.sum(jnp.where(p, cum, 0.0), axis=0, keepdims=True).astype(jnp.int32) for p in picks]
    carry_ref[...] = carry_ref[...] + jnp.sum(onehot, axis=1, keepdims=True)
    cnt_o[...] = carry_ref[...]

    fill = 8 - TOP_K
    e_o[...] = jnp.concatenate(e_rows + [jnp.zeros((fill, tt), jnp.int32)], axis=0)
    r_o[...] = jnp.concatenate(r_rows + [jnp.zeros((fill, tt), jnp.int32)], axis=0)
    w_pad = jnp.concatenate([w * scale for w in w_rows] + [jnp.zeros((LANES - TOP_K, tt), F32)], axis=0)
    wt_o[...] = jnp.transpose(w_pad)


def _router(lg_t, router_bias, tt):
    e, t = lg_t.shape
    idx_spec = pl.BlockSpec((8, tt), lambda i: (0, i))
    return pl.pallas_call(
        _router_kernel,
        grid=(t // tt,),
        in_specs=[pl.BlockSpec((e, tt), lambda i: (0, i)), pl.BlockSpec((e, 1), lambda i: (0, 0))],
        out_specs=[idx_spec, idx_spec, pl.BlockSpec((tt, LANES), lambda i: (i, 0)),
                   pl.BlockSpec((e, LANES), lambda i: (0, 0))],
        out_shape=[jax.ShapeDtypeStruct((8, t), jnp.int32), jax.ShapeDtypeStruct((8, t), jnp.int32),
                   jax.ShapeDtypeStruct((t, LANES), F32), jax.ShapeDtypeStruct((e, LANES), F32)],
        scratch_shapes=[pltpu.VMEM((e, LANES), F32)],
        compiler_params=_cparams("arbitrary"),
        name="router",
    )(lg_t, router_bias.reshape(e, 1))


SLOT_STRIDE = 8
ROW_GROUP = 8


def _slot_kernel(pstart_ref, e_ref, r_ref, o_ref):
    e = e_ref[...]
    acc = r_ref[...]
    for x in range(N_EXPERTS):
        acc = acc + jnp.where(e == x, pstart_ref[x], 0)
    tt = acc.shape[1]
    pad = jnp.concatenate([acc, jnp.zeros((LANES - acc.shape[0], tt), jnp.int32)], axis=0)
    o_ref[...] = jnp.transpose(pad)[:, :SLOT_STRIDE]


def _slot_index(pstart, e_idx, rank, tt):
    rows, t = e_idx.shape
    spec = pl.BlockSpec((rows, tt), lambda i, ps: (0, i))
    slot = pl.pallas_call(
        _slot_kernel,
        grid_spec=pltpu.PrefetchScalarGridSpec(
            num_scalar_prefetch=1, grid=(t // tt,), in_specs=[spec, spec],
            out_specs=pl.BlockSpec((tt, SLOT_STRIDE), lambda i, ps: (i, 0))),
        out_shape=jax.ShapeDtypeStruct((t, SLOT_STRIDE), jnp.int32),
        compiler_params=_cparams("parallel"),
        name="slot_index",
    )(pstart, e_idx, rank)
    return slot.reshape(t * SLOT_STRIDE)


def _row_copies(n_tokens, make_copy):
    def body(g, carry):
        base = g * (ROW_GROUP * SLOT_STRIDE)
        for s in range(ROW_GROUP):
            for j in range(TOP_K):
                make_copy(g, s, j, base + (s * SLOT_STRIDE + j)).start(priority=(s * TOP_K + j) % 2)
        return carry

    lax.fori_loop(0, n_tokens // ROW_GROUP, body, 0)


def _dispatch_kernel(pstart_ref, count_ref, slot_ref, hn_ref, xs_hbm, zero_ref, sem):
    td = hn_ref.shape[0] * ROW_GROUP
    bm = zero_ref.shape[0]

    def zero_fill(e, wait):
        end = pstart_ref[e] + count_ref[e]
        aligned = pl.multiple_of((end + 7) // 8 * 8, 8)
        for i in range(7):
            @pl.when(end + i < aligned)
            def _():
                cp = pltpu.make_async_copy(zero_ref.at[pl.ds(0, 1)], xs_hbm.at[pl.ds(end + i, 1)], sem)
                cp.wait() if wait else cp.start()
        cp = pltpu.make_async_copy(zero_ref, xs_hbm.at[pl.ds(aligned, bm)], sem)
        cp.wait() if wait else cp.start()

    @pl.when(pl.program_id(0) == 0)
    def _():
        zero_ref[...] = jnp.zeros(zero_ref.shape, zero_ref.dtype)

        def start(e, carry):
            zero_fill(e, False)
            return carry

        def wait(e, carry):
            zero_fill(e, True)
            return carry

        lax.fori_loop(0, N_EXPERTS, start, 0)
        lax.fori_loop(0, N_EXPERTS, wait, 0)

        last = N_EXPERTS - 1
        tail = (pstart_ref[last] + count_ref[last] + bm - 1) // bm

        def tail_copy(b):
            return pltpu.make_async_copy(zero_ref, xs_hbm.at[pl.ds(pl.multiple_of(b * bm, bm), bm)], sem)

        def tail_start(b, carry):
            tail_copy(b).start()
            return carry

        def tail_wait(b, carry):
            tail_copy(b).wait()
            return carry

        lax.fori_loop(tail, xs_hbm.shape[0] // bm, tail_start, 0)
        lax.fori_loop(tail, xs_hbm.shape[0] // bm, tail_wait, 0)

    _row_copies(td, lambda g, s, j, i: pltpu.make_async_copy(
        hn_ref.at[g, pl.ds(s, 1)], xs_hbm.at[pl.ds(slot_ref[i], 1)], sem))
    pltpu.make_async_copy(xs_hbm.at[pl.ds(0, td * TOP_K)], xs_hbm.at[pl.ds(0, td * TOP_K)], sem).wait()


def _dispatch(pstart, counts, slot, hn, n_slots, td, bm):
    t, d = hn.shape
    return pl.pallas_call(
        _dispatch_kernel,
        grid_spec=pltpu.PrefetchScalarGridSpec(
            num_scalar_prefetch=2, grid=(t // td,),
            in_specs=[pl.BlockSpec((td * SLOT_STRIDE,), lambda i, ps, cn: (i,), memory_space=pltpu.SMEM),
                      pl.BlockSpec((td // ROW_GROUP, ROW_GROUP, d), lambda i, ps, cn: (i, 0, 0))],
            out_specs=pl.BlockSpec(memory_space=pl.ANY),
            scratch_shapes=[pltpu.VMEM((bm, d), hn.dtype), pltpu.SemaphoreType.DMA(())]),
        out_shape=jax.ShapeDtypeStruct((n_slots, d), hn.dtype),
        compiler_params=_cparams("arbitrary"),
        name="dispatch",
    )(pstart, counts, slot, hn.reshape(t // ROW_GROUP, ROW_GROUP, d))


EXPERT_SUB = 256


def _expert_kernel(bexp_ref, bsrc_ref, nused_ref, xs_ref, wg_ref, wu_ref, wd_ref, y_ref):
    del bexp_ref, bsrc_ref
    i = pl.program_id(0)

    @pl.when(i < nused_ref[0])
    def _():
        wg = wg_ref[0].astype(BF16)
        wu = wu_ref[0].astype(BF16)
        wd = wd_ref[0].astype(BF16)
        for r0 in range(0, xs_ref.shape[0], EXPERT_SUB):
            rows = slice(r0, r0 + EXPERT_SUB)
            xb = jnp.concatenate(_unpack_bf16_pairs(xs_ref[rows, :]), axis=1).astype(BF16)
            g = jnp.dot(xb, wg, preferred_element_type=F32)
            u = jnp.dot(xb, wu, preferred_element_type=F32)
            act = (_silu(g) * u).astype(BF16)
            y_ref[rows, :] = _pack_bf16_pairs(jnp.dot(act, wd, preferred_element_type=F32))

    @pl.when(i >= nused_ref[0])
    def _():
        y_ref[...] = jnp.zeros(y_ref.shape, y_ref.dtype)


def _experts(block_exp, block_src, n_used, xs, w_gate, w_up, w_down, bm):
    n_slots, d = xs.shape
    w_map = lambda i, be, bs, nu: (be[i], 0, 0)
    return pl.pallas_call(
        _expert_kernel,
        grid_spec=pltpu.PrefetchScalarGridSpec(
            num_scalar_prefetch=3, grid=(n_slots // bm,),
            in_specs=[pl.BlockSpec((bm, d), lambda i, be, bs, nu: (bs[i], 0)),
                      pl.BlockSpec((1,) + w_gate.shape[1:], w_map),
                      pl.BlockSpec((1,) + w_up.shape[1:], w_map),
                      pl.BlockSpec((1,) + w_down.shape[1:], w_map)],
            out_specs=pl.BlockSpec((bm, d), lambda i, be, bs, nu: (i, 0))),
        out_shape=jax.ShapeDtypeStruct((n_slots, d), xs.dtype),
        compiler_params=_cparams("arbitrary"),
        name="experts",
    )(block_exp, block_src, n_used, xs, w_gate, w_up, w_down)


def _combine_kernel(slot_ref, wt_ref, base_ref, gt2_ref, gfin_ref, y_hbm, o_ref, buf, sem):
    tg = base_ref.shape[0]
    _row_copies(tg, lambda g, s, j, i: pltpu.make_async_copy(
        y_hbm.at[pl.ds(slot_ref[i], 1)], buf.at[j, g, pl.ds(s, 1)], sem))
    pltpu.make_async_copy(buf, buf, sem).wait()
    wt = wt_ref[...]
    lo = hi = None
    for j in range(TOP_K):
        ylo, yhi = _unpack_bf16_pairs(buf[j].reshape(tg, buf.shape[-1]))
        w = wt[:, j:j + 1]
        lo = w * ylo if lo is None else lo + w * ylo
        hi = w * yhi if hi is None else hi + w * yhi
    xf = base_ref[...] + gt2_ref[0] * jnp.concatenate([lo, hi], axis=1)
    o_ref[...] = xf * lax.rsqrt(jnp.mean(xf * xf, axis=-1, keepdims=True) + EPS) * gfin_ref[...]


def _combine(slot, wt, base, gt2, g_final, y, seq, tg):
    t, d = base.shape
    tiles = seq // tg
    return pl.pallas_call(
        _combine_kernel,
        grid=(t // tg,),
        in_specs=[pl.BlockSpec((tg * SLOT_STRIDE,), lambda i: (i,), memory_space=pltpu.SMEM),
                  pl.BlockSpec((tg, LANES), lambda i: (i, 0)),
                  pl.BlockSpec((tg, d), lambda i: (i, 0)),
                  pl.BlockSpec((1, 1, d), lambda i: (i // tiles, 0, 0)),
                  pl.BlockSpec((1, d), lambda i: (0, 0)),
                  pl.BlockSpec(memory_space=pl.ANY)],
        out_specs=pl.BlockSpec((tg, d), lambda i: (i, 0)),
        out_shape=jax.ShapeDtypeStruct((t, d), F32),
        scratch_shapes=[pltpu.VMEM((TOP_K, tg // ROW_GROUP, ROW_GROUP, y.shape[1]), y.dtype),
                        pltpu.SemaphoreType.DMA(())],
        compiler_params=_cparams("arbitrary"),
        name="combine",
    )(slot, wt, base, gt2, g_final.reshape(1, d), y)


EXPERT_BLOCK = 512
ROW_TILE = 512
ROUTER_TILE = 512
GATHER_TILE = 512
SLOT_TILE = 4096


def _moe_plan(counts, n_tokens, bm):
    padded = (counts + bm - 1) // bm * bm
    pad_end = jnp.cumsum(padded)
    pad_start = (pad_end - padded).astype(jnp.int32)
    n_blocks = -(-(n_tokens * TOP_K) // bm) + N_EXPERTS + 1
    n_used = (pad_end[-1] // bm).astype(jnp.int32)
    block_src = jnp.minimum(jnp.arange(n_blocks, dtype=jnp.int32), jnp.maximum(n_used - 1, 0))
    block_exp = jnp.sum((pad_end[None, :] <= (block_src * bm)[:, None]).astype(jnp.int32), axis=1)
    block_exp = jnp.minimum(block_exp, N_EXPERTS - 1)
    return pad_start, block_exp, block_src, n_used.reshape(1), n_blocks * bm


def kernel(x, c, ctx, c_ctx, w_mod, b_mod, g_mix, g_ffn, w_in, w_out, dn_conv_w, dn_a_log, dn_dt_bias,
           dn_norm_g, cf_dw_w, cf_dw_b, cf_ln_g, cf_ln_b, router_w, router_bias, exp_w_gate, exp_w_up,
           exp_w_down, sh_w_gate, sh_w_up, sh_w_down, g_final):
    assert w_mod.shape[0] == 1, "single-layer block: the context stream is never re-read"
    bsz, seq, d = x.shape
    t = bsz * seq
    x2 = x.reshape(t, d)
    st = _mixer_front(x, c, ctx, c_ctx, w_mod[0], b_mod[0], g_mix[0], w_in[0], dn_conv_w[0], dn_a_log[0],
                      dn_dt_bias[0])
    _, _, gt1, sh2, sc2, gt2 = st['lat']
    cfo = _cf_conv(st['y'], cf_dw_w[0], cf_dw_b[0], cf_ln_g[0], cf_ln_b[0], bsz)
    wsgu = jnp.concatenate([sh_w_gate[0], sh_w_up[0]], axis=1).astype(BF16)
    base, hn, lg_t = _mixer_out(st['o_f'], st['o_b'], st['sz'], cfo, x2, gt1, sh2, sc2, gt2,
                                dn_norm_g[0].reshape(1, DN_D), w_out[0].astype(BF16), g_ffn[0].reshape(1, d),
                                router_w[0].T, wsgu, sh_w_down[0].astype(BF16), seq, min(seq, ROW_TILE))
    e_idx, rank, wt, cnt = _router(lg_t, router_bias[0], min(t, ROUTER_TILE))
    counts = cnt[:, 0].astype(jnp.int32)
    pstart, block_exp, block_src, n_used, n_slots = _moe_plan(counts, t, EXPERT_BLOCK)
    slot = _slot_index(pstart, e_idx, rank, min(t, SLOT_TILE))
    xs = _dispatch(pstart, counts, slot, hn, n_slots, min(t, GATHER_TILE), EXPERT_BLOCK)
    y = _experts(block_exp, block_src, n_used, xs, exp_w_gate[0], exp_w_up[0], exp_w_down[0], EXPERT_BLOCK)
    out = _combine(slot, wt, base, gt2, g_final, y, seq, min(seq, GATHER_TILE))
    return out.reshape(bsz, seq, d)
```

```python
import functools

import jax
import jax.numpy as jnp
from jax import lax
from jax.experimental import pallas as pl
from jax.experimental.pallas import tpu as pltpu

F32 = jnp.float32
U32 = jnp.uint32
BF16 = jnp.bfloat16
HIGHEST = lax.Precision.HIGHEST

EPS = 1e-6
LANES = 128
GRID_W = 64
DN_HEADS = 4
DN_D = 128
DN_QK = DN_HEADS * DN_D
DN_WIDTH = DN_HEADS * DN_D
DN_CONV_DIM = 2 * DN_QK + DN_WIDTH
SHORT_CONV = 7
CHUNK = 64
CF_CH = 512
CF_K = 31
N_EXPERTS = 64
TOP_K = 6
N_GROUPS = 8
GROUP_SIZE = N_EXPERTS // N_GROUPS
TOPK_GROUPS = 4
ROUTED_SCALE = 2.5
HALO = 8
VMEM_LIMIT = 56 * 1024 * 1024


def _cparams(*sem):
    return pltpu.CompilerParams(dimension_semantics=sem, vmem_limit_bytes=VMEM_LIMIT)


def _silu(v):
    return v * jax.nn.sigmoid(v)


def _bdot(a, b):
    return jnp.dot(a.astype(BF16), b.astype(BF16), preferred_element_type=F32)


def _pack_bf16_pairs(v):
    n = v.shape[1] // 2
    lo = lax.bitcast_convert_type(v[:, :n].astype(BF16).astype(F32), U32)
    hi = lax.bitcast_convert_type(v[:, n:].astype(BF16).astype(F32), U32)
    return (lo >> 16) | (hi & jnp.uint32(0xFFFF0000))


def _unpack_bf16_pairs(w):
    lo = lax.bitcast_convert_type(w << 16, F32)
    hi = lax.bitcast_convert_type(w & jnp.uint32(0xFFFF0000), F32)
    return lo, hi


def _mod_kernel(c_ref, w_ref, b_ref, o_ref):
    o_ref[...] = jnp.dot(_silu(c_ref[...]), w_ref[...], preferred_element_type=F32,
                         precision=HIGHEST) + b_ref[...]


def _modulation(cc, w_mod, b_mod):
    rows, d = cc.shape
    n = w_mod.shape[1]
    tn = 1024
    return pl.pallas_call(
        _mod_kernel,
        grid=(n // tn,),
        in_specs=[pl.BlockSpec((rows, d), lambda j: (0, 0)),
                  pl.BlockSpec((d, tn), lambda j: (0, j)),
                  pl.BlockSpec((1, tn), lambda j: (0, j))],
        out_specs=pl.BlockSpec((rows, tn), lambda j: (0, j)),
        out_shape=jax.ShapeDtypeStruct((rows, n), F32),
        compiler_params=_cparams("parallel"),
        name="modulation",
    )(cc, w_mod, b_mod.reshape(1, n))


def _chunk_cumsum(g, reverse):
    n = g.shape[0]
    pos = lax.broadcasted_iota(jnp.int32, g.shape, 0) % CHUNK
    s = 1
    while s < CHUNK:
        if reverse:
            shifted = pltpu.roll(g, n - s, 0)
            ok = pos < CHUNK - s
        else:
            shifted = pltpu.roll(g, s, 0)
            ok = pos >= s
        g = g + jnp.where(ok, shifted, 0.0)
        s *= 2
    return g


def _inproj_kernel(latent, x_ref, sh_ref, sc_ref, g_ref, wqkv_ref, wba_ref, alog_ref, dtb_ref, *rest):
    if latent:
        wz_ref, wcf_ref, qkv_o, bgc_o, bgct_o, sz_o, y_o = rest
    else:
        qkv_o, bgc_o, bgct_o = rest
    x = x_ref[...]
    xn = x * lax.rsqrt(jnp.mean(x * x, axis=-1, keepdims=True) + EPS) * g_ref[...]
    hb = (xn * (1.0 + sc_ref[0]) + sh_ref[0]).astype(BF16)
    qkv_o[...] = jnp.dot(hb, wqkv_ref[...], preferred_element_type=F32)

    ba = jnp.dot(hb, wba_ref[...], preferred_element_type=F32)
    col = lax.broadcasted_iota(jnp.int32, ba.shape, 1)
    is_beta = (col % 8) < DN_HEADS
    g = -jnp.exp(alog_ref[...]) * jax.nn.softplus(ba + dtb_ref[...])
    g = jnp.where(is_beta, 0.0, g)
    gc = jnp.where(col < 8, _chunk_cumsum(g, False), _chunk_cumsum(g, True))
    bgc = jnp.where(is_beta, jax.nn.sigmoid(ba), gc)
    bgc_o[...] = bgc
    for c in range(bgc.shape[0] // CHUNK):
        bgct_o[c] = jnp.transpose(bgc[c * CHUNK:(c + 1) * CHUNK, :])[:16, :]

    if latent:
        sz_o[...] = _silu(jnp.dot(hb, wz_ref[...], preferred_element_type=F32))
        cf = jnp.dot(hb, wcf_ref[...], preferred_element_type=F32)
        y_o[...] = cf[:, :CF_CH] * jax.nn.sigmoid(cf[:, CF_CH:])


def _in_projection(x2, sh, sc, g_mix, wqkv, wba, alog_row, dtb_row, wz, wcf, rows_per_mod, tm):
    t, d = x2.shape
    latent = wz is not None
    tiles_per_mod = rows_per_mod // tm
    const = lambda i: (0, 0)
    mod_map = lambda i: (i // tiles_per_mod, 0, 0)
    row_map = lambda i: (i, 0)
    in_specs = [pl.BlockSpec((tm, d), row_map),
                pl.BlockSpec((1, 1, d), mod_map), pl.BlockSpec((1, 1, d), mod_map),
                pl.BlockSpec((1, d), const),
                pl.BlockSpec(wqkv.shape, const), pl.BlockSpec(wba.shape, const),
                pl.BlockSpec((1, LANES), const), pl.BlockSpec((1, LANES), const)]
    args = [x2, sh, sc, g_mix, wqkv, wba, alog_row, dtb_row]
    out_specs = [pl.BlockSpec((tm, DN_CONV_DIM), row_map), pl.BlockSpec((tm, LANES), row_map),
                 pl.BlockSpec((tm // CHUNK, 16, CHUNK), lambda i: (i, 0, 0))]
    out_shape = [jax.ShapeDtypeStruct((t, DN_CONV_DIM), F32), jax.ShapeDtypeStruct((t, LANES), F32),
                 jax.ShapeDtypeStruct((t // CHUNK, 16, CHUNK), F32)]
    if latent:
        in_specs += [pl.BlockSpec(wz.shape, const), pl.BlockSpec(wcf.shape, const)]
        args += [wz, wcf]
        out_specs += [pl.BlockSpec((tm, DN_WIDTH), row_map), pl.BlockSpec((tm, CF_CH), row_map)]
        out_shape += [jax.ShapeDtypeStruct((t, DN_WIDTH), F32), jax.ShapeDtypeStruct((t, CF_CH), F32)]
    return pl.pallas_call(
        functools.partial(_inproj_kernel, latent),
        grid=(t // tm,),
        in_specs=in_specs, out_specs=out_specs, out_shape=out_shape,
        compiler_params=_cparams("parallel"),
        name="in_projection_latent" if latent else "in_projection_context",
    )(*args)


CONV_ROWS = 64


def _shortconv_kernel(tiles_per_seq, prev_ref, cur_ref, next_ref, w_ref, o_ref, ext_ref):
    i = pl.program_id(0)
    tm = cur_ref.shape[0]
    pos = i % tiles_per_seq
    ext_ref[0:HALO, :] = jnp.where(pos == 0, 0.0, prev_ref[...])
    ext_ref[HALO:HALO + tm, :] = cur_ref[...]
    ext_ref[HALO + tm:, :] = jnp.where(pos == tiles_per_seq - 1, 0.0, next_ref[...])
    reach = SHORT_CONV // 2

    for r0 in range(0, tm, CONV_ROWS):
        for cb in range(DN_CONV_DIM // LANES):
            cols = slice(cb * LANES, (cb + 1) * LANES)
            acc = jnp.zeros((CONV_ROWS, LANES), F32)
            for k in range(SHORT_CONV):
                s = r0 + HALO - reach + k
                acc = acc + w_ref[k:k + 1, cols] * ext_ref[s:s + CONV_ROWS, cols]
            a = _silu(acc)
            if cb < 2 * DN_HEADS:
                a = a * lax.rsqrt(jnp.sum(a * a, axis=-1, keepdims=True) + EPS)
            if cb < DN_HEADS:
                a = a * (DN_D ** -0.5)
            o_ref[r0:r0 + CONV_ROWS, cols] = a


def _short_conv(qkv, conv_w, seq_len, tm):
    t, c = qkv.shape
    tiles_per_seq = seq_len // tm
    hb = tm // HALO
    n_halo_blocks = t // HALO
    return pl.pallas_call(
        functools.partial(_shortconv_kernel, tiles_per_seq),
        grid=(t // tm,),
        in_specs=[pl.BlockSpec((HALO, c), lambda i: (jnp.maximum(i * hb - 1, 0), 0)),
                  pl.BlockSpec((tm, c), lambda i: (i, 0)),
                  pl.BlockSpec((HALO, c), lambda i: (jnp.minimum((i + 1) * hb, n_halo_blocks - 1), 0)),
                  pl.BlockSpec(conv_w.shape, lambda i: (0, 0))],
        out_specs=pl.BlockSpec((tm, c), lambda i: (i, 0)),
        out_shape=jax.ShapeDtypeStruct((t, c), F32),
        scratch_shapes=[pltpu.VMEM((tm + 2 * HALO, c), F32)],
        compiler_params=_cparams("parallel"),
        name="short_conv",
    )(qkv, qkv, qkv, conv_w)


DELTA_STEPS = 2
INV_BLOCK = 16


def _delta_pre(refs, ci_f, ci_b, masks, blocks, eye):
    qkv_ref, bgc_ref, bgct_ref = refs
    chains = range(2 * DN_HEADS * len(ci_f))
    q, k, v, beta, gcol, decay, eg, gl = [], [], [], [], [], [], [], []
    for d, ci in [(d, c[step]) for step in range(len(ci_f)) for d, c in enumerate((ci_f, ci_b))]:
        rows = pl.ds(pl.multiple_of(ci * CHUNK, CHUNK), CHUNK)
        bg = bgc_ref[rows, :]
        bgt = bgct_ref[ci]
        for h in range(DN_HEADS):
            q.append(qkv_ref[rows, h * DN_D:(h + 1) * DN_D])
            k.append(qkv_ref[rows, DN_QK + h * DN_D:DN_QK + (h + 1) * DN_D])
            v.append(qkv_ref[rows, 2 * DN_QK + h * DN_D:2 * DN_QK + (h + 1) * DN_D])
            jb = d * 8 + h
            jg = d * 8 + DN_HEADS + h
            beta.append(bg[:, jb:jb + 1])
            gc = bg[:, jg:jg + 1]
            gcol.append(gc)
            decay.append(jnp.exp(jnp.where(masks[d][0], gc - bgt[jg:jg + 1, :], -jnp.inf)))
            eg.append(jnp.exp(gc))
            gl.append(gc[CHUNK - 1:CHUNK] if d == 0 else gc[0:1])

    contract_last = (((1,), (1,)), ((), ()))
    gram = [lax.dot_general(jnp.concatenate([k[n], q[n]], axis=0).astype(BF16), k[n].astype(BF16),
                            contract_last, preferred_element_type=F32) for n in chains]
    lmat = [jnp.where(masks[n // DN_HEADS % 2][1], gram[n][:CHUNK] * decay[n], 0.0) * beta[n] for n in chains]
    same_small, same_half = blocks
    dmat = [jnp.where(same_small, lmat[n], 0.0) for n in chains]
    tinv = [eye - dmat[n] for n in chains]
    p = [_bdot(dmat[n], dmat[n]) for n in chains]
    n_sq = 2
    while n_sq < INV_BLOCK // 2:
        x = [_bdot(jnp.concatenate([tinv[n], p[n]], axis=0), p[n]) for n in chains]
        tinv = [tinv[n] + x[n][:CHUNK] for n in chains]
        p = [x[n][CHUNK:] for n in chains]
        n_sq *= 2
    tinv = [tinv[n] + _bdot(tinv[n], p[n]) for n in chains]
    for off_mask in (jnp.logical_and(same_half, jnp.logical_not(same_small)), jnp.logical_not(same_half)):
        x = [_bdot(jnp.where(off_mask, lmat[n], 0.0), tinv[n]) for n in chains]
        tinv = [tinv[n] - _bdot(tinv[n], x[n]) for n in chains]
    uw = [_bdot(tinv[n], jnp.concatenate([v[n] * beta[n], k[n] * (beta[n] * eg[n])], axis=1)) for n in chains]

    return [(uw[n][:, :DN_D], uw[n][:, DN_D:], q[n] * eg[n], k[n] * jnp.exp(gl[n] - gcol[n]),
             gram[n][CHUNK:] * decay[n], jnp.exp(gl[n])) for n in chains]


def _delta_rec(pre, s_ref, want_out):
    chains = range(len(pre))
    contract_first = (((0,), (0,)), ((), ()))
    s = [s_ref[n] for n in chains]
    if want_out:
        wq = [_bdot(jnp.concatenate([pre[n][1], pre[n][2]], axis=0), s[n]) for n in chains]
        ws = [wq[n][:CHUNK] for n in chains]
    else:
        ws = [_bdot(pre[n][1], s[n]) for n in chains]
    vb = [(pre[n][0] - ws[n]).astype(BF16) for n in chains]
    for n in chains:
        s_ref[n] = s[n] * pre[n][5] + lax.dot_general(pre[n][3].astype(BF16), vb[n], contract_first,
                                                      preferred_element_type=F32)
    if not want_out:
        return None
    return [wq[n][CHUNK:] + jnp.dot(pre[n][4].astype(BF16), vb[n], preferred_element_type=F32)
            for n in chains]


def _delta_kernel(qc_ref, bc_ref, btc_ref, qx_ref, bx_ref, btx_ref, of_ref, ob_ref, s_ref, p_ref):
    s_ref[...] = jnp.zeros(s_ref.shape, F32)
    row = lax.broadcasted_iota(jnp.int32, (CHUNK, CHUNK), 0)
    col = lax.broadcasted_iota(jnp.int32, (CHUNK, CHUNK), 1)
    masks = ((row >= col, row > col), (row <= col, row < col))
    eye = (row == col).astype(F32)
    blocks = (row // INV_BLOCK == col // INV_BLOCK, row // (CHUNK // 2) == col // (CHUNK // 2))
    per_step = 2 * DN_HEADS
    n_chains = DELTA_STEPS * per_step

    def stage(pre):
        for n in range(n_chains):
            u, w, qe, kd, amat, egl = pre[n]
            for j, val in enumerate((u, w, qe, kd)):
                p_ref[n, j] = val
            p_ref[n, 4] = jnp.concatenate([amat, jnp.broadcast_to(egl, (CHUNK, DN_D - CHUNK))], axis=1)

    def staged():
        return [(p_ref[n, 0], p_ref[n, 1], p_ref[n, 2], p_ref[n, 3], p_ref[n, 4][:, :CHUNK],
                 p_ref[n, 4][0:1, CHUNK:CHUNK + 1]) for n in range(n_chains)]

    def scan(refs, want_out):
        n_chunks = refs[0].shape[0] // CHUNK
        assert n_chunks % DELTA_STEPS == 0

        def pre(first):
            fwd = [first + s for s in range(DELTA_STEPS)]
            return _delta_pre(refs, fwd, [n_chunks - 1 - c for c in fwd], masks, blocks, eye)

        stage(pre(0))

        def update(i, cur):
            for s in range(DELTA_STEPS):
                o = _delta_rec(cur[s * per_step:(s + 1) * per_step], s_ref, want_out)
                if want_out:
                    rf = pl.ds(pl.multiple_of((i + s) * CHUNK, CHUNK), CHUNK)
                    rb = pl.ds(pl.multiple_of((n_chunks - 1 - i - s) * CHUNK, CHUNK), CHUNK)
                    for h in range(DN_HEADS):
                        cols = slice(h * DN_D, (h + 1) * DN_D)
                        of_ref[rf, cols] = o[h]
                        ob_ref[rb, cols] = o[DN_HEADS + h]

        def body(it, carry):
            i = it * DELTA_STEPS
            cur = staged()
            new = pre(i + DELTA_STEPS)
            update(i, cur)
            stage(new)
            return carry

        n_iter = n_chunks // DELTA_STEPS
        lax.fori_loop(0, n_iter - 1, body, 0)
        update((n_iter - 1) * DELTA_STEPS, staged())

    scan((qc_ref, bc_ref, btc_ref), False)
    scan((qx_ref, bx_ref, btx_ref), True)


def _delta_scan(qkv_c, bgc_c, bgct_c, qkv_x, bgc_x, bgct_x, bsz):
    lc = qkv_c.shape[0] // bsz
    lx = qkv_x.shape[0] // bsz
    row_map = lambda b: (b, 0)
    out = jax.ShapeDtypeStruct((bsz * lx, DN_WIDTH), F32)
    return pl.pallas_call(
        _delta_kernel,
        grid=(bsz,),
        in_specs=[pl.BlockSpec((lc, DN_CONV_DIM), row_map), pl.BlockSpec((lc, LANES), row_map),
                  pl.BlockSpec((lc // CHUNK, 16, CHUNK), lambda b: (b, 0, 0)),
                  pl.BlockSpec((lx, DN_CONV_DIM), row_map), pl.BlockSpec((lx, LANES), row_map),
                  pl.BlockSpec((lx // CHUNK, 16, CHUNK), lambda b: (b, 0, 0))],
        out_specs=[pl.BlockSpec((lx, DN_WIDTH), row_map), pl.BlockSpec((lx, DN_WIDTH), row_map)],
        out_shape=[out, out],
        scratch_shapes=[pltpu.VMEM((2 * DN_HEADS, DN_D, DN_D), F32),
                        pltpu.VMEM((DELTA_STEPS * 2 * DN_HEADS, 5, CHUNK, DN_D), F32)],
        compiler_params=_cparams("parallel"),
        name="delta_scan",
    )(qkv_c, bgc_c, bgct_c, qkv_x, bgc_x, bgct_x)


def _pad_rows(a, rows):
    return jnp.pad(a, ((0, rows - a.shape[0]), (0, 0)))


def _mixer_front(x, c, ctx, c_ctx, w_mod, b_mod, g_mix, w_in, conv_w, a_log, dt_bias):
    bsz, seq, d = x.shape
    lc = ctx.shape[1]
    off_z = DN_CONV_DIM
    off_ba = off_z + DN_WIDTH
    off_cf = off_ba + 4 * DN_HEADS

    cc = _pad_rows(jnp.concatenate([c, c_ctx[None, :]], axis=0), -(-(bsz + 1) // 8) * 8)
    mod = _modulation(cc, w_mod, b_mod)
    mods = [mod[:, j * d:(j + 1) * d] for j in range(6)]
    lat = [m[:bsz].reshape(bsz, 1, d) for m in mods]
    con = [m[bsz:bsz + 1].reshape(1, 1, d) for m in mods]

    wqkv = w_in[:, :off_z].astype(BF16)
    wz = w_in[:, off_z:off_ba].astype(BF16)
    wba = jnp.pad(w_in[:, off_ba:off_cf], ((0, 0), (0, LANES - 4 * DN_HEADS))).astype(BF16)
    wcf = w_in[:, off_cf:].astype(BF16)
    zeros4 = jnp.zeros((2, DN_HEADS), F32)
    alog_row = jnp.pad(jnp.concatenate([zeros4, a_log], axis=1).reshape(1, -1), ((0, 0), (0, LANES - 16)))
    dtb_row = jnp.pad(jnp.concatenate([zeros4, dt_bias], axis=1).reshape(1, -1), ((0, 0), (0, LANES - 16)))
    g_row = g_mix.reshape(1, d)

    qkv_c, bgc_c, bgct_c = _in_projection(ctx.reshape(bsz * lc, d), con[0], con[1], g_row, wqkv, wba,
                                          alog_row, dtb_row, None, None, bsz * lc, min(lc, 512))
    qkv_x, bgc_x, bgct_x, sz, y = _in_projection(x.reshape(bsz * seq, d), lat[0], lat[1], g_row, wqkv, wba,
                                                 alog_row, dtb_row, wz, wcf, seq, min(seq, 512))
    qkvn_c = _short_conv(qkv_c, conv_w, lc, min(lc, 512))
    qkvn_x = _short_conv(qkv_x, conv_w, seq, min(seq, 512))
    o_f, o_b = _delta_scan(qkvn_c, bgc_c, bgct_c, qkvn_x, bgc_x, bgct_x, bsz)
    return dict(mod=mod, lat=lat, qkv_x=qkv_x, qkv_c=qkv_c, sz=sz, y=y, qkvn_x=qkvn_x, bgc_x=bgc_x,
                bgct_x=bgct_x, o_f=o_f, o_b=o_b)


CF_ROWS = 32


def _cfconv_kernel(y_ref, w_ref, b_ref, lng_ref, lnb_ref, o_ref, pad_ref):
    n = y_ref.shape[0]
    reach = (CF_K // 2) * GRID_W
    pad_ref[0:reach, :] = jnp.zeros((reach, CF_CH), F32)
    pad_ref[reach:reach + n, :] = y_ref[...]
    pad_ref[reach + n:, :] = jnp.zeros((reach, CF_CH), F32)

    def body(r, carry):
        r0 = pl.multiple_of(r * CF_ROWS, CF_ROWS)
        acc = jnp.zeros((CF_ROWS, CF_CH), F32)
        for k in range(CF_K):
            acc = acc + w_ref[k:k + 1, :] * pad_ref[pl.ds(r0 + k * GRID_W, CF_ROWS), :]
        acc = acc + b_ref[...]
        mu = jnp.mean(acc, axis=-1, keepdims=True)
        xc = acc - mu
        var = jnp.mean(xc * xc, axis=-1, keepdims=True)
        o_ref[pl.ds(r0, CF_ROWS), :] = _silu(xc * lax.rsqrt(var + EPS) * lng_ref[...] + lnb_ref[...])
        return carry

    lax.fori_loop(0, n // CF_ROWS, body, 0, unroll=4)


def _cf_conv(y, dw_w, dw_b, ln_g, ln_b, bsz):
    t, ch = y.shape
    n = t // bsz
    reach = (CF_K // 2) * GRID_W
    const = lambda b: (0, 0)
    return pl.pallas_call(
        _cfconv_kernel,
        grid=(bsz,),
        in_specs=[pl.BlockSpec((n, ch), lambda b: (b, 0)), pl.BlockSpec(dw_w.shape, const),
                  pl.BlockSpec((1, ch), const), pl.BlockSpec((1, ch), const), pl.BlockSpec((1, ch), const)],
        out_specs=pl.BlockSpec((n, ch), lambda b: (b, 0)),
        out_shape=jax.ShapeDtypeStruct((t, ch), F32),
        scratch_shapes=[pltpu.VMEM((n + 2 * reach, ch), F32)],
        compiler_params=_cparams("parallel"),
        name="conformer_conv",
    )(y, dw_w, dw_b.reshape(1, ch), ln_g.reshape(1, ch), ln_b.reshape(1, ch))


def _mixout_kernel(of_ref, ob_ref, sz_ref, cf_ref, x_ref, gt1_ref, sh2_ref, sc2_ref, gt2_ref, ng_ref, wo_ref,
                   gffn_ref, rwt_ref, wsgu_ref, wsd_ref, base_o, hn_o, lg_o):
    contract_last = (((1,), (1,)), ((), ()))
    rw = rwt_ref[...]
    rw_hi = rw.astype(BF16)
    rw_lo = (rw - rw_hi.astype(F32)).astype(BF16)
    rw_both = jnp.concatenate([rw_hi, rw_lo], axis=0)
    tm = x_ref.shape[0]
    for r0 in range(0, tm, tm // 2):
        rows = slice(r0, r0 + tm // 2)
        o = of_ref[rows, :] + ob_ref[rows, :]
        parts = []
        for h in range(DN_HEADS):
            oh = o[:, h * DN_D:(h + 1) * DN_D]
            parts.append(oh * lax.rsqrt(jnp.mean(oh * oh, axis=-1, keepdims=True) + EPS) * ng_ref[...])
        dn = jnp.concatenate(parts, axis=1) * sz_ref[rows, :]
        heads = jnp.concatenate([dn, cf_ref[rows, :]], axis=1).astype(BF16)
        x1 = x_ref[rows, :] + gt1_ref[0] * jnp.dot(heads, wo_ref[...], preferred_element_type=F32)
        hn = (x1 * lax.rsqrt(jnp.mean(x1 * x1, axis=-1, keepdims=True) + EPS) * gffn_ref[...]
              * (1.0 + sc2_ref[0]) + sh2_ref[0])
        hn_o[rows, :] = _pack_bf16_pairs(hn)
        hn_hi = hn.astype(BF16)
        hn_lo = (hn - hn_hi.astype(F32)).astype(BF16)
        both = lax.dot_general(rw_both, hn_hi, contract_last, preferred_element_type=F32)
        lg_o[:, rows] = (both[:N_EXPERTS] + both[N_EXPERTS:]
                         + lax.dot_general(rw_hi, hn_lo, contract_last, preferred_element_type=F32))
        gu = jnp.dot(hn_hi, wsgu_ref[...], preferred_element_type=F32)
        ds = gu.shape[1] // 2
        act = (_silu(gu[:, :ds]) * gu[:, ds:]).astype(BF16)
        base_o[rows, :] = x1 + gt2_ref[0] * jnp.dot(act, wsd_ref[...], preferred_element_type=F32)


def _mixer_out(o_f, o_b, sz, cfo, x2, gt1, sh2, sc2, gt2, norm_g, w_out, g_ffn, rwt, wsgu, wsd, seq, tm):
    t, d = x2.shape
    tiles = seq // tm
    const = lambda i: (0, 0)
    row_map = lambda i: (i, 0)
    mod_map = lambda i: (i // tiles, 0, 0)
    half = pl.BlockSpec((tm, DN_WIDTH), row_map)
    mod_spec = pl.BlockSpec((1, 1, d), mod_map)
    return pl.pallas_call(
        _mixout_kernel,
        grid=(t // tm,),
        in_specs=[half, half, half, half, pl.BlockSpec((tm, d), row_map),
                  mod_spec, mod_spec, mod_spec, mod_spec,
                  pl.BlockSpec((1, DN_D), const), pl.BlockSpec(w_out.shape, const), pl.BlockSpec((1, d), const),
                  pl.BlockSpec(rwt.shape, const), pl.BlockSpec(wsgu.shape, const), pl.BlockSpec(wsd.shape, const)],
        out_specs=[pl.BlockSpec((tm, d), row_map), pl.BlockSpec((tm, d // 2), row_map),
                   pl.BlockSpec((N_EXPERTS, tm), lambda i: (0, i))],
        out_shape=[jax.ShapeDtypeStruct((t, d), F32), jax.ShapeDtypeStruct((t, d // 2), U32),
                   jax.ShapeDtypeStruct((N_EXPERTS, t), F32)],
        compiler_params=_cparams("parallel"),
        name="mixer_out",
    )(o_f, o_b, sz, cfo, x2, gt1, sh2, sc2, gt2, norm_g, w_out, g_ffn, rwt, wsgu, wsd)


def _first_argmax(vals, idx, sentinel):
    m = jnp.max(vals, axis=0, keepdims=True)
    return m, jnp.min(jnp.where(vals == m, idx, sentinel), axis=0, keepdims=True)


def _router_kernel(lg_ref, bias_ref, e_o, r_o, wt_o, cnt_o, carry_ref):
    @pl.when(pl.program_id(0) == 0)
    def _():
        carry_ref[...] = jnp.zeros(carry_ref.shape, F32)

    tt = lg_ref.shape[1]
    neg = -jnp.inf
    scores = jax.nn.sigmoid(lg_ref[...])
    sel = scores + bias_ref[...]
    sub = lax.broadcasted_iota(jnp.int32, (GROUP_SIZE, tt), 0)

    rows = []
    for g in range(N_GROUPS):
        sg = sel[g * GROUP_SIZE:(g + 1) * GROUP_SIZE]
        m1, first = _first_argmax(sg, sub, GROUP_SIZE)
        m2 = jnp.max(jnp.where(sub == first, neg, sg), axis=0, keepdims=True)
        rows.append(m1 + m2)
    cur = jnp.concatenate(rows, axis=0)
    gidx = lax.broadcasted_iota(jnp.int32, (N_GROUPS, tt), 0)
    keep = gidx < 0
    for _ in range(TOPK_GROUPS):
        _, a = _first_argmax(cur, gidx, N_GROUPS)
        pick = gidx == a
        keep = jnp.logical_or(keep, pick)
        cur = jnp.where(pick, neg, cur)
    keep_e = jnp.concatenate([jnp.broadcast_to(keep[g:g + 1], (GROUP_SIZE, tt)) for g in range(N_GROUPS)],
                             axis=0)
    masked = jnp.where(keep_e, sel, neg)

    eidx = lax.broadcasted_iota(jnp.int32, (N_EXPERTS, tt), 0)
    e_rows, w_rows, picks = [], [], []
    for _ in range(TOP_K):
        _, a = _first_argmax(masked, eidx, N_EXPERTS)
        pick = eidx == a
        e_rows.append(a)
        w_rows.append(jnp.sum(jnp.where(pick, scores, 0.0), axis=0, keepdims=True))
        picks.append(pick)
        masked = jnp.where(pick, neg, masked)
    onehot = sum(p.astype(F32) for p in picks)
    scale = ROUTED_SCALE / sum(w_rows)

    tri = (lax.broadcasted_iota(jnp.int32, (tt, tt), 0) < lax.broadcasted_iota(jnp.int32, (tt, tt), 1))
    cum = jnp.dot(onehot.astype(BF16), tri.astype(BF16), preferred_element_type=F32) + carry_ref[:, 0:1]
    r_rows = [jnp.sum(jnp.where(p, cum, 0.0), axis=0, keepdims=True).astype(jnp.int32) for p in picks]
    carry_ref[...] = carry_ref[...] + jnp.sum(onehot, axis=1, keepdims=True)
    cnt_o[...] = carry_ref[...]

    fill = 8 - TOP_K
    e_o[...] = jnp.concatenate(e_rows + [jnp.zeros((fill, tt), jnp.int32)], axis=0)
    r_o[...] = jnp.concatenate(r_rows + [jnp.zeros((fill, tt), jnp.int32)], axis=0)
    w_pad = jnp.concatenate([w * scale for w in w_rows] + [jnp.zeros((LANES - TOP_K, tt), F32)], axis=0)
    wt_o[...] = jnp.transpose(w_pad)


def _router(lg_t, router_bias, tt):
    e, t = lg_t.shape
    idx_spec = pl.BlockSpec((8, tt), lambda i: (0, i))
    return pl.pallas_call(
        _router_kernel,
        grid=(t // tt,),
        in_specs=[pl.BlockSpec((e, tt), lambda i: (0, i)), pl.BlockSpec((e, 1), lambda i: (0, 0))],
        out_specs=[idx_spec, idx_spec, pl.BlockSpec((tt, LANES), lambda i: (i, 0)),
                   pl.BlockSpec((e, LANES), lambda i: (0, 0))],
        out_shape=[jax.ShapeDtypeStruct((8, t), jnp.int32), jax.ShapeDtypeStruct((8, t), jnp.int32),
                   jax.ShapeDtypeStruct((t, LANES), F32), jax.ShapeDtypeStruct((e, LANES), F32)],
        scratch_shapes=[pltpu.VMEM((e, LANES), F32)],
        compiler_params=_cparams("arbitrary"),
        name="router",
    )(lg_t, router_bias.reshape(e, 1))


SLOT_STRIDE = 8
ROW_GROUP = 8


def _slot_kernel(pstart_ref, e_ref, r_ref, o_ref):
    e = e_ref[...]
    acc = r_ref[...]
    for x in range(N_EXPERTS):
        acc = acc + jnp.where(e == x, pstart_ref[x], 0)
    tt = acc.shape[1]
    pad = jnp.concatenate([acc, jnp.zeros((LANES - acc.shape[0], tt), jnp.int32)], axis=0)
    o_ref[...] = jnp.transpose(pad)[:, :SLOT_STRIDE]


def _slot_index(pstart, e_idx, rank, tt):
    rows, t = e_idx.shape
    spec = pl.BlockSpec((rows, tt), lambda i, ps: (0, i))
    slot = pl.pallas_call(
        _slot_kernel,
        grid_spec=pltpu.PrefetchScalarGridSpec(
            num_scalar_prefetch=1, grid=(t // tt,), in_specs=[spec, spec],
            out_specs=pl.BlockSpec((tt, SLOT_STRIDE), lambda i, ps: (i, 0))),
        out_shape=jax.ShapeDtypeStruct((t, SLOT_STRIDE), jnp.int32),
        compiler_params=_cparams("parallel"),
        name="slot_index",
    )(pstart, e_idx, rank)
    return slot.reshape(t * SLOT_STRIDE)


def _row_copies(n_tokens, make_copy):
    def body(g, carry):
        base = g * (ROW_GROUP * SLOT_STRIDE)
        for s in range(ROW_GROUP):
            for j in range(TOP_K):
                make_copy(g, s, j, base + (s * SLOT_STRIDE + j)).start(priority=(s * TOP_K + j) % 2)
        return carry

    lax.fori_loop(0, n_tokens // ROW_GROUP, body, 0)


def _dispatch_kernel(pstart_ref, count_ref, slot_ref, hn_ref, xs_hbm, zero_ref, sem):
    td = hn_ref.shape[0] * ROW_GROUP
    bm = zero_ref.shape[0]

    def zero_fill(e, wait):
        end = pstart_ref[e] + count_ref[e]
        aligned = pl.multiple_of((end + 7) // 8 * 8, 8)
        for i in range(7):
            @pl.when(end + i < aligned)
            def _():
                cp = pltpu.make_async_copy(zero_ref.at[pl.ds(0, 1)], xs_hbm.at[pl.ds(end + i, 1)], sem)
                cp.wait() if wait else cp.start()
        cp = pltpu.make_async_copy(zero_ref, xs_hbm.at[pl.ds(aligned, bm)], sem)
        cp.wait() if wait else cp.start()

    @pl.when(pl.program_id(0) == 0)
    def _():
        zero_ref[...] = jnp.zeros(zero_ref.shape, zero_ref.dtype)

        def start(e, carry):
            zero_fill(e, False)
            return carry

        def wait(e, carry):
            zero_fill(e, True)
            return carry

        lax.fori_loop(0, N_EXPERTS, start, 0)
        lax.fori_loop(0, N_EXPERTS, wait, 0)

        last = N_EXPERTS - 1
        tail = (pstart_ref[last] + count_ref[last] + bm - 1) // bm

        def tail_copy(b):
            return pltpu.make_async_copy(zero_ref, xs_hbm.at[pl.ds(pl.multiple_of(b * bm, bm), bm)], sem)

        def tail_start(b, carry):
            tail_copy(b).start()
            return carry

        def tail_wait(b, carry):
            tail_copy(b).wait()
            return carry

        lax.fori_loop(tail, xs_hbm.shape[0] // bm, tail_start, 0)
        lax.fori_loop(tail, xs_hbm.shape[0] // bm, tail_wait, 0)

    _row_copies(td, lambda g, s, j, i: pltpu.make_async_copy(
        hn_ref.at[g, pl.ds(s, 1)], xs_hbm.at[pl.ds(slot_ref[i], 1)], sem))
    pltpu.make_async_copy(xs_hbm.at[pl.ds(0, td * TOP_K)], xs_hbm.at[pl.ds(0, td * TOP_K)], sem).wait()


def _dispatch(pstart, counts, slot, hn, n_slots, td, bm):
    t, d = hn.shape
    return pl.pallas_call(
        _dispatch_kernel,
        grid_spec=pltpu.PrefetchScalarGridSpec(
            num_scalar_prefetch=2, grid=(t // td,),
            in_specs=[pl.BlockSpec((td * SLOT_STRIDE,), lambda i, ps, cn: (i,), memory_space=pltpu.SMEM),
                      pl.BlockSpec((td // ROW_GROUP, ROW_GROUP, d), lambda i, ps, cn: (i, 0, 0))],
            out_specs=pl.BlockSpec(memory_space=pl.ANY),
            scratch_shapes=[pltpu.VMEM((bm, d), hn.dtype), pltpu.SemaphoreType.DMA(())]),
        out_shape=jax.ShapeDtypeStruct((n_slots, d), hn.dtype),
        compiler_params=_cparams("arbitrary"),
        name="dispatch",
    )(pstart, counts, slot, hn.reshape(t // ROW_GROUP, ROW_GROUP, d))


EXPERT_SUB = 256


def _expert_kernel(bexp_ref, bsrc_ref, nused_ref, xs_ref, wg_ref, wu_ref, wd_ref, y_ref):
    del bexp_ref, bsrc_ref
    i = pl.program_id(0)

    @pl.when(i < nused_ref[0])
    def _():
        wg = wg_ref[0].astype(BF16)
        wu = wu_ref[0].astype(BF16)
        wd = wd_ref[0].astype(BF16)
        for r0 in range(0, xs_ref.shape[0], EXPERT_SUB):
            rows = slice(r0, r0 + EXPERT_SUB)
            xb = jnp.concatenate(_unpack_bf16_pairs(xs_ref[rows, :]), axis=1).astype(BF16)
            g = jnp.dot(xb, wg, preferred_element_type=F32)
            u = jnp.dot(xb, wu, preferred_element_type=F32)
            act = (_silu(g) * u).astype(BF16)
            y_ref[rows, :] = _pack_bf16_pairs(jnp.dot(act, wd, preferred_element_type=F32))

    @pl.when(i >= nused_ref[0])
    def _():
        y_ref[...] = jnp.zeros(y_ref.shape, y_ref.dtype)


def _experts(block_exp, block_src, n_used, xs, w_gate, w_up, w_down, bm):
    n_slots, d = xs.shape
    w_map = lambda i, be, bs, nu: (be[i], 0, 0)
    return pl.pallas_call(
        _expert_kernel,
        grid_spec=pltpu.PrefetchScalarGridSpec(
            num_scalar_prefetch=3, grid=(n_slots // bm,),
            in_specs=[pl.BlockSpec((bm, d), lambda i, be, bs, nu: (bs[i], 0)),
                      pl.BlockSpec((1,) + w_gate.shape[1:], w_map),
                      pl.BlockSpec((1,) + w_up.shape[1:], w_map),
                      pl.BlockSpec((1,) + w_down.shape[1:], w_map)],
            out_specs=pl.BlockSpec((bm, d), lambda i, be, bs, nu: (i, 0))),
        out_shape=jax.ShapeDtypeStruct((n_slots, d), xs.dtype),
        compiler_params=_cparams("arbitrary"),
        name="experts",
    )(block_exp, block_src, n_used, xs, w_gate, w_up, w_down)


def _combine_kernel(slot_ref, wt_ref, base_ref, gt2_ref, gfin_ref, y_hbm, o_ref, buf, sem):
    tg = base_ref.shape[0]
    _row_copies(tg, lambda g, s, j, i: pltpu.make_async_copy(
        y_hbm.at[pl.ds(slot_ref[i], 1)], buf.at[j, g, pl.ds(s, 1)], sem))
    pltpu.make_async_copy(buf, buf, sem).wait()
    wt = wt_ref[...]
    lo = hi = None
    for j in range(TOP_K):
        ylo, yhi = _unpack_bf16_pairs(buf[j].reshape(tg, buf.shape[-1]))
        w = wt[:, j:j + 1]
        lo = w * ylo if lo is None else lo + w * ylo
        hi = w * yhi if hi is None else hi + w * yhi
    xf = base_ref[...] + gt2_ref[0] * jnp.concatenate([lo, hi], axis=1)
    o_ref[...] = xf * lax.rsqrt(jnp.mean(xf * xf, axis=-1, keepdims=True) + EPS) * gfin_ref[...]


def _combine(slot, wt, base, gt2, g_final, y, seq, tg):
    t, d = base.shape
    tiles = seq // tg
    return pl.pallas_call(
        _combine_kernel,
        grid=(t // tg,),
        in_specs=[pl.BlockSpec((tg * SLOT_STRIDE,), lambda i: (i,), memory_space=pltpu.SMEM),
                  pl.BlockSpec((tg, LANES), lambda i: (i, 0)),
                  pl.BlockSpec((tg, d), lambda i: (i, 0)),
                  pl.BlockSpec((1, 1, d), lambda i: (i // tiles, 0, 0)),
                  pl.BlockSpec((1, d), lambda i: (0, 0)),
                  pl.BlockSpec(memory_space=pl.ANY)],
        out_specs=pl.BlockSpec((tg, d), lambda i: (i, 0)),
        out_shape=jax.ShapeDtypeStruct((t, d), F32),
        scratch_shapes=[pltpu.VMEM((TOP_K, tg // ROW_GROUP, ROW_GROUP, y.shape[1]), y.dtype),
                        pltpu.SemaphoreType.DMA(())],
        compiler_params=_cparams("arbitrary"),
        name="combine",
    )(slot, wt, base, gt2, g_final.reshape(1, d), y)


EXPERT_BLOCK = 512
ROW_TILE = 512
ROUTER_TILE = 512
GATHER_TILE = 1024
SLOT_TILE = 4096


def _moe_plan(counts, n_tokens, bm):
    padded = (counts + bm - 1) // bm * bm
    pad_end = jnp.cumsum(padded)
    pad_start = (pad_end - padded).astype(jnp.int32)
    n_blocks = -(-(n_tokens * TOP_K) // bm) + N_EXPERTS + 1
    n_used = (pad_end[-1] // bm).astype(jnp.int32)
    block_src = jnp.minimum(jnp.arange(n_blocks, dtype=jnp.int32), jnp.maximum(n_used - 1, 0))
    block_exp = jnp.sum((pad_end[None, :] <= (block_src * bm)[:, None]).astype(jnp.int32), axis=1)
    block_exp = jnp.minimum(block_exp, N_EXPERTS - 1)
    return pad_start, block_exp, block_src, n_used.reshape(1), n_blocks * bm


def kernel(x, c, ctx, c_ctx, w_mod, b_mod, g_mix, g_ffn, w_in, w_out, dn_conv_w, dn_a_log, dn_dt_bias,
           dn_norm_g, cf_dw_w, cf_dw_b, cf_ln_g, cf_ln_b, router_w, router_bias, exp_w_gate, exp_w_up,
           exp_w_down, sh_w_gate, sh_w_up, sh_w_down, g_final):
    assert w_mod.shape[0] == 1, "single-layer block: the context stream is never re-read"
    bsz, seq, d = x.shape
    t = bsz * seq
    x2 = x.reshape(t, d)
    st = _mixer_front(x, c, ctx, c_ctx, w_mod[0], b_mod[0], g_mix[0], w_in[0], dn_conv_w[0], dn_a_log[0],
                      dn_dt_bias[0])
    _, _, gt1, sh2, sc2, gt2 = st['lat']
    cfo = _cf_conv(st['y'], cf_dw_w[0], cf_dw_b[0], cf_ln_g[0], cf_ln_b[0], bsz)
    wsgu = jnp.concatenate([sh_w_gate[0], sh_w_up[0]], axis=1).astype(BF16)
    base, hn, lg_t = _mixer_out(st['o_f'], st['o_b'], st['sz'], cfo, x2, gt1, sh2, sc2, gt2,
                                dn_norm_g[0].reshape(1, DN_D), w_out[0].astype(BF16), g_ffn[0].reshape(1, d),
                                router_w[0].T, wsgu, sh_w_down[0].astype(BF16), seq, min(seq, ROW_TILE))
    e_idx, rank, wt, cnt = _router(lg_t, router_bias[0], min(t, ROUTER_TILE))
    counts = cnt[:, 0].astype(jnp.int32)
    pstart, block_exp, block_src, n_used, n_slots = _moe_plan(counts, t, EXPERT_BLOCK)
    slot = _slot_index(pstart, e_idx, rank, min(t, SLOT_TILE))
    xs = _dispatch(pstart, counts, slot, hn, n_slots, min(t, GATHER_TILE), EXPERT_BLOCK)
    y = _experts(block_exp, block_src, n_used, xs, exp_w_gate[0], exp_w_up[0], exp_w_down[0], EXPERT_BLOCK)
    out = _combine(slot, wt, base, gt2, g_final, y, seq, min(seq, GATHER_TILE))
    return out.reshape(bsz, seq, d)
```

```python
import functools

import jax
import jax.numpy as jnp
from jax import lax
from jax.experimental import pallas as pl
from jax.experimental.pallas import tpu as pltpu

F32 = jnp.float32
U32 = jnp.uint32
BF16 = jnp.bfloat16
HIGHEST = lax.Precision.HIGHEST

EPS = 1e-6
LANES = 128
GRID_W = 64
DN_HEADS = 4
DN_D = 128
DN_QK = DN_HEADS * DN_D
DN_WIDTH = DN_HEADS * DN_D
DN_CONV_DIM = 2 * DN_QK + DN_WIDTH
SHORT_CONV = 7
CHUNK = 64
CF_CH = 512
CF_K = 31
N_EXPERTS = 64
TOP_K = 6
N_GROUPS = 8
GROUP_SIZE = N_EXPERTS // N_GROUPS
TOPK_GROUPS = 4
ROUTED_SCALE = 2.5
HALO = 8
VMEM_LIMIT = 56 * 1024 * 1024


def _cparams(*sem):
    return pltpu.CompilerParams(dimension_semantics=sem, vmem_limit_bytes=VMEM_LIMIT)


def _silu(v):
    return v * jax.nn.sigmoid(v)


def _bdot(a, b):
    return jnp.dot(a.astype(BF16), b.astype(BF16), preferred_element_type=F32)


def _pack_bf16_pairs(v):
    n = v.shape[1] // 2
    lo = lax.bitcast_convert_type(v[:, :n].astype(BF16).astype(F32), U32)
    hi = lax.bitcast_convert_type(v[:, n:].astype(BF16).astype(F32), U32)
    return (lo >> 16) | (hi & jnp.uint32(0xFFFF0000))


def _unpack_bf16_pairs(w):
    lo = lax.bitcast_convert_type(w << 16, F32)
    hi = lax.bitcast_convert_type(w & jnp.uint32(0xFFFF0000), F32)
    return lo, hi


def _mod_kernel(c_ref, w_ref, b_ref, o_ref):
    o_ref[...] = jnp.dot(_silu(c_ref[...]), w_ref[...], preferred_element_type=F32,
                         precision=HIGHEST) + b_ref[...]


def _modulation(cc, w_mod, b_mod):
    rows, d = cc.shape
    n = w_mod.shape[1]
    tn = 1024
    return pl.pallas_call(
        _mod_kernel,
        grid=(n // tn,),
        in_specs=[pl.BlockSpec((rows, d), lambda j: (0, 0)),
                  pl.BlockSpec((d, tn), lambda j: (0, j)),
                  pl.BlockSpec((1, tn), lambda j: (0, j))],
        out_specs=pl.BlockSpec((rows, tn), lambda j: (0, j)),
        out_shape=jax.ShapeDtypeStruct((rows, n), F32),
        compiler_params=_cparams("parallel"),
        name="modulation",
    )(cc, w_mod, b_mod.reshape(1, n))


def _chunk_cumsum(g, reverse):
    n = g.shape[0]
    pos = lax.broadcasted_iota(jnp.int32, g.shape, 0) % CHUNK
    s = 1
    while s < CHUNK:
        if reverse:
            shifted = pltpu.roll(g, n - s, 0)
            ok = pos < CHUNK - s
        else:
            shifted = pltpu.roll(g, s, 0)
            ok = pos >= s
        g = g + jnp.where(ok, shifted, 0.0)
        s *= 2
    return g


def _inproj_kernel(latent, x_ref, sh_ref, sc_ref, g_ref, wqkv_ref, wba_ref, alog_ref, dtb_ref, *rest):
    if latent:
        wz_ref, wcf_ref, qkv_o, bgc_o, bgct_o, sz_o, y_o = rest
    else:
        qkv_o, bgc_o, bgct_o = rest
    x = x_ref[...]
    xn = x * lax.rsqrt(jnp.mean(x * x, axis=-1, keepdims=True) + EPS) * g_ref[...]
    hb = (xn * (1.0 + sc_ref[0]) + sh_ref[0]).astype(BF16)
    qkv_o[...] = jnp.dot(hb, wqkv_ref[...], preferred_element_type=F32)

    ba = jnp.dot(hb, wba_ref[...], preferred_element_type=F32)
    col = lax.broadcasted_iota(jnp.int32, ba.shape, 1)
    is_beta = (col % 8) < DN_HEADS
    g = -jnp.exp(alog_ref[...]) * jax.nn.softplus(ba + dtb_ref[...])
    g = jnp.where(is_beta, 0.0, g)
    gc = jnp.where(col < 8, _chunk_cumsum(g, False), _chunk_cumsum(g, True))
    bgc = jnp.where(is_beta, jax.nn.sigmoid(ba), gc)
    bgc_o[...] = bgc
    for c in range(bgc.shape[0] // CHUNK):
        bgct_o[c] = jnp.transpose(bgc[c * CHUNK:(c + 1) * CHUNK, :])[:16, :]

    if latent:
        sz_o[...] = _silu(jnp.dot(hb, wz_ref[...], preferred_element_type=F32))
        cf = jnp.dot(hb, wcf_ref[...], preferred_element_type=F32)
        y_o[...] = cf[:, :CF_CH] * jax.nn.sigmoid(cf[:, CF_CH:])


def _in_projection(x2, sh, sc, g_mix, wqkv, wba, alog_row, dtb_row, wz, wcf, rows_per_mod, tm):
    t, d = x2.shape
    latent = wz is not None
    tiles_per_mod = rows_per_mod // tm
    const = lambda i: (0, 0)
    mod_map = lambda i: (i // tiles_per_mod, 0, 0)
    row_map = lambda i: (i, 0)
    in_specs = [pl.BlockSpec((tm, d), row_map),
                pl.BlockSpec((1, 1, d), mod_map), pl.BlockSpec((1, 1, d), mod_map),
                pl.BlockSpec((1, d), const),
                pl.BlockSpec(wqkv.shape, const), pl.BlockSpec(wba.shape, const),
                pl.BlockSpec((1, LANES), const), pl.BlockSpec((1, LANES), const)]
    args = [x2, sh, sc, g_mix, wqkv, wba, alog_row, dtb_row]
    out_specs = [pl.BlockSpec((tm, DN_CONV_DIM), row_map), pl.BlockSpec((tm, LANES), row_map),
                 pl.BlockSpec((tm // CHUNK, 16, CHUNK), lambda i: (i, 0, 0))]
    out_shape = [jax.ShapeDtypeStruct((t, DN_CONV_DIM), F32), jax.ShapeDtypeStruct((t, LANES), F32),
                 jax.ShapeDtypeStruct((t // CHUNK, 16, CHUNK), F32)]
    if latent:
        in_specs += [pl.BlockSpec(wz.shape, const), pl.BlockSpec(wcf.shape, const)]
        args += [wz, wcf]
        out_specs += [pl.BlockSpec((tm, DN_WIDTH), row_map), pl.BlockSpec((tm, CF_CH), row_map)]
        out_shape += [jax.ShapeDtypeStruct((t, DN_WIDTH), F32), jax.ShapeDtypeStruct((t, CF_CH), F32)]
    return pl.pallas_call(
        functools.partial(_inproj_kernel, latent),
        grid=(t // tm,),
        in_specs=in_specs, out_specs=out_specs, out_shape=out_shape,
        compiler_params=_cparams("parallel"),
        name="in_projection_latent" if latent else "in_projection_context",
    )(*args)


CONV_ROWS = 64


def _shortconv_kernel(tiles_per_seq, prev_ref, cur_ref, next_ref, w_ref, o_ref, ext_ref):
    i = pl.program_id(0)
    tm = cur_ref.shape[0]
    pos = i % tiles_per_seq
    ext_ref[0:HALO, :] = jnp.where(pos == 0, 0.0, prev_ref[...])
    ext_ref[HALO:HALO + tm, :] = cur_ref[...]
    ext_ref[HALO + tm:, :] = jnp.where(pos == tiles_per_seq - 1, 0.0, next_ref[...])
    reach = SHORT_CONV // 2

    for r0 in range(0, tm, CONV_ROWS):
        for cb in range(DN_CONV_DIM // LANES):
            cols = slice(cb * LANES, (cb + 1) * LANES)
            acc = jnp.zeros((CONV_ROWS, LANES), F32)
            for k in range(SHORT_CONV):
                s = r0 + HALO - reach + k
                acc = acc + w_ref[k:k + 1, cols] * ext_ref[s:s + CONV_ROWS, cols]
            a = _silu(acc)
            if cb < 2 * DN_HEADS:
                a = a * lax.rsqrt(jnp.sum(a * a, axis=-1, keepdims=True) + EPS)
            if cb < DN_HEADS:
                a = a * (DN_D ** -0.5)
            o_ref[r0:r0 + CONV_ROWS, cols] = a


def _short_conv(qkv, conv_w, seq_len, tm):
    t, c = qkv.shape
    tiles_per_seq = seq_len // tm
    hb = tm // HALO
    n_halo_blocks = t // HALO
    return pl.pallas_call(
        functools.partial(_shortconv_kernel, tiles_per_seq),
        grid=(t // tm,),
        in_specs=[pl.BlockSpec((HALO, c), lambda i: (jnp.maximum(i * hb - 1, 0), 0)),
                  pl.BlockSpec((tm, c), lambda i: (i, 0)),
                  pl.BlockSpec((HALO, c), lambda i: (jnp.minimum((i + 1) * hb, n_halo_blocks - 1), 0)),
                  pl.BlockSpec(conv_w.shape, lambda i: (0, 0))],
        out_specs=pl.BlockSpec((tm, c), lambda i: (i, 0)),
        out_shape=jax.ShapeDtypeStruct((t, c), F32),
        scratch_shapes=[pltpu.VMEM((tm + 2 * HALO, c), F32)],
        compiler_params=_cparams("parallel"),
        name="short_conv",
    )(qkv, qkv, qkv, conv_w)


DELTA_STEPS = 2
INV_BLOCK = 16


def _delta_pre(refs, ci_f, ci_b, masks, blocks, eye):
    qkv_ref, bgc_ref, bgct_ref = refs
    chains = range(2 * DN_HEADS * len(ci_f))
    q, k, v, beta, gcol, decay, eg, gl = [], [], [], [], [], [], [], []
    for d, ci in [(d, c[step]) for step in range(len(ci_f)) for d, c in enumerate((ci_f, ci_b))]:
        rows = pl.ds(pl.multiple_of(ci * CHUNK, CHUNK), CHUNK)
        bg = bgc_ref[rows, :]
        bgt = bgct_ref[ci]
        for h in range(DN_HEADS):
            q.append(qkv_ref[rows, h * DN_D:(h + 1) * DN_D])
            k.append(qkv_ref[rows, DN_QK + h * DN_D:DN_QK + (h + 1) * DN_D])
            v.append(qkv_ref[rows, 2 * DN_QK + h * DN_D:2 * DN_QK + (h + 1) * DN_D])
            jb = d * 8 + h
            jg = d * 8 + DN_HEADS + h
            beta.append(bg[:, jb:jb + 1])
            gc = bg[:, jg:jg + 1]
            gcol.append(gc)
            decay.append(jnp.exp(jnp.where(masks[d][0], gc - bgt[jg:jg + 1, :], -jnp.inf)))
            eg.append(jnp.exp(gc))
            gl.append(gc[CHUNK - 1:CHUNK] if d == 0 else gc[0:1])

    contract_last = (((1,), (1,)), ((), ()))
    gram = [lax.dot_general(jnp.concatenate([k[n], q[n]], axis=0).astype(BF16), k[n].astype(BF16),
                            contract_last, preferred_element_type=F32) for n in chains]
    lmat = [jnp.where(masks[n // DN_HEADS % 2][1], gram[n][:CHUNK] * decay[n], 0.0) * beta[n] for n in chains]
    same_small, same_half = blocks
    dmat = [jnp.where(same_small, lmat[n], 0.0) for n in chains]
    tinv = [eye - dmat[n] for n in chains]
    p = [_bdot(dmat[n], dmat[n]) for n in chains]
    n_sq = 2
    while n_sq < INV_BLOCK // 2:
        x = [_bdot(jnp.concatenate([tinv[n], p[n]], axis=0), p[n]) for n in chains]
        tinv = [tinv[n] + x[n][:CHUNK] for n in chains]
        p = [x[n][CHUNK:] for n in chains]
        n_sq *= 2
    tinv = [tinv[n] + _bdot(tinv[n], p[n]) for n in chains]
    for off_mask in (jnp.logical_and(same_half, jnp.logical_not(same_small)), jnp.logical_not(same_half)):
        x = [_bdot(jnp.where(off_mask, lmat[n], 0.0), tinv[n]) for n in chains]
        tinv = [tinv[n] - _bdot(tinv[n], x[n]) for n in chains]
    uw = [_bdot(tinv[n], jnp.concatenate([v[n] * beta[n], k[n] * (beta[n] * eg[n])], axis=1)) for n in chains]

    return [(uw[n][:, :DN_D], uw[n][:, DN_D:], q[n] * eg[n], k[n] * jnp.exp(gl[n] - gcol[n]),
             gram[n][CHUNK:] * decay[n], jnp.exp(gl[n])) for n in chains]


def _delta_rec(pre, s_ref, want_out):
    chains = range(len(pre))
    contract_first = (((0,), (0,)), ((), ()))
    s = [s_ref[n] for n in chains]
    if want_out:
        wq = [_bdot(jnp.concatenate([pre[n][1], pre[n][2]], axis=0), s[n]) for n in chains]
        ws = [wq[n][:CHUNK] for n in chains]
    else:
        ws = [_bdot(pre[n][1], s[n]) for n in chains]
    vb = [(pre[n][0] - ws[n]).astype(BF16) for n in chains]
    for n in chains:
        s_ref[n] = s[n] * pre[n][5] + lax.dot_general(pre[n][3].astype(BF16), vb[n], contract_first,
                                                      preferred_element_type=F32)
    if not want_out:
        return None
    return [wq[n][CHUNK:] + jnp.dot(pre[n][4].astype(BF16), vb[n], preferred_element_type=F32)
            for n in chains]


def _delta_kernel(qc_ref, bc_ref, btc_ref, qx_ref, bx_ref, btx_ref, of_ref, ob_ref, s_ref, p_ref):
    s_ref[...] = jnp.zeros(s_ref.shape, F32)
    row = lax.broadcasted_iota(jnp.int32, (CHUNK, CHUNK), 0)
    col = lax.broadcasted_iota(jnp.int32, (CHUNK, CHUNK), 1)
    masks = ((row >= col, row > col), (row <= col, row < col))
    eye = (row == col).astype(F32)
    blocks = (row // INV_BLOCK == col // INV_BLOCK, row // (CHUNK // 2) == col // (CHUNK // 2))
    per_step = 2 * DN_HEADS
    n_chains = DELTA_STEPS * per_step

    def stage(pre):
        for n in range(n_chains):
            u, w, qe, kd, amat, egl = pre[n]
            for j, val in enumerate((u, w, qe, kd)):
                p_ref[n, j] = val
            p_ref[n, 4] = jnp.concatenate([amat, jnp.broadcast_to(egl, (CHUNK, DN_D - CHUNK))], axis=1)

    def staged():
        return [(p_ref[n, 0], p_ref[n, 1], p_ref[n, 2], p_ref[n, 3], p_ref[n, 4][:, :CHUNK],
                 p_ref[n, 4][0:1, CHUNK:CHUNK + 1]) for n in range(n_chains)]

    def scan(refs, want_out):
        n_chunks = refs[0].shape[0] // CHUNK
        assert n_chunks % DELTA_STEPS == 0

        def pre(first):
            fwd = [first + s for s in range(DELTA_STEPS)]
            return _delta_pre(refs, fwd, [n_chunks - 1 - c for c in fwd], masks, blocks, eye)

        stage(pre(0))

        def update(i, cur):
            for s in range(DELTA_STEPS):
                o = _delta_rec(cur[s * per_step:(s + 1) * per_step], s_ref, want_out)
                if want_out:
                    rf = pl.ds(pl.multiple_of((i + s) * CHUNK, CHUNK), CHUNK)
                    rb = pl.ds(pl.multiple_of((n_chunks - 1 - i - s) * CHUNK, CHUNK), CHUNK)
                    for h in range(DN_HEADS):
                        cols = slice(h * DN_D, (h + 1) * DN_D)
                        of_ref[rf, cols] = o[h]
                        ob_ref[rb, cols] = o[DN_HEADS + h]

        def body(it, carry):
            i = it * DELTA_STEPS
            cur = staged()
            new = pre(i + DELTA_STEPS)
            update(i, cur)
            stage(new)
            return carry

        n_iter = n_chunks // DELTA_STEPS
        lax.fori_loop(0, n_iter - 1, body, 0)
        update((n_iter - 1) * DELTA_STEPS, staged())

    scan((qc_ref, bc_ref, btc_ref), False)
    scan((qx_ref, bx_ref, btx_ref), True)


def _delta_scan(qkv_c, bgc_c, bgct_c, qkv_x, bgc_x, bgct_x, bsz):
    lc = qkv_c.shape[0] // bsz
    lx = qkv_x.shape[0] // bsz
    row_map = lambda b: (b, 0)
    out = jax.ShapeDtypeStruct((bsz * lx, DN_WIDTH), F32)
    return pl.pallas_call(
        _delta_kernel,
        grid=(bsz,),
        in_specs=[pl.BlockSpec((lc, DN_CONV_DIM), row_map), pl.BlockSpec((lc, LANES), row_map),
                  pl.BlockSpec((lc // CHUNK, 16, CHUNK), lambda b: (b, 0, 0)),
                  pl.BlockSpec((lx, DN_CONV_DIM), row_map), pl.BlockSpec((lx, LANES), row_map),
                  pl.BlockSpec((lx // CHUNK, 16, CHUNK), lambda b: (b, 0, 0))],
        out_specs=[pl.BlockSpec((lx, DN_WIDTH), row_map), pl.BlockSpec((lx, DN_WIDTH), row_map)],
        out_shape=[out, out],
        scratch_shapes=[pltpu.VMEM((2 * DN_HEADS, DN_D, DN_D), F32),
                        pltpu.VMEM((DELTA_STEPS * 2 * DN_HEADS, 5, CHUNK, DN_D), F32)],
        compiler_params=_cparams("parallel"),
        name="delta_scan",
    )(qkv_c, bgc_c, bgct_c, qkv_x, bgc_x, bgct_x)


def _pad_rows(a, rows):
    return jnp.pad(a, ((0, rows - a.shape[0]), (0, 0)))


def _mixer_front(x, c, ctx, c_ctx, w_mod, b_mod, g_mix, w_in, conv_w, a_log, dt_bias):
    bsz, seq, d = x.shape
    lc = ctx.shape[1]
    off_z = DN_CONV_DIM
    off_ba = off_z + DN_WIDTH
    off_cf = off_ba + 4 * DN_HEADS

    cc = _pad_rows(jnp.concatenate([c, c_ctx[None, :]], axis=0), -(-(bsz + 1) // 8) * 8)
    mod = _modulation(cc, w_mod, b_mod)
    mods = [mod[:, j * d:(j + 1) * d] for j in range(6)]
    lat = [m[:bsz].reshape(bsz, 1, d) for m in mods]
    con = [m[bsz:bsz + 1].reshape(1, 1, d) for m in mods]

    wqkv = w_in[:, :off_z].astype(BF16)
    wz = w_in[:, off_z:off_ba].astype(BF16)
    wba = jnp.pad(w_in[:, off_ba:off_cf], ((0, 0), (0, LANES - 4 * DN_HEADS))).astype(BF16)
    wcf = w_in[:, off_cf:].astype(BF16)
    zeros4 = jnp.zeros((2, DN_HEADS), F32)
    alog_row = jnp.pad(jnp.concatenate([zeros4, a_log], axis=1).reshape(1, -1), ((0, 0), (0, LANES - 16)))
    dtb_row = jnp.pad(jnp.concatenate([zeros4, dt_bias], axis=1).reshape(1, -1), ((0, 0), (0, LANES - 16)))
    g_row = g_mix.reshape(1, d)

    qkv_c, bgc_c, bgct_c = _in_projection(ctx.reshape(bsz * lc, d), con[0], con[1], g_row, wqkv, wba,
                                          alog_row, dtb_row, None, None, bsz * lc, min(lc, 512))
    qkv_x, bgc_x, bgct_x, sz, y = _in_projection(x.reshape(bsz * seq, d), lat[0], lat[1], g_row, wqkv, wba,
                                                 alog_row, dtb_row, wz, wcf, seq, min(seq, 512))
    qkvn_c = _short_conv(qkv_c, conv_w, lc, min(lc, 512))
    qkvn_x = _short_conv(qkv_x, conv_w, seq, min(seq, 512))
    o_f, o_b = _delta_scan(qkvn_c, bgc_c, bgct_c, qkvn_x, bgc_x, bgct_x, bsz)
    return dict(mod=mod, lat=lat, qkv_x=qkv_x, qkv_c=qkv_c, sz=sz, y=y, qkvn_x=qkvn_x, bgc_x=bgc_x,
                bgct_x=bgct_x, o_f=o_f, o_b=o_b)


CF_ROWS = 32


def _cfconv_kernel(y_ref, w_ref, b_ref, lng_ref, lnb_ref, o_ref, pad_ref):
    n = y_ref.shape[0]
    reach = (CF_K // 2) * GRID_W
    pad_ref[0:reach, :] = jnp.zeros((reach, CF_CH), F32)
    pad_ref[reach:reach + n, :] = y_ref[...]
    pad_ref[reach + n:, :] = jnp.zeros((reach, CF_CH), F32)

    def body(r, carry):
        r0 = pl.multiple_of(r * CF_ROWS, CF_ROWS)
        acc = jnp.zeros((CF_ROWS, CF_CH), F32)
        for k in range(CF_K):
            acc = acc + w_ref[k:k + 1, :] * pad_ref[pl.ds(r0 + k * GRID_W, CF_ROWS), :]
        acc = acc + b_ref[...]
        mu = jnp.mean(acc, axis=-1, keepdims=True)
        xc = acc - mu
        var = jnp.mean(xc * xc, axis=-1, keepdims=True)
        o_ref[pl.ds(r0, CF_ROWS), :] = _silu(xc * lax.rsqrt(var + EPS) * lng_ref[...] + lnb_ref[...])
        return carry

    lax.fori_loop(0, n // CF_ROWS, body, 0, unroll=4)


def _cf_conv(y, dw_w, dw_b, ln_g, ln_b, bsz):
    t, ch = y.shape
    n = t // bsz
    reach = (CF_K // 2) * GRID_W
    const = lambda b: (0, 0)
    return pl.pallas_call(
        _cfconv_kernel,
        grid=(bsz,),
        in_specs=[pl.BlockSpec((n, ch), lambda b: (b, 0)), pl.BlockSpec(dw_w.shape, const),
                  pl.BlockSpec((1, ch), const), pl.BlockSpec((1, ch), const), pl.BlockSpec((1, ch), const)],
        out_specs=pl.BlockSpec((n, ch), lambda b: (b, 0)),
        out_shape=jax.ShapeDtypeStruct((t, ch), F32),
        scratch_shapes=[pltpu.VMEM((n + 2 * reach, ch), F32)],
        compiler_params=_cparams("parallel"),
        name="conformer_conv",
    )(y, dw_w, dw_b.reshape(1, ch), ln_g.reshape(1, ch), ln_b.reshape(1, ch))


def _mixout_kernel(of_ref, ob_ref, sz_ref, cf_ref, x_ref, gt1_ref, sh2_ref, sc2_ref, gt2_ref, ng_ref, wo_ref,
                   gffn_ref, rwt_ref, wsgu_ref, wsd_ref, base_o, hn_o, lg_o):
    contract_last = (((1,), (1,)), ((), ()))
    rw = rwt_ref[...]
    rw_hi = rw.astype(BF16)
    rw_lo = (rw - rw_hi.astype(F32)).astype(BF16)
    rw_both = jnp.concatenate([rw_hi, rw_lo], axis=0)
    tm = x_ref.shape[0]
    for r0 in range(0, tm, tm // 2):
        rows = slice(r0, r0 + tm // 2)
        o = of_ref[rows, :] + ob_ref[rows, :]
        parts = []
        for h in range(DN_HEADS):
            oh = o[:, h * DN_D:(h + 1) * DN_D]
            parts.append(oh * lax.rsqrt(jnp.mean(oh * oh, axis=-1, keepdims=True) + EPS) * ng_ref[...])
        dn = jnp.concatenate(parts, axis=1) * sz_ref[rows, :]
        heads = jnp.concatenate([dn, cf_ref[rows, :]], axis=1).astype(BF16)
        x1 = x_ref[rows, :] + gt1_ref[0] * jnp.dot(heads, wo_ref[...], preferred_element_type=F32)
        hn = (x1 * lax.rsqrt(jnp.mean(x1 * x1, axis=-1, keepdims=True) + EPS) * gffn_ref[...]
              * (1.0 + sc2_ref[0]) + sh2_ref[0])
        hn_o[rows, :] = _pack_bf16_pairs(hn)
        hn_hi = hn.astype(BF16)
        hn_lo = (hn - hn_hi.astype(F32)).astype(BF16)
        both = lax.dot_general(rw_both, hn_hi, contract_last, preferred_element_type=F32)
        lg_o[:, rows] = (both[:N_EXPERTS] + both[N_EXPERTS:]
                         + lax.dot_general(rw_hi, hn_lo, contract_last, preferred_element_type=F32))
        gu = jnp.dot(hn_hi, wsgu_ref[...], preferred_element_type=F32)
        ds = gu.shape[1] // 2
        act = (_silu(gu[:, :ds]) * gu[:, ds:]).astype(BF16)
        base_o[rows, :] = x1 + gt2_ref[0] * jnp.dot(act, wsd_ref[...], preferred_element_type=F32)


def _mixer_out(o_f, o_b, sz, cfo, x2, gt1, sh2, sc2, gt2, norm_g, w_out, g_ffn, rwt, wsgu, wsd, seq, tm):
    t, d = x2.shape
    tiles = seq // tm
    const = lambda i: (0, 0)
    row_map = lambda i: (i, 0)
    mod_map = lambda i: (i // tiles, 0, 0)
    half = pl.BlockSpec((tm, DN_WIDTH), row_map)
    mod_spec = pl.BlockSpec((1, 1, d), mod_map)
    return pl.pallas_call(
        _mixout_kernel,
        grid=(t // tm,),
        in_specs=[half, half, half, half, pl.BlockSpec((tm, d), row_map),
                  mod_spec, mod_spec, mod_spec, mod_spec,
                  pl.BlockSpec((1, DN_D), const), pl.BlockSpec(w_out.shape, const), pl.BlockSpec((1, d), const),
                  pl.BlockSpec(rwt.shape, const), pl.BlockSpec(wsgu.shape, const), pl.BlockSpec(wsd.shape, const)],
        out_specs=[pl.BlockSpec((tm, d), row_map), pl.BlockSpec((tm, d // 2), row_map),
                   pl.BlockSpec((N_EXPERTS, tm), lambda i: (0, i))],
        out_shape=[jax.ShapeDtypeStruct((t, d), F32), jax.ShapeDtypeStruct((t, d // 2), U32),
                   jax.ShapeDtypeStruct((N_EXPERTS, t), F32)],
        compiler_params=_cparams("parallel"),
        name="mixer_out",
    )(o_f, o_b, sz, cfo, x2, gt1, sh2, sc2, gt2, norm_g, w_out, g_ffn, rwt, wsgu, wsd)


def _first_argmax(vals, idx, sentinel):
    m = jnp.max(vals, axis=0, keepdims=True)
    return m, jnp.min(jnp.where(vals == m, idx, sentinel), axis=0, keepdims=True)


def _router_kernel(lg_ref, bias_ref, e_o, r_o, wt_o, cnt_o, carry_ref):
    @pl.when(pl.program_id(0) == 0)
    def _():
        carry_ref[...] = jnp.zeros(carry_ref.shape, F32)

    tt = lg_ref.shape[1]
    neg = -jnp.inf
    scores = jax.nn.sigmoid(lg_ref[...])
    sel = scores + bias_ref[...]
    sub = lax.broadcasted_iota(jnp.int32, (GROUP_SIZE, tt), 0)

    rows = []
    for g in range(N_GROUPS):
        sg = sel[g * GROUP_SIZE:(g + 1) * GROUP_SIZE]
        m1, first = _first_argmax(sg, sub, GROUP_SIZE)
        m2 = jnp.max(jnp.where(sub == first, neg, sg), axis=0, keepdims=True)
        rows.append(m1 + m2)
    cur = jnp.concatenate(rows, axis=0)
    gidx = lax.broadcasted_iota(jnp.int32, (N_GROUPS, tt), 0)
    keep = gidx < 0
    for _ in range(TOPK_GROUPS):
        _, a = _first_argmax(cur, gidx, N_GROUPS)
        pick = gidx == a
        keep = jnp.logical_or(keep, pick)
        cur = jnp.where(pick, neg, cur)
    keep_e = jnp.concatenate([jnp.broadcast_to(keep[g:g + 1], (GROUP_SIZE, tt)) for g in range(N_GROUPS)],
                             axis=0)
    masked = jnp.where(keep_e, sel, neg)

    eidx = lax.broadcasted_iota(jnp.int32, (N_EXPERTS, tt), 0)
    e_rows, w_rows, picks = [], [], []
    for _ in range(TOP_K):
        _, a = _first_argmax(masked, eidx, N_EXPERTS)
        pick = eidx == a
        e_rows.append(a)
        w_rows.append(jnp.sum(jnp.where(pick, scores, 0.0), axis=0, keepdims=True))
        picks.append(pick)
        masked = jnp.where(pick, neg, masked)
    onehot = sum(p.astype(F32) for p in picks)
    scale = ROUTED_SCALE / sum(w_rows)

    tri = (lax.broadcasted_iota(jnp.int32, (tt, tt), 0) < lax.broadcasted_iota(jnp.int32, (tt, tt), 1))
    cum = jnp.dot(onehot.astype(BF16), tri.astype(BF16), preferred_element_type=F32) + carry_ref[:, 0:1]
    r_rows = [jnp.sum(jnp.where(p, cum, 0.0), axis=0, keepdims=True).astype(jnp.int32) for p in picks]
    carry_ref[...] = carry_ref[...] + jnp.sum(onehot, axis=1, keepdims=True)
    cnt_o[...] = carry_ref[...]

    fill = 8 - TOP_K
    e_o[...] = jnp.concatenate(e_rows + [jnp.zeros((fill, tt), jnp.int32)], axis=0)
    r_o[...] = jnp.concatenate(r_rows + [jnp.zeros((fill, tt), jnp.int32)], axis=0)
    w_pad = jnp.concatenate([w * scale for w in w_rows] + [jnp.zeros((LANES - TOP_K, tt), F32)], axis=0)
    wt_o[...] = jnp.transpose(w_pad)


def _router(lg_t, router_bias, tt):
    e, t = lg_t.shape
    idx_spec = pl.BlockSpec((8, tt), lambda i: (0, i))
    return pl.pallas_call(
        _router_kernel,
        grid=(t // tt,),
        in_specs=[pl.BlockSpec((e, tt), lambda i: (0, i)), pl.BlockSpec((e, 1), lambda i: (0, 0))],
        out_specs=[idx_spec, idx_spec, pl.BlockSpec((tt, LANES), lambda i: (i, 0)),
                   pl.BlockSpec((e, LANES), lambda i: (0, 0))],
        out_shape=[jax.ShapeDtypeStruct((8, t), jnp.int32), jax.ShapeDtypeStruct((8, t), jnp.int32),
                   jax.ShapeDtypeStruct((t, LANES), F32), jax.ShapeDtypeStruct((e, LANES), F32)],
        scratch_shapes=[pltpu.VMEM((e, LANES), F32)],
        compiler_params=_cparams("arbitrary"),
        name="router",
    )(lg_t, router_bias.reshape(e, 1))


SLOT_STRIDE = 8
ROW_GROUP = 8


def _slot_kernel(pstart_ref, e_ref, r_ref, o_ref):
    e = e_ref[...]
    acc = r_ref[...]
    for x in range(N_EXPERTS):
        acc = acc + jnp.where(e == x, pstart_ref[x], 0)
    tt = acc.shape[1]
    pad = jnp.concatenate([acc, jnp.zeros((LANES - acc.shape[0], tt), jnp.int32)], axis=0)
    o_ref[...] = jnp.transpose(pad)[:, :SLOT_STRIDE]


def _slot_index(pstart, e_idx, rank, tt):
    rows, t = e_idx.shape
    spec = pl.BlockSpec((rows, tt), lambda i, ps: (0, i))
    slot = pl.pallas_call(
        _slot_kernel,
        grid_spec=pltpu.PrefetchScalarGridSpec(
            num_scalar_prefetch=1, grid=(t // tt,), in_specs=[spec, spec],
            out_specs=pl.BlockSpec((tt, SLOT_STRIDE), lambda i, ps: (i, 0))),
        out_shape=jax.ShapeDtypeStruct((t, SLOT_STRIDE), jnp.int32),
        compiler_params=_cparams("parallel"),
        name="slot_index",
    )(pstart, e_idx, rank)
    return slot.reshape(t * SLOT_STRIDE)


def _row_copies(n_tokens, make_copy):
    def body(g, carry):
        base = g * (ROW_GROUP * SLOT_STRIDE)
        for s in range(ROW_GROUP):
            for j in range(TOP_K):
                make_copy(g, s, j, base + (s * SLOT_STRIDE + j)).start(priority=(s * TOP_K + j) % 2)
        return carry

    lax.fori_loop(0, n_tokens // ROW_GROUP, body, 0)


def _dispatch_kernel(pstart_ref, count_ref, slot_ref, hn_ref, xs_hbm, zero_ref, sem):
    td = hn_ref.shape[0] * ROW_GROUP
    bm = zero_ref.shape[0]

    def zero_fill(e, wait):
        end = pstart_ref[e] + count_ref[e]
        aligned = pl.multiple_of((end + 7) // 8 * 8, 8)
        for i in range(7):
            @pl.when(end + i < aligned)
            def _():
                cp = pltpu.make_async_copy(zero_ref.at[pl.ds(0, 1)], xs_hbm.at[pl.ds(end + i, 1)], sem)
                cp.wait() if wait else cp.start()
        cp = pltpu.make_async_copy(zero_ref, xs_hbm.at[pl.ds(aligned, bm)], sem)
        cp.wait() if wait else cp.start()

    @pl.when(pl.program_id(0) == 0)
    def _():
        zero_ref[...] = jnp.zeros(zero_ref.shape, zero_ref.dtype)

        def start(e, carry):
            zero_fill(e, False)
            return carry

        def wait(e, carry):
            zero_fill(e, True)
            return carry

        lax.fori_loop(0, N_EXPERTS, start, 0)
        lax.fori_loop(0, N_EXPERTS, wait, 0)

        last = N_EXPERTS - 1
        tail = (pstart_ref[last] + count_ref[last] + bm - 1) // bm

        def tail_copy(b):
            return pltpu.make_async_copy(zero_ref, xs_hbm.at[pl.ds(pl.multiple_of(b * bm, bm), bm)], sem)

        def tail_start(b, carry):
            tail_copy(b).start()
            return carry

        def tail_wait(b, carry):
            tail_copy(b).wait()
            return carry

        lax.fori_loop(tail, xs_hbm.shape[0] // bm, tail_start, 0)
        lax.fori_loop(tail, xs_hbm.shape[0] // bm, tail_wait, 0)

    _row_copies(td, lambda g, s, j, i: pltpu.make_async_copy(
        hn_ref.at[g, pl.ds(s, 1)], xs_hbm.at[pl.ds(slot_ref[i], 1)], sem))
    pltpu.make_async_copy(xs_hbm.at[pl.ds(0, td * TOP_K)], xs_hbm.at[pl.ds(0, td * TOP_K)], sem).wait()


def _dispatch(pstart, counts, slot, hn, n_slots, td, bm):
    t, d = hn.shape
    return pl.pallas_call(
        _dispatch_kernel,
        grid_spec=pltpu.PrefetchScalarGridSpec(
            num_scalar_prefetch=2, grid=(t // td,),
            in_specs=[pl.BlockSpec((td * SLOT_STRIDE,), lambda i, ps, cn: (i,), memory_space=pltpu.SMEM),
                      pl.BlockSpec((td // ROW_GROUP, ROW_GROUP, d), lambda i, ps, cn: (i, 0, 0))],
            out_specs=pl.BlockSpec(memory_space=pl.ANY),
            scratch_shapes=[pltpu.VMEM((bm, d), hn.dtype), pltpu.SemaphoreType.DMA(())]),
        out_shape=jax.ShapeDtypeStruct((n_slots, d), hn.dtype),
        compiler_params=_cparams("arbitrary"),
        name="dispatch",
    )(pstart, counts, slot, hn.reshape(t // ROW_GROUP, ROW_GROUP, d))


EXPERT_SUB = 256


def _expert_kernel(bexp_ref, bsrc_ref, nused_ref, xs_ref, wg_ref, wu_ref, wd_ref, y_ref):
    del bexp_ref, bsrc_ref
    i = pl.program_id(0)

    @pl.when(i < nused_ref[0])
    def _():
        wg = wg_ref[0].astype(BF16)
        wu = wu_ref[0].astype(BF16)
        wd = wd_ref[0].astype(BF16)
        for r0 in range(0, xs_ref.shape[0], EXPERT_SUB):
            rows = slice(r0, r0 + EXPERT_SUB)
            xb = jnp.concatenate(_unpack_bf16_pairs(xs_ref[rows, :]), axis=1).astype(BF16)
            g = jnp.dot(xb, wg, preferred_element_type=F32)
            u = jnp.dot(xb, wu, preferred_element_type=F32)
            act = (_silu(g) * u).astype(BF16)
            y_ref[rows, :] = _pack_bf16_pairs(jnp.dot(act, wd, preferred_element_type=F32))

    @pl.when(i >= nused_ref[0])
    def _():
        y_ref[...] = jnp.zeros(y_ref.shape, y_ref.dtype)


def _experts(block_exp, block_src, n_used, xs, w_gate, w_up, w_down, bm):
    n_slots, d = xs.shape
    w_map = lambda i, be, bs, nu: (be[i], 0, 0)
    return pl.pallas_call(
        _expert_kernel,
        grid_spec=pltpu.PrefetchScalarGridSpec(
            num_scalar_prefetch=3, grid=(n_slots // bm,),
            in_specs=[pl.BlockSpec((bm, d), lambda i, be, bs, nu: (bs[i], 0)),
                      pl.BlockSpec((1,) + w_gate.shape[1:], w_map),
                      pl.BlockSpec((1,) + w_up.shape[1:], w_map),
                      pl.BlockSpec((1,) + w_down.shape[1:], w_map)],
            out_specs=pl.BlockSpec((bm, d), lambda i, be, bs, nu: (i, 0))),
        out_shape=jax.ShapeDtypeStruct((n_slots, d), xs.dtype),
        compiler_params=_cparams("arbitrary"),
        name="experts",
    )(block_exp, block_src, n_used, xs, w_gate, w_up, w_down)


def _combine_kernel(slot_ref, slot_next_ref, wt_ref, base_ref, gt2_ref, gfin_ref, y_hbm, o_ref, buf, sem):
    tg = base_ref.shape[0]
    i = pl.program_id(0)
    cur = i % 2

    def gather(table, b):
        _row_copies(tg, lambda g, s, j, k: pltpu.make_async_copy(
            y_hbm.at[pl.ds(table[k], 1)], buf.at[b, j, g, pl.ds(s, 1)], sem.at[b]))

    @pl.when(i == 0)
    def _():
        gather(slot_ref, 0)

    @pl.when(i + 1 < pl.num_programs(0))
    def _():
        gather(slot_next_ref, 1 - cur)

    pltpu.make_async_copy(buf.at[cur], buf.at[cur], sem.at[cur]).wait()
    wt = wt_ref[...]
    lo = hi = None
    for j in range(TOP_K):
        ylo, yhi = _unpack_bf16_pairs(buf[cur, j].reshape(tg, buf.shape[-1]))
        w = wt[:, j:j + 1]
        lo = w * ylo if lo is None else lo + w * ylo
        hi = w * yhi if hi is None else hi + w * yhi
    xf = base_ref[...] + gt2_ref[0] * jnp.concatenate([lo, hi], axis=1)
    o_ref[...] = xf * lax.rsqrt(jnp.mean(xf * xf, axis=-1, keepdims=True) + EPS) * gfin_ref[...]


def _combine(slot, wt, base, gt2, g_final, y, seq, tg):
    t, d = base.shape
    tiles = seq // tg
    return pl.pallas_call(
        _combine_kernel,
        grid=(t // tg,),
        in_specs=[pl.BlockSpec((tg * SLOT_STRIDE,), lambda i: (i,), memory_space=pltpu.SMEM),
                  pl.BlockSpec((tg * SLOT_STRIDE,), lambda i: (jnp.minimum(i + 1, t // tg - 1),),
                               memory_space=pltpu.SMEM),
                  pl.BlockSpec((tg, LANES), lambda i: (i, 0)),
                  pl.BlockSpec((tg, d), lambda i: (i, 0)),
                  pl.BlockSpec((1, 1, d), lambda i: (i // tiles, 0, 0)),
                  pl.BlockSpec((1, d), lambda i: (0, 0)),
                  pl.BlockSpec(memory_space=pl.ANY)],
        out_specs=pl.BlockSpec((tg, d), lambda i: (i, 0)),
        out_shape=jax.ShapeDtypeStruct((t, d), F32),
        scratch_shapes=[pltpu.VMEM((2, TOP_K, tg // ROW_GROUP, ROW_GROUP, y.shape[1]), y.dtype),
                        pltpu.SemaphoreType.DMA((2,))],
        compiler_params=_cparams("arbitrary"),
        name="combine",
    )(slot, slot, wt, base, gt2, g_final.reshape(1, d), y)


EXPERT_BLOCK = 512
ROW_TILE = 512
ROUTER_TILE = 512
GATHER_TILE = 1024
SLOT_TILE = 4096


def _moe_plan(counts, n_tokens, bm):
    padded = (counts + bm - 1) // bm * bm
    pad_end = jnp.cumsum(padded)
    pad_start = (pad_end - padded).astype(jnp.int32)
    n_blocks = -(-(n_tokens * TOP_K) // bm) + N_EXPERTS + 1
    n_used = (pad_end[-1] // bm).astype(jnp.int32)
    block_src = jnp.minimum(jnp.arange(n_blocks, dtype=jnp.int32), jnp.maximum(n_used - 1, 0))
    block_exp = jnp.sum((pad_end[None, :] <= (block_src * bm)[:, None]).astype(jnp.int32), axis=1)
    block_exp = jnp.minimum(block_exp, N_EXPERTS - 1)
    return pad_start, block_exp, block_src, n_used.reshape(1), n_blocks * bm


def kernel(x, c, ctx, c_ctx, w_mod, b_mod, g_mix, g_ffn, w_in, w_out, dn_conv_w, dn_a_log, dn_dt_bias,
           dn_norm_g, cf_dw_w, cf_dw_b, cf_ln_g, cf_ln_b, router_w, router_bias, exp_w_gate, exp_w_up,
           exp_w_down, sh_w_gate, sh_w_up, sh_w_down, g_final):
    assert w_mod.shape[0] == 1, "single-layer block: the context stream is never re-read"
    bsz, seq, d = x.shape
    t = bsz * seq
    x2 = x.reshape(t, d)
    st = _mixer_front(x, c, ctx, c_ctx, w_mod[0], b_mod[0], g_mix[0], w_in[0], dn_conv_w[0], dn_a_log[0],
                      dn_dt_bias[0])
    _, _, gt1, sh2, sc2, gt2 = st['lat']
    cfo = _cf_conv(st['y'], cf_dw_w[0], cf_dw_b[0], cf_ln_g[0], cf_ln_b[0], bsz)
    wsgu = jnp.concatenate([sh_w_gate[0], sh_w_up[0]], axis=1).astype(BF16)
    base, hn, lg_t = _mixer_out(st['o_f'], st['o_b'], st['sz'], cfo, x2, gt1, sh2, sc2, gt2,
                                dn_norm_g[0].reshape(1, DN_D), w_out[0].astype(BF16), g_ffn[0].reshape(1, d),
                                router_w[0].T, wsgu, sh_w_down[0].astype(BF16), seq, min(seq, ROW_TILE))
    e_idx, rank, wt, cnt = _router(lg_t, router_bias[0], min(t, ROUTER_TILE))
    counts = cnt[:, 0].astype(jnp.int32)
    pstart, block_exp, block_src, n_used, n_slots = _moe_plan(counts, t, EXPERT_BLOCK)
    slot = _slot_index(pstart, e_idx, rank, min(t, SLOT_TILE))
    xs = _dispatch(pstart, counts, slot, hn, n_slots, min(t, GATHER_TILE), EXPERT_BLOCK)
    y = _experts(block_exp, block_src, n_used, xs, exp_w_gate[0], exp_w_up[0], exp_w_down[0], EXPERT_BLOCK)
    out = _combine(slot, wt, base, gt2, g_final, y, seq, min(seq, GATHER_TILE))
    return out.reshape(bsz, seq, d)
```

```python
import functools

import jax
import jax.numpy as jnp
from jax import lax
from jax.experimental import pallas as pl
from jax.experimental.pallas import tpu as pltpu

F32 = jnp.float32
U32 = jnp.uint32
BF16 = jnp.bfloat16
HIGHEST = lax.Precision.HIGHEST

EPS = 1e-6
LANES = 128
GRID_W = 64
DN_HEADS = 4
DN_D = 128
DN_QK = DN_HEADS * DN_D
DN_WIDTH = DN_HEADS * DN_D
DN_CONV_DIM = 2 * DN_QK + DN_WIDTH
SHORT_CONV = 7
CHUNK = 64
CF_CH = 512
CF_K = 31
N_EXPERTS = 64
TOP_K = 6
N_GROUPS = 8
GROUP_SIZE = N_EXPERTS // N_GROUPS
TOPK_GROUPS = 4
ROUTED_SCALE = 2.5
HALO = 8
VMEM_LIMIT = 56 * 1024 * 1024


def _cparams(*sem):
    return pltpu.CompilerParams(dimension_semantics=sem, vmem_limit_bytes=VMEM_LIMIT)


def _silu(v):
    return v * jax.nn.sigmoid(v)


def _bdot(a, b):
    return jnp.dot(a.astype(BF16), b.astype(BF16), preferred_element_type=F32)


def _pack_bf16_pairs(v):
    n = v.shape[1] // 2
    lo = lax.bitcast_convert_type(v[:, :n].astype(BF16).astype(F32), U32)
    hi = lax.bitcast_convert_type(v[:, n:].astype(BF16).astype(F32), U32)
    return (lo >> 16) | (hi & jnp.uint32(0xFFFF0000))


def _unpack_bf16_pairs(w):
    lo = lax.bitcast_convert_type(w << 16, F32)
    hi = lax.bitcast_convert_type(w & jnp.uint32(0xFFFF0000), F32)
    return lo, hi


def _mod_kernel(c_ref, w_ref, b_ref, o_ref):
    o_ref[...] = jnp.dot(_silu(c_ref[...]), w_ref[...], preferred_element_type=F32,
                         precision=HIGHEST) + b_ref[...]


def _modulation(cc, w_mod, b_mod):
    rows, d = cc.shape
    n = w_mod.shape[1]
    tn = 1024
    return pl.pallas_call(
        _mod_kernel,
        grid=(n // tn,),
        in_specs=[pl.BlockSpec((rows, d), lambda j: (0, 0)),
                  pl.BlockSpec((d, tn), lambda j: (0, j)),
                  pl.BlockSpec((1, tn), lambda j: (0, j))],
        out_specs=pl.BlockSpec((rows, tn), lambda j: (0, j)),
        out_shape=jax.ShapeDtypeStruct((rows, n), F32),
        compiler_params=_cparams("parallel"),
        name="modulation",
    )(cc, w_mod, b_mod.reshape(1, n))


def _chunk_cumsum(g, reverse):
    n = g.shape[0]
    pos = lax.broadcasted_iota(jnp.int32, g.shape, 0) % CHUNK
    s = 1
    while s < CHUNK:
        if reverse:
            shifted = pltpu.roll(g, n - s, 0)
            ok = pos < CHUNK - s
        else:
            shifted = pltpu.roll(g, s, 0)
            ok = pos >= s
        g = g + jnp.where(ok, shifted, 0.0)
        s *= 2
    return g


def _inproj_kernel(latent, x_ref, sh_ref, sc_ref, g_ref, wqkv_ref, wba_ref, alog_ref, dtb_ref, *rest):
    if latent:
        wz_ref, wcf_ref, qkv_o, bgc_o, bgct_o, sz_o, y_o = rest
    else:
        qkv_o, bgc_o, bgct_o = rest
    x = x_ref[...]
    xn = x * lax.rsqrt(jnp.mean(x * x, axis=-1, keepdims=True) + EPS) * g_ref[...]
    hb = (xn * (1.0 + sc_ref[0]) + sh_ref[0]).astype(BF16)
    qkv_o[...] = jnp.dot(hb, wqkv_ref[...], preferred_element_type=F32)

    ba = jnp.dot(hb, wba_ref[...], preferred_element_type=F32)
    col = lax.broadcasted_iota(jnp.int32, ba.shape, 1)
    is_beta = (col % 8) < DN_HEADS
    g = -jnp.exp(alog_ref[...]) * jax.nn.softplus(ba + dtb_ref[...])
    g = jnp.where(is_beta, 0.0, g)
    gc = jnp.where(col < 8, _chunk_cumsum(g, False), _chunk_cumsum(g, True))
    bgc = jnp.where(is_beta, jax.nn.sigmoid(ba), gc)
    bgc_o[...] = bgc
    for c in range(bgc.shape[0] // CHUNK):
        bgct_o[c] = jnp.transpose(bgc[c * CHUNK:(c + 1) * CHUNK, :])[:16, :]

    if latent:
        sz_o[...] = _silu(jnp.dot(hb, wz_ref[...], preferred_element_type=F32))
        cf = jnp.dot(hb, wcf_ref[...], preferred_element_type=F32)
        y_o[...] = cf[:, :CF_CH] * jax.nn.sigmoid(cf[:, CF_CH:])


def _in_projection(x2, sh, sc, g_mix, wqkv, wba, alog_row, dtb_row, wz, wcf, rows_per_mod, tm):
    t, d = x2.shape
    latent = wz is not None
    tiles_per_mod = rows_per_mod // tm
    const = lambda i: (0, 0)
    mod_map = lambda i: (i // tiles_per_mod, 0, 0)
    row_map = lambda i: (i, 0)
    in_specs = [pl.BlockSpec((tm, d), row_map),
                pl.BlockSpec((1, 1, d), mod_map), pl.BlockSpec((1, 1, d), mod_map),
                pl.BlockSpec((1, d), const),
                pl.BlockSpec(wqkv.shape, const), pl.BlockSpec(wba.shape, const),
                pl.BlockSpec((1, LANES), const), pl.BlockSpec((1, LANES), const)]
    args = [x2, sh, sc, g_mix, wqkv, wba, alog_row, dtb_row]
    out_specs = [pl.BlockSpec((tm, DN_CONV_DIM), row_map), pl.BlockSpec((tm, LANES), row_map),
                 pl.BlockSpec((tm // CHUNK, 16, CHUNK), lambda i: (i, 0, 0))]
    out_shape = [jax.ShapeDtypeStruct((t, DN_CONV_DIM), F32), jax.ShapeDtypeStruct((t, LANES), F32),
                 jax.ShapeDtypeStruct((t // CHUNK, 16, CHUNK), F32)]
    if latent:
        in_specs += [pl.BlockSpec(wz.shape, const), pl.BlockSpec(wcf.shape, const)]
        args += [wz, wcf]
        out_specs += [pl.BlockSpec((tm, DN_WIDTH), row_map), pl.BlockSpec((tm, CF_CH), row_map)]
        out_shape += [jax.ShapeDtypeStruct((t, DN_WIDTH), F32), jax.ShapeDtypeStruct((t, CF_CH), F32)]
    return pl.pallas_call(
        functools.partial(_inproj_kernel, latent),
        grid=(t // tm,),
        in_specs=in_specs, out_specs=out_specs, out_shape=out_shape,
        compiler_params=_cparams("parallel"),
        name="in_projection_latent" if latent else "in_projection_context",
    )(*args)


CONV_ROWS = 64


def _shortconv_kernel(tiles_per_seq, prev_ref, cur_ref, next_ref, w_ref, o_ref, ext_ref):
    i = pl.program_id(0)
    tm = cur_ref.shape[0]
    pos = i % tiles_per_seq
    ext_ref[0:HALO, :] = jnp.where(pos == 0, 0.0, prev_ref[...])
    ext_ref[HALO:HALO + tm, :] = cur_ref[...]
    ext_ref[HALO + tm:, :] = jnp.where(pos == tiles_per_seq - 1, 0.0, next_ref[...])
    reach = SHORT_CONV // 2

    for r0 in range(0, tm, CONV_ROWS):
        for cb in range(DN_CONV_DIM // LANES):
            cols = slice(cb * LANES, (cb + 1) * LANES)
            acc = jnp.zeros((CONV_ROWS, LANES), F32)
            for k in range(SHORT_CONV):
                s = r0 + HALO - reach + k
                acc = acc + w_ref[k:k + 1, cols] * ext_ref[s:s + CONV_ROWS, cols]
            a = _silu(acc)
            if cb < 2 * DN_HEADS:
                a = a * lax.rsqrt(jnp.sum(a * a, axis=-1, keepdims=True) + EPS)
            if cb < DN_HEADS:
                a = a * (DN_D ** -0.5)
            o_ref[r0:r0 + CONV_ROWS, cols] = a


def _short_conv(qkv, conv_w, seq_len, tm):
    t, c = qkv.shape
    tiles_per_seq = seq_len // tm
    hb = tm // HALO
    n_halo_blocks = t // HALO
    return pl.pallas_call(
        functools.partial(_shortconv_kernel, tiles_per_seq),
        grid=(t // tm,),
        in_specs=[pl.BlockSpec((HALO, c), lambda i: (jnp.maximum(i * hb - 1, 0), 0)),
                  pl.BlockSpec((tm, c), lambda i: (i, 0)),
                  pl.BlockSpec((HALO, c), lambda i: (jnp.minimum((i + 1) * hb, n_halo_blocks - 1), 0)),
                  pl.BlockSpec(conv_w.shape, lambda i: (0, 0))],
        out_specs=pl.BlockSpec((tm, c), lambda i: (i, 0)),
        out_shape=jax.ShapeDtypeStruct((t, c), F32),
        scratch_shapes=[pltpu.VMEM((tm + 2 * HALO, c), F32)],
        compiler_params=_cparams("parallel"),
        name="short_conv",
    )(qkv, qkv, qkv, conv_w)


DELTA_STEPS = 2
INV_BLOCK = 16


def _delta_pre(refs, ci_f, ci_b, masks, blocks, eye):
    qkv_ref, bgc_ref, bgct_ref = refs
    chains = range(2 * DN_HEADS * len(ci_f))
    q, k, v, beta, gcol, decay, eg, gl = [], [], [], [], [], [], [], []
    for d, ci in [(d, c[step]) for step in range(len(ci_f)) for d, c in enumerate((ci_f, ci_b))]:
        rows = pl.ds(pl.multiple_of(ci * CHUNK, CHUNK), CHUNK)
        bg = bgc_ref[rows, :]
        bgt = bgct_ref[ci]
        for h in range(DN_HEADS):
            q.append(qkv_ref[rows, h * DN_D:(h + 1) * DN_D])
            k.append(qkv_ref[rows, DN_QK + h * DN_D:DN_QK + (h + 1) * DN_D])
            v.append(qkv_ref[rows, 2 * DN_QK + h * DN_D:2 * DN_QK + (h + 1) * DN_D])
            jb = d * 8 + h
            jg = d * 8 + DN_HEADS + h
            beta.append(bg[:, jb:jb + 1])
            gc = bg[:, jg:jg + 1]
            gcol.append(gc)
            decay.append(jnp.exp(jnp.where(masks[d][0], gc - bgt[jg:jg + 1, :], -jnp.inf)))
            eg.append(jnp.exp(gc))
            gl.append(gc[CHUNK - 1:CHUNK] if d == 0 else gc[0:1])

    contract_last = (((1,), (1,)), ((), ()))
    gram = [lax.dot_general(jnp.concatenate([k[n], q[n]], axis=0).astype(BF16), k[n].astype(BF16),
                            contract_last, preferred_element_type=F32) for n in chains]
    lmat = [jnp.where(masks[n // DN_HEADS % 2][1], gram[n][:CHUNK] * decay[n], 0.0) * beta[n] for n in chains]
    same_small, same_half = blocks
    dmat = [jnp.where(same_small, lmat[n], 0.0) for n in chains]
    tinv = [eye - dmat[n] for n in chains]
    p = [_bdot(dmat[n], dmat[n]) for n in chains]
    n_sq = 2
    while n_sq < INV_BLOCK // 2:
        x = [_bdot(jnp.concatenate([tinv[n], p[n]], axis=0), p[n]) for n in chains]
        tinv = [tinv[n] + x[n][:CHUNK] for n in chains]
        p = [x[n][CHUNK:] for n in chains]
        n_sq *= 2
    tinv = [tinv[n] + _bdot(tinv[n], p[n]) for n in chains]
    for off_mask in (jnp.logical_and(same_half, jnp.logical_not(same_small)), jnp.logical_not(same_half)):
        x = [_bdot(jnp.where(off_mask, lmat[n], 0.0), tinv[n]) for n in chains]
        tinv = [tinv[n] - _bdot(tinv[n], x[n]) for n in chains]
    uw = [_bdot(tinv[n], jnp.concatenate([v[n] * beta[n], k[n] * (beta[n] * eg[n])], axis=1)) for n in chains]

    return [(uw[n][:, :DN_D], uw[n][:, DN_D:], q[n] * eg[n], k[n] * jnp.exp(gl[n] - gcol[n]),
             gram[n][CHUNK:] * decay[n], jnp.exp(gl[n])) for n in chains]


def _delta_rec(pre, s_ref, want_out):
    chains = range(len(pre))
    contract_first = (((0,), (0,)), ((), ()))
    s = [s_ref[n] for n in chains]
    if want_out:
        wq = [_bdot(jnp.concatenate([pre[n][1], pre[n][2]], axis=0), s[n]) for n in chains]
        ws = [wq[n][:CHUNK] for n in chains]
    else:
        ws = [_bdot(pre[n][1], s[n]) for n in chains]
    vb = [(pre[n][0] - ws[n]).astype(BF16) for n in chains]
    for n in chains:
        s_ref[n] = s[n] * pre[n][5] + lax.dot_general(pre[n][3].astype(BF16), vb[n], contract_first,
                                                      preferred_element_type=F32)
    if not want_out:
        return None
    return [wq[n][CHUNK:] + jnp.dot(pre[n][4].astype(BF16), vb[n], preferred_element_type=F32)
            for n in chains]


def _delta_kernel(qc_ref, bc_ref, btc_ref, qx_ref, bx_ref, btx_ref, of_ref, ob_ref, s_ref, p_ref):
    s_ref[...] = jnp.zeros(s_ref.shape, F32)
    row = lax.broadcasted_iota(jnp.int32, (CHUNK, CHUNK), 0)
    col = lax.broadcasted_iota(jnp.int32, (CHUNK, CHUNK), 1)
    masks = ((row >= col, row > col), (row <= col, row < col))
    eye = (row == col).astype(F32)
    blocks = (row // INV_BLOCK == col // INV_BLOCK, row // (CHUNK // 2) == col // (CHUNK // 2))
    per_step = 2 * DN_HEADS
    n_chains = DELTA_STEPS * per_step

    def stage(pre):
        for n in range(n_chains):
            u, w, qe, kd, amat, egl = pre[n]
            for j, val in enumerate((u, w, qe, kd)):
                p_ref[n, j] = val
            p_ref[n, 4] = jnp.concatenate([amat, jnp.broadcast_to(egl, (CHUNK, DN_D - CHUNK))], axis=1)

    def staged():
        return [(p_ref[n, 0], p_ref[n, 1], p_ref[n, 2], p_ref[n, 3], p_ref[n, 4][:, :CHUNK],
                 p_ref[n, 4][0:1, CHUNK:CHUNK + 1]) for n in range(n_chains)]

    def scan(refs, want_out):
        n_chunks = refs[0].shape[0] // CHUNK
        assert n_chunks % DELTA_STEPS == 0

        def pre(first):
            fwd = [first + s for s in range(DELTA_STEPS)]
            return _delta_pre(refs, fwd, [n_chunks - 1 - c for c in fwd], masks, blocks, eye)

        stage(pre(0))

        def update(i, cur):
            for s in range(DELTA_STEPS):
                o = _delta_rec(cur[s * per_step:(s + 1) * per_step], s_ref, want_out)
                if want_out:
                    rf = pl.ds(pl.multiple_of((i + s) * CHUNK, CHUNK), CHUNK)
                    rb = pl.ds(pl.multiple_of((n_chunks - 1 - i - s) * CHUNK, CHUNK), CHUNK)
                    for h in range(DN_HEADS):
                        cols = slice(h * DN_D, (h + 1) * DN_D)
                        of_ref[rf, cols] = o[h]
                        ob_ref[rb, cols] = o[DN_HEADS + h]

        def body(it, carry):
            i = it * DELTA_STEPS
            cur = staged()
            new = pre(i + DELTA_STEPS)
            update(i, cur)
            stage(new)
            return carry

        n_iter = n_chunks // DELTA_STEPS
        lax.fori_loop(0, n_iter - 1, body, 0)
        update((n_iter - 1) * DELTA_STEPS, staged())

    scan((qc_ref, bc_ref, btc_ref), False)
    scan((qx_ref, bx_ref, btx_ref), True)


def _delta_scan(qkv_c, bgc_c, bgct_c, qkv_x, bgc_x, bgct_x, bsz):
    lc = qkv_c.shape[0] // bsz
    lx = qkv_x.shape[0] // bsz
    row_map = lambda b: (b, 0)
    out = jax.ShapeDtypeStruct((bsz * lx, DN_WIDTH), F32)
    return pl.pallas_call(
        _delta_kernel,
        grid=(bsz,),
        in_specs=[pl.BlockSpec((lc, DN_CONV_DIM), row_map), pl.BlockSpec((lc, LANES), row_map),
                  pl.BlockSpec((lc // CHUNK, 16, CHUNK), lambda b: (b, 0, 0)),
                  pl.BlockSpec((lx, DN_CONV_DIM), row_map), pl.BlockSpec((lx, LANES), row_map),
                  pl.BlockSpec((lx // CHUNK, 16, CHUNK), lambda b: (b, 0, 0))],
        out_specs=[pl.BlockSpec((lx, DN_WIDTH), row_map), pl.BlockSpec((lx, DN_WIDTH), row_map)],
        out_shape=[out, out],
        scratch_shapes=[pltpu.VMEM((2 * DN_HEADS, DN_D, DN_D), F32),
                        pltpu.VMEM((DELTA_STEPS * 2 * DN_HEADS, 5, CHUNK, DN_D), F32)],
        compiler_params=_cparams("parallel"),
        name="delta_scan",
    )(qkv_c, bgc_c, bgct_c, qkv_x, bgc_x, bgct_x)


def _pad_rows(a, rows):
    return jnp.pad(a, ((0, rows - a.shape[0]), (0, 0)))


def _mixer_front(x, c, ctx, c_ctx, w_mod, b_mod, g_mix, w_in, conv_w, a_log, dt_bias):
    bsz, seq, d = x.shape
    lc = ctx.shape[1]
    off_z = DN_CONV_DIM
    off_ba = off_z + DN_WIDTH
    off_cf = off_ba + 4 * DN_HEADS

    cc = _pad_rows(jnp.concatenate([c, c_ctx[None, :]], axis=0), -(-(bsz + 1) // 8) * 8)
    mod = _modulation(cc, w_mod, b_mod)
    mods = [mod[:, j * d:(j + 1) * d] for j in range(6)]
    lat = [m[:bsz].reshape(bsz, 1, d) for m in mods]
    con = [m[bsz:bsz + 1].reshape(1, 1, d) for m in mods]

    wqkv = w_in[:, :off_z].astype(BF16)
    wz = w_in[:, off_z:off_ba].astype(BF16)
    wba = jnp.pad(w_in[:, off_ba:off_cf], ((0, 0), (0, LANES - 4 * DN_HEADS))).astype(BF16)
    wcf = w_in[:, off_cf:].astype(BF16)
    zeros4 = jnp.zeros((2, DN_HEADS), F32)
    alog_row = jnp.pad(jnp.concatenate([zeros4, a_log], axis=1).reshape(1, -1), ((0, 0), (0, LANES - 16)))
    dtb_row = jnp.pad(jnp.concatenate([zeros4, dt_bias], axis=1).reshape(1, -1), ((0, 0), (0, LANES - 16)))
    g_row = g_mix.reshape(1, d)

    qkv_c, bgc_c, bgct_c = _in_projection(ctx.reshape(bsz * lc, d), con[0], con[1], g_row, wqkv, wba,
                                          alog_row, dtb_row, None, None, bsz * lc, min(lc, 512))
    qkv_x, bgc_x, bgct_x, sz, y = _in_projection(x.reshape(bsz * seq, d), lat[0], lat[1], g_row, wqkv, wba,
                                                 alog_row, dtb_row, wz, wcf, seq, min(seq, 512))
    qkvn_c = _short_conv(qkv_c, conv_w, lc, min(lc, 512))
    qkvn_x = _short_conv(qkv_x, conv_w, seq, min(seq, 512))
    o_f, o_b = _delta_scan(qkvn_c, bgc_c, bgct_c, qkvn_x, bgc_x, bgct_x, bsz)
    return dict(mod=mod, lat=lat, qkv_x=qkv_x, qkv_c=qkv_c, sz=sz, y=y, qkvn_x=qkvn_x, bgc_x=bgc_x,
                bgct_x=bgct_x, o_f=o_f, o_b=o_b)


CF_ROWS = 32


def _cfconv_kernel(y_ref, w_ref, b_ref, lng_ref, lnb_ref, o_ref, pad_ref):
    n = y_ref.shape[0]
    reach = (CF_K // 2) * GRID_W
    pad_ref[0:reach, :] = jnp.zeros((reach, CF_CH), F32)
    pad_ref[reach:reach + n, :] = y_ref[...]
    pad_ref[reach + n:, :] = jnp.zeros((reach, CF_CH), F32)

    def body(r, carry):
        r0 = pl.multiple_of(r * CF_ROWS, CF_ROWS)
        acc = jnp.zeros((CF_ROWS, CF_CH), F32)
        for k in range(CF_K):
            acc = acc + w_ref[k:k + 1, :] * pad_ref[pl.ds(r0 + k * GRID_W, CF_ROWS), :]
        acc = acc + b_ref[...]
        mu = jnp.mean(acc, axis=-1, keepdims=True)
        xc = acc - mu
        var = jnp.mean(xc * xc, axis=-1, keepdims=True)
        o_ref[pl.ds(r0, CF_ROWS), :] = _silu(xc * lax.rsqrt(var + EPS) * lng_ref[...] + lnb_ref[...])
        return carry

    lax.fori_loop(0, n // CF_ROWS, body, 0, unroll=8)


def _cf_conv(y, dw_w, dw_b, ln_g, ln_b, bsz):
    t, ch = y.shape
    n = t // bsz
    reach = (CF_K // 2) * GRID_W
    const = lambda b: (0, 0)
    return pl.pallas_call(
        _cfconv_kernel,
        grid=(bsz,),
        in_specs=[pl.BlockSpec((n, ch), lambda b: (b, 0)), pl.BlockSpec(dw_w.shape, const),
                  pl.BlockSpec((1, ch), const), pl.BlockSpec((1, ch), const), pl.BlockSpec((1, ch), const)],
        out_specs=pl.BlockSpec((n, ch), lambda b: (b, 0)),
        out_shape=jax.ShapeDtypeStruct((t, ch), F32),
        scratch_shapes=[pltpu.VMEM((n + 2 * reach, ch), F32)],
        compiler_params=_cparams("parallel"),
        name="conformer_conv",
    )(y, dw_w, dw_b.reshape(1, ch), ln_g.reshape(1, ch), ln_b.reshape(1, ch))


def _mixout_kernel(of_ref, ob_ref, sz_ref, cf_ref, x_ref, gt1_ref, sh2_ref, sc2_ref, gt2_ref, ng_ref, wo_ref,
                   gffn_ref, rwt_ref, wsgu_ref, wsd_ref, base_o, hn_o, lg_o):
    contract_last = (((1,), (1,)), ((), ()))
    rw = rwt_ref[...]
    rw_hi = rw.astype(BF16)
    rw_lo = (rw - rw_hi.astype(F32)).astype(BF16)
    rw_both = jnp.concatenate([rw_hi, rw_lo], axis=0)
    tm = x_ref.shape[0]
    for r0 in range(0, tm, tm // 2):
        rows = slice(r0, r0 + tm // 2)
        o = of_ref[rows, :] + ob_ref[rows, :]
        parts = []
        for h in range(DN_HEADS):
            oh = o[:, h * DN_D:(h + 1) * DN_D]
            parts.append(oh * lax.rsqrt(jnp.mean(oh * oh, axis=-1, keepdims=True) + EPS) * ng_ref[...])
        dn = jnp.concatenate(parts, axis=1) * sz_ref[rows, :]
        heads = jnp.concatenate([dn, cf_ref[rows, :]], axis=1).astype(BF16)
        x1 = x_ref[rows, :] + gt1_ref[0] * jnp.dot(heads, wo_ref[...], preferred_element_type=F32)
        hn = (x1 * lax.rsqrt(jnp.mean(x1 * x1, axis=-1, keepdims=True) + EPS) * gffn_ref[...]
              * (1.0 + sc2_ref[0]) + sh2_ref[0])
        hn_o[rows, :] = _pack_bf16_pairs(hn)
        hn_hi = hn.astype(BF16)
        hn_lo = (hn - hn_hi.astype(F32)).astype(BF16)
        both = lax.dot_general(rw_both, hn_hi, contract_last, preferred_element_type=F32)
        lg_o[:, rows] = (both[:N_EXPERTS] + both[N_EXPERTS:]
                         + lax.dot_general(rw_hi, hn_lo, contract_last, preferred_element_type=F32))
        gu = jnp.dot(hn_hi, wsgu_ref[...], preferred_element_type=F32)
        ds = gu.shape[1] // 2
        act = (_silu(gu[:, :ds]) * gu[:, ds:]).astype(BF16)
        base_o[rows, :] = x1 + gt2_ref[0] * jnp.dot(act, wsd_ref[...], preferred_element_type=F32)


def _mixer_out(o_f, o_b, sz, cfo, x2, gt1, sh2, sc2, gt2, norm_g, w_out, g_ffn, rwt, wsgu, wsd, seq, tm):
    t, d = x2.shape
    tiles = seq // tm
    const = lambda i: (0, 0)
    row_map = lambda i: (i, 0)
    mod_map = lambda i: (i // tiles, 0, 0)
    half = pl.BlockSpec((tm, DN_WIDTH), row_map)
    mod_spec = pl.BlockSpec((1, 1, d), mod_map)
    return pl.pallas_call(
        _mixout_kernel,
        grid=(t // tm,),
        in_specs=[half, half, half, half, pl.BlockSpec((tm, d), row_map),
                  mod_spec, mod_spec, mod_spec, mod_spec,
                  pl.BlockSpec((1, DN_D), const), pl.BlockSpec(w_out.shape, const), pl.BlockSpec((1, d), const),
                  pl.BlockSpec(rwt.shape, const), pl.BlockSpec(wsgu.shape, const), pl.BlockSpec(wsd.shape, const)],
        out_specs=[pl.BlockSpec((tm, d), row_map), pl.BlockSpec((tm, d // 2), row_map),
                   pl.BlockSpec((N_EXPERTS, tm), lambda i: (0, i))],
        out_shape=[jax.ShapeDtypeStruct((t, d), F32), jax.ShapeDtypeStruct((t, d // 2), U32),
                   jax.ShapeDtypeStruct((N_EXPERTS, t), F32)],
        compiler_params=_cparams("parallel"),
        name="mixer_out",
    )(o_f, o_b, sz, cfo, x2, gt1, sh2, sc2, gt2, norm_g, w_out, g_ffn, rwt, wsgu, wsd)


def _first_argmax(vals, idx, sentinel):
    m = jnp.max(vals, axis=0, keepdims=True)
    return m, jnp.min(jnp.where(vals == m, idx, sentinel), axis=0, keepdims=True)


def _router_kernel(lg_ref, bias_ref, e_o, r_o, wt_o, cnt_o, carry_ref):
    @pl.when(pl.program_id(0) == 0)
    def _():
        carry_ref[...] = jnp.zeros(carry_ref.shape, F32)

    tt = lg_ref.shape[1]
    neg = -jnp.inf
    scores = jax.nn.sigmoid(lg_ref[...])
    sel = scores + bias_ref[...]
    sub = lax.broadcasted_iota(jnp.int32, (GROUP_SIZE, tt), 0)

    rows = []
    for g in range(N_GROUPS):
        sg = sel[g * GROUP_SIZE:(g + 1) * GROUP_SIZE]
        m1, first = _first_argmax(sg, sub, GROUP_SIZE)
        m2 = jnp.max(jnp.where(sub == first, neg, sg), axis=0, keepdims=True)
        rows.append(m1 + m2)
    cur = jnp.concatenate(rows, axis=0)
    gidx = lax.broadcasted_iota(jnp.int32, (N_GROUPS, tt), 0)
    keep = gidx < 0
    for _ in range(TOPK_GROUPS):
        _, a = _first_argmax(cur, gidx, N_GROUPS)
        pick = gidx == a
        keep = jnp.logical_or(keep, pick)
        cur = jnp.where(pick, neg, cur)
    keep_e = jnp.concatenate([jnp.broadcast_to(keep[g:g + 1], (GROUP_SIZE, tt)) for g in range(N_GROUPS)],
                             axis=0)
    masked = jnp.where(keep_e, sel, neg)

    eidx = lax.broadcasted_iota(jnp.int32, (N_EXPERTS, tt), 0)
    e_rows, w_rows, picks = [], [], []
    for _ in range(TOP_K):
        _, a = _first_argmax(masked, eidx, N_EXPERTS)
        pick = eidx == a
        e_rows.append(a)
        w_rows.append(jnp.sum(jnp.where(pick, scores, 0.0), axis=0, keepdims=True))
        picks.append(pick)
        masked = jnp.where(pick, neg, masked)
    onehot = sum(p.astype(F32) for p in picks)
    scale = ROUTED_SCALE / sum(w_rows)

    tri = (lax.broadcasted_iota(jnp.int32, (tt, tt), 0) < lax.broadcasted_iota(jnp.int32, (tt, tt), 1))
    cum = jnp.dot(onehot.astype(BF16), tri.astype(BF16), preferred_element_type=F32) + carry_ref[:, 0:1]
    r_rows = [jnp.sum(jnp.where(p, cum, 0.0), axis=0, keepdims=True).astype(jnp.int32) for p in picks]
    carry_ref[...] = carry_ref[...] + jnp.sum(onehot, axis=1, keepdims=True)
    cnt_o[...] = carry_ref[...]

    fill = 8 - TOP_K
    e_o[...] = jnp.concatenate(e_rows + [jnp.zeros((fill, tt), jnp.int32)], axis=0)
    r_o[...] = jnp.concatenate(r_rows + [jnp.zeros((fill, tt), jnp.int32)], axis=0)
    w_pad = jnp.concatenate([w * scale for w in w_rows] + [jnp.zeros((LANES - TOP_K, tt), F32)], axis=0)
    wt_o[...] = jnp.transpose(w_pad)


def _router(lg_t, router_bias, tt):
    e, t = lg_t.shape
    idx_spec = pl.BlockSpec((8, tt), lambda i: (0, i))
    return pl.pallas_call(
        _router_kernel,
        grid=(t // tt,),
        in_specs=[pl.BlockSpec((e, tt), lambda i: (0, i)), pl.BlockSpec((e, 1), lambda i: (0, 0))],
        out_specs=[idx_spec, idx_spec, pl.BlockSpec((tt, LANES), lambda i: (i, 0)),
                   pl.BlockSpec((e, LANES), lambda i: (0, 0))],
        out_shape=[jax.ShapeDtypeStruct((8, t), jnp.int32), jax.ShapeDtypeStruct((8, t), jnp.int32),
                   jax.ShapeDtypeStruct((t, LANES), F32), jax.ShapeDtypeStruct((e, LANES), F32)],
        scratch_shapes=[pltpu.VMEM((e, LANES), F32)],
        compiler_params=_cparams("arbitrary"),
        name="router",
    )(lg_t, router_bias.reshape(e, 1))


SLOT_STRIDE = 8
ROW_GROUP = 8


def _slot_kernel(pstart_ref, e_ref, r_ref, o_ref):
    e = e_ref[...]
    acc = r_ref[...]
    for x in range(N_EXPERTS):
        acc = acc + jnp.where(e == x, pstart_ref[x], 0)
    tt = acc.shape[1]
    pad = jnp.concatenate([acc, jnp.zeros((LANES - acc.shape[0], tt), jnp.int32)], axis=0)
    o_ref[...] = jnp.transpose(pad)[:, :SLOT_STRIDE]


def _slot_index(pstart, e_idx, rank, tt):
    rows, t = e_idx.shape
    spec = pl.BlockSpec((rows, tt), lambda i, ps: (0, i))
    slot = pl.pallas_call(
        _slot_kernel,
        grid_spec=pltpu.PrefetchScalarGridSpec(
            num_scalar_prefetch=1, grid=(t // tt,), in_specs=[spec, spec],
            out_specs=pl.BlockSpec((tt, SLOT_STRIDE), lambda i, ps: (i, 0))),
        out_shape=jax.ShapeDtypeStruct((t, SLOT_STRIDE), jnp.int32),
        compiler_params=_cparams("parallel"),
        name="slot_index",
    )(pstart, e_idx, rank)
    return slot.reshape(t * SLOT_STRIDE)


def _row_copies(n_tokens, make_copy):
    def body(g, carry):
        base = g * (ROW_GROUP * SLOT_STRIDE)
        for s in range(ROW_GROUP):
            for j in range(TOP_K):
                make_copy(g, s, j, base + (s * SLOT_STRIDE + j)).start(priority=(s * TOP_K + j) % 2)
        return carry

    lax.fori_loop(0, n_tokens // ROW_GROUP, body, 0)


def _dispatch_kernel(pstart_ref, count_ref, slot_ref, hn_ref, xs_hbm, zero_ref, sem):
    td = hn_ref.shape[0] * ROW_GROUP
    bm = zero_ref.shape[0]

    def zero_fill(e, wait):
        end = pstart_ref[e] + count_ref[e]
        aligned = pl.multiple_of((end + 7) // 8 * 8, 8)
        for i in range(7):
            @pl.when(end + i < aligned)
            def _():
                cp = pltpu.make_async_copy(zero_ref.at[pl.ds(0, 1)], xs_hbm.at[pl.ds(end + i, 1)], sem)
                cp.wait() if wait else cp.start()
        cp = pltpu.make_async_copy(zero_ref, xs_hbm.at[pl.ds(aligned, bm)], sem)
        cp.wait() if wait else cp.start()

    @pl.when(pl.program_id(0) == 0)
    def _():
        zero_ref[...] = jnp.zeros(zero_ref.shape, zero_ref.dtype)

        def start(e, carry):
            zero_fill(e, False)
            return carry

        def wait(e, carry):
            zero_fill(e, True)
            return carry

        lax.fori_loop(0, N_EXPERTS, start, 0)
        lax.fori_loop(0, N_EXPERTS, wait, 0)

        last = N_EXPERTS - 1
        tail = (pstart_ref[last] + count_ref[last] + bm - 1) // bm

        def tail_copy(b):
            return pltpu.make_async_copy(zero_ref, xs_hbm.at[pl.ds(pl.multiple_of(b * bm, bm), bm)], sem)

        def tail_start(b, carry):
            tail_copy(b).start()
            return carry

        def tail_wait(b, carry):
            tail_copy(b).wait()
            return carry

        lax.fori_loop(tail, xs_hbm.shape[0] // bm, tail_start, 0)
        lax.fori_loop(tail, xs_hbm.shape[0] // bm, tail_wait, 0)

    _row_copies(td, lambda g, s, j, i: pltpu.make_async_copy(
        hn_ref.at[g, pl.ds(s, 1)], xs_hbm.at[pl.ds(slot_ref[i], 1)], sem))
    pltpu.make_async_copy(xs_hbm.at[pl.ds(0, td * TOP_K)], xs_hbm.at[pl.ds(0, td * TOP_K)], sem).wait()


def _dispatch(pstart, counts, slot, hn, n_slots, td, bm):
    t, d = hn.shape
    return pl.pallas_call(
        _dispatch_kernel,
        grid_spec=pltpu.PrefetchScalarGridSpec(
            num_scalar_prefetch=2, grid=(t // td,),
            in_specs=[pl.BlockSpec((td * SLOT_STRIDE,), lambda i, ps, cn: (i,), memory_space=pltpu.SMEM),
                      pl.BlockSpec((td // ROW_GROUP, ROW_GROUP, d), lambda i, ps, cn: (i, 0, 0))],
            out_specs=pl.BlockSpec(memory_space=pl.ANY),
            scratch_shapes=[pltpu.VMEM((bm, d), hn.dtype), pltpu.SemaphoreType.DMA(())]),
        out_shape=jax.ShapeDtypeStruct((n_slots, d), hn.dtype),
        compiler_params=_cparams("arbitrary"),
        name="dispatch",
    )(pstart, counts, slot, hn.reshape(t // ROW_GROUP, ROW_GROUP, d))


EXPERT_SUB = 256


def _expert_kernel(bexp_ref, bsrc_ref, nused_ref, xs_ref, wg_ref, wu_ref, wd_ref, y_ref):
    del bexp_ref, bsrc_ref
    i = pl.program_id(0)

    @pl.when(i < nused_ref[0])
    def _():
        wg = wg_ref[0].astype(BF16)
        wu = wu_ref[0].astype(BF16)
        wd = wd_ref[0].astype(BF16)
        for r0 in range(0, xs_ref.shape[0], EXPERT_SUB):
            rows = slice(r0, r0 + EXPERT_SUB)
            xb = jnp.concatenate(_unpack_bf16_pairs(xs_ref[rows, :]), axis=1).astype(BF16)
            g = jnp.dot(xb, wg, preferred_element_type=F32)
            u = jnp.dot(xb, wu, preferred_element_type=F32)
            act = (_silu(g) * u).astype(BF16)
            y_ref[rows, :] = _pack_bf16_pairs(jnp.dot(act, wd, preferred_element_type=F32))

    @pl.when(i >= nused_ref[0])
    def _():
        y_ref[...] = jnp.zeros(y_ref.shape, y_ref.dtype)


def _experts(block_exp, block_src, n_used, xs, w_gate, w_up, w_down, bm):
    n_slots, d = xs.shape
    w_map = lambda i, be, bs, nu: (be[i], 0, 0)
    return pl.pallas_call(
        _expert_kernel,
        grid_spec=pltpu.PrefetchScalarGridSpec(
            num_scalar_prefetch=3, grid=(n_slots // bm,),
            in_specs=[pl.BlockSpec((bm, d), lambda i, be, bs, nu: (bs[i], 0)),
                      pl.BlockSpec((1,) + w_gate.shape[1:], w_map),
                      pl.BlockSpec((1,) + w_up.shape[1:], w_map),
                      pl.BlockSpec((1,) + w_down.shape[1:], w_map)],
            out_specs=pl.BlockSpec((bm, d), lambda i, be, bs, nu: (i, 0))),
        out_shape=jax.ShapeDtypeStruct((n_slots, d), xs.dtype),
        compiler_params=_cparams("arbitrary"),
        name="experts",
    )(block_exp, block_src, n_used, xs, w_gate, w_up, w_down)


def _combine_kernel(slot_ref, slot_next_ref, wt_ref, base_ref, gt2_ref, gfin_ref, y_hbm, o_ref, buf, sem):
    tg = base_ref.shape[0]
    i = pl.program_id(0)
    cur = i % 2

    def gather(table, b):
        _row_copies(tg, lambda g, s, j, k: pltpu.make_async_copy(
            y_hbm.at[pl.ds(table[k], 1)], buf.at[b, j, g, pl.ds(s, 1)], sem.at[b]))

    @pl.when(i == 0)
    def _():
        gather(slot_ref, 0)

    @pl.when(i + 1 < pl.num_programs(0))
    def _():
        gather(slot_next_ref, 1 - cur)

    pltpu.make_async_copy(buf.at[cur], buf.at[cur], sem.at[cur]).wait()
    wt = wt_ref[...]
    lo = hi = None
    for j in range(TOP_K):
        ylo, yhi = _unpack_bf16_pairs(buf[cur, j].reshape(tg, buf.shape[-1]))
        w = wt[:, j:j + 1]
        lo = w * ylo if lo is None else lo + w * ylo
        hi = w * yhi if hi is None else hi + w * yhi
    xf = base_ref[...] + gt2_ref[0] * jnp.concatenate([lo, hi], axis=1)
    o_ref[...] = xf * lax.rsqrt(jnp.mean(xf * xf, axis=-1, keepdims=True) + EPS) * gfin_ref[...]


def _combine(slot, wt, base, gt2, g_final, y, seq, tg):
    t, d = base.shape
    tiles = seq // tg
    return pl.pallas_call(
        _combine_kernel,
        grid=(t // tg,),
        in_specs=[pl.BlockSpec((tg * SLOT_STRIDE,), lambda i: (i,), memory_space=pltpu.SMEM),
                  pl.BlockSpec((tg * SLOT_STRIDE,), lambda i: (jnp.minimum(i + 1, t // tg - 1),),
                               memory_space=pltpu.SMEM),
                  pl.BlockSpec((tg, LANES), lambda i: (i, 0)),
                  pl.BlockSpec((tg, d), lambda i: (i, 0)),
                  pl.BlockSpec((1, 1, d), lambda i: (i // tiles, 0, 0)),
                  pl.BlockSpec((1, d), lambda i: (0, 0)),
                  pl.BlockSpec(memory_space=pl.ANY)],
        out_specs=pl.BlockSpec((tg, d), lambda i: (i, 0)),
        out_shape=jax.ShapeDtypeStruct((t, d), F32),
        scratch_shapes=[pltpu.VMEM((2, TOP_K, tg // ROW_GROUP, ROW_GROUP, y.shape[1]), y.dtype),
                        pltpu.SemaphoreType.DMA((2,))],
        compiler_params=_cparams("arbitrary"),
        name="combine",
    )(slot, slot, wt, base, gt2, g_final.reshape(1, d), y)


EXPERT_BLOCK = 512
ROW_TILE = 512
ROUTER_TILE = 512
GATHER_TILE = 1024
SLOT_TILE = 4096


def _moe_plan(counts, n_tokens, bm):
    padded = (counts + bm - 1) // bm * bm
    pad_end = jnp.cumsum(padded)
    pad_start = (pad_end - padded).astype(jnp.int32)
    n_blocks = -(-(n_tokens * TOP_K) // bm) + N_EXPERTS + 1
    n_used = (pad_end[-1] // bm).astype(jnp.int32)
    block_src = jnp.minimum(jnp.arange(n_blocks, dtype=jnp.int32), jnp.maximum(n_used - 1, 0))
    block_exp = jnp.sum((pad_end[None, :] <= (block_src * bm)[:, None]).astype(jnp.int32), axis=1)
    block_exp = jnp.minimum(block_exp, N_EXPERTS - 1)
    return pad_start, block_exp, block_src, n_used.reshape(1), n_blocks * bm


def kernel(x, c, ctx, c_ctx, w_mod, b_mod, g_mix, g_ffn, w_in, w_out, dn_conv_w, dn_a_log, dn_dt_bias,
           dn_norm_g, cf_dw_w, cf_dw_b, cf_ln_g, cf_ln_b, router_w, router_bias, exp_w_gate, exp_w_up,
           exp_w_down, sh_w_gate, sh_w_up, sh_w_down, g_final):
    assert w_mod.shape[0] == 1, "single-layer block: the context stream is never re-read"
    bsz, seq, d = x.shape
    t = bsz * seq
    x2 = x.reshape(t, d)
    st = _mixer_front(x, c, ctx, c_ctx, w_mod[0], b_mod[0], g_mix[0], w_in[0], dn_conv_w[0], dn_a_log[0],
                      dn_dt_bias[0])
    _, _, gt1, sh2, sc2, gt2 = st['lat']
    cfo = _cf_conv(st['y'], cf_dw_w[0], cf_dw_b[0], cf_ln_g[0], cf_ln_b[0], bsz)
    wsgu = jnp.concatenate([sh_w_gate[0], sh_w_up[0]], axis=1).astype(BF16)
    base, hn, lg_t = _mixer_out(st['o_f'], st['o_b'], st['sz'], cfo, x2, gt1, sh2, sc2, gt2,
                                dn_norm_g[0].reshape(1, DN_D), w_out[0].astype(BF16), g_ffn[0].reshape(1, d),
                                router_w[0].T, wsgu, sh_w_down[0].astype(BF16), seq, min(seq, ROW_TILE))
    e_idx, rank, wt, cnt = _router(lg_t, router_bias[0], min(t, ROUTER_TILE))
    counts = cnt[:, 0].astype(jnp.int32)
    pstart, block_exp, block_src, n_used, n_slots = _moe_plan(counts, t, EXPERT_BLOCK)
    slot = _slot_index(pstart, e_idx, rank, min(t, SLOT_TILE))
    xs = _dispatch(pstart, counts, slot, hn, n_slots, min(t, GATHER_TILE), EXPERT_BLOCK)
    y = _experts(block_exp, block_src, n_used, xs, exp_w_gate[0], exp_w_up[0], exp_w_down[0], EXPERT_BLOCK)
    out = _combine(slot, wt, base, gt2, g_final, y, seq, min(seq, GATHER_TILE))
    return out.reshape(bsz, seq, d)
```
